```python
import jax, jax.numpy as jnp
from jax import lax
import numpy as np

D_MODEL = 2048
BATCH = 4
SEQ = 2048
DEPTH = 1
DEC_BATCH = 8
DEC_SEQ = 8
PAST_LEN = 16384
PAGE_SIZE = 128

N_HEADS = D_MODEL // 128
ATTN_WIDTH = D_MODEL // 2
HEAD_DIM = ATTN_WIDTH // N_HEADS
POOL_WIDTH = D_MODEL // 2
POOL_WINDOWS = (2, 4, 8, 16)
N_POOL_GROUPS = len(POOL_WINDOWS)
POOL_GROUP = POOL_WIDTH // N_POOL_GROUPS
POOL_BUF = max(POOL_WINDOWS) - 1
Q_BLOCK = 128
EPS = 1e-6
NEG_INF = -1e30
IN_SIZES = (POOL_WIDTH, POOL_WIDTH, ATTN_WIDTH, ATTN_WIDTH, ATTN_WIDTH, ATTN_WIDTH, N_HEADS, D_MODEL, D_MODEL)
IN_COLS = sum(IN_SIZES)

kernel_name = "hybrid_pool_forgetting_attn_step"


def rms_norm(x, g):
    xf = x.astype(jnp.float32)
    r = lax.rsqrt(jnp.mean(xf * xf, axis=-1, keepdims=True) + EPS)
    return (xf * r * g.astype(jnp.float32)).astype(x.dtype)


def pool_mixer(u, buf, pos0, w_pool_map, pool_scale):
    B, L, _ = u.shape
    z = jnp.concatenate([buf.astype(u.dtype), u], axis=1).astype(jnp.float32)
    c = jnp.concatenate([jnp.zeros((B, 1, POOL_WIDTH), jnp.float32), jnp.cumsum(z, axis=1)], axis=1)
    pos = pos0 + jnp.arange(L)
    means = []
    for g, w in enumerate(POOL_WINDOWS):
        sl = slice(g * POOL_GROUP, (g + 1) * POOL_GROUP)
        end = c[:, POOL_BUF + 1:POOL_BUF + 1 + L, sl]
        start = c[:, POOL_BUF + 1 - w:POOL_BUF + 1 - w + L, sl]
        cnt = jnp.minimum(pos + 1, w).astype(jnp.float32)[None, :, None]
        means.append((end - start) / cnt)
    d = (jnp.concatenate(means, axis=-1) - u.astype(jnp.float32)).astype(u.dtype)
    d = d.reshape(B, L, N_POOL_GROUPS, POOL_GROUP)
    mixed = jnp.einsum('blgc,gcd->blgd', d, w_pool_map).reshape(B, L, POOL_WIDTH)
    return mixed * pool_scale


def forgetting_attention(q, k, v, qb, kb, q_pos, k_pos):
    B, Lq, H, Dh = q.shape
    blk = min(Q_BLOCK, Lq)
    nb = -(-Lq // blk)
    pad = nb * blk - Lq
    qp = jnp.pad(q, ((0, 0), (0, pad), (0, 0), (0, 0)))
    qbp = jnp.pad(qb, ((0, 0), (0, pad), (0, 0)))
    posp = jnp.pad(q_pos, (0, pad), mode='edge')
    qs = qp.reshape(B, nb, blk, H, Dh).transpose(1, 0, 2, 3, 4)
    qbs = qbp.reshape(B, nb, blk, H).transpose(1, 0, 3, 2)
    ps = posp.reshape(nb, blk)
    kb_t = kb.transpose(0, 2, 1)[:, :, None, :]
    scale = HEAD_DIM ** -0.5

    def block(args):
        qblk, qbblk, pblk = args
        s = jnp.einsum('bqhd,bkhd->bhqk', qblk, k, preferred_element_type=jnp.float32) * scale
        s = s + qbblk[..., None] + kb_t
        mask = k_pos[None, :] <= pblk[:, None]
        s = jnp.where(mask[None, None], s, NEG_INF)
        p = jax.nn.softmax(s, axis=-1)
        return jnp.einsum('bhqk,bkhd->bqhd', p.astype(v.dtype), v)

    o = lax.map(block, (qs, qbs, ps))
    return o.transpose(1, 0, 2, 3, 4).reshape(B, nb * blk, H, Dh)[:, :Lq]


def mixer_layer(x, pool_buf, pos0, past, norm_gain, w_in, b_f, q_gain, k_gain,
                w_pool_map, pool_scale, w_up_pool, w_up_attn, w_out):
    B, L, _ = x.shape
    xn = rms_norm(x, norm_gain)
    h = xn @ w_in
    splits = [int(s) for s in np.cumsum(IN_SIZES)[:-1]]
    u, gp, q, k, v, ga, fl, m_a, m_b = jnp.split(h, splits, axis=-1)
    q = rms_norm(q.reshape(B, L, N_HEADS, HEAD_DIM), q_gain)
    k = rms_norm(k.reshape(B, L, N_HEADS, HEAD_DIM), k_gain)
    v = v.reshape(B, L, N_HEADS, HEAD_DIM)
    logf = jax.nn.log_sigmoid(fl.astype(jnp.float32) + b_f.astype(jnp.float32))
    f_new = jnp.cumsum(logf, axis=1)
    if past is None:
        qb, kb, K, V = f_new, -f_new, k, v
        q_pos = jnp.arange(L)
        k_pos = q_pos
    else:
        past_k, past_v, past_lf = past
        P = past_k.shape[1]
        plf = past_lf.astype(jnp.float32)
        rest = lax.cumsum(plf, axis=1, reverse=True) - plf
        qb = f_new
        kb = jnp.concatenate([rest, -f_new], axis=1)
        K = jnp.concatenate([past_k.astype(k.dtype), k], axis=1)
        V = jnp.concatenate([past_v.astype(v.dtype), v], axis=1)
        q_pos = P + jnp.arange(L)
        k_pos = jnp.arange(P + L)
    att = forgetting_attention(q, K, V, qb, kb, q_pos, k_pos).reshape(B, L, ATTN_WIDTH)
    branch_b = att * jax.nn.silu(ga)
    branch_a = pool_mixer(u, pool_buf, pos0, w_pool_map, pool_scale) * jax.nn.silu(gp)
    new_buf = jnp.concatenate([pool_buf.astype(u.dtype), u], axis=1)[:, -POOL_BUF:]
    merged = jax.nn.sigmoid(m_a) * (branch_a @ w_up_pool) + jax.nn.sigmoid(m_b) * (branch_b @ w_up_attn)
    y = x + merged @ w_out
    return y, k, v, logf, new_buf


def setup_inputs(seed: int = 0) -> dict:
    key = jax.random.key(seed)
    ks = jax.random.split(key, 20)
    n_pages = PAST_LEN // PAGE_SIZE
    n_used = DEC_BATCH * n_pages
    n_phys = n_used + n_used // 4
    f32 = jnp.float32
    x_prompt = jax.random.normal(ks[0], (BATCH, SEQ, D_MODEL), f32)
    x_sample = jax.random.normal(ks[1], (DEC_BATCH, DEC_SEQ, D_MODEL), f32)
    cache_k = jax.random.normal(ks[2], (DEPTH, n_phys, PAGE_SIZE, N_HEADS, HEAD_DIM), f32)
    cache_v = jax.random.normal(ks[3], (DEPTH, n_phys, PAGE_SIZE, N_HEADS, HEAD_DIM), f32)
    cache_logf = jax.nn.log_sigmoid(jax.random.uniform(ks[4], (DEPTH, n_phys, PAGE_SIZE, N_HEADS), f32, 3.0, 7.0))
    state_pool = jax.random.normal(ks[5], (DEPTH, DEC_BATCH, POOL_BUF, POOL_WIDTH), f32)
    page_table = jax.random.permutation(ks[6], n_phys)[:n_used].reshape(DEC_BATCH, n_pages).astype(jnp.int32)
    norm_gain = 1.0 + 0.02 * jax.random.normal(ks[7], (DEPTH, D_MODEL), f32)
    w_in = jax.random.normal(ks[8], (DEPTH, D_MODEL, IN_COLS), f32) * D_MODEL ** -0.5
    b_f = jax.random.uniform(ks[9], (DEPTH, N_HEADS), f32, 3.0, 6.0)
    q_norm_gain = 1.0 + 0.02 * jax.random.normal(ks[10], (DEPTH, HEAD_DIM), f32)
    k_norm_gain = 1.0 + 0.02 * jax.random.normal(ks[11], (DEPTH, HEAD_DIM), f32)
    w_pool_map = jax.random.normal(ks[12], (DEPTH, N_POOL_GROUPS, POOL_GROUP, POOL_GROUP), f32) * POOL_GROUP ** -0.5
    pool_scale = 1.0 + 0.02 * jax.random.normal(ks[13], (DEPTH, POOL_WIDTH), f32)
    w_up_pool = jax.random.normal(ks[14], (DEPTH, POOL_WIDTH, D_MODEL), f32) * POOL_WIDTH ** -0.5
    w_up_attn = jax.random.normal(ks[15], (DEPTH, ATTN_WIDTH, D_MODEL), f32) * ATTN_WIDTH ** -0.5
    w_out = jax.random.normal(ks[16], (DEPTH, D_MODEL, D_MODEL), f32) * D_MODEL ** -0.5
    return {"x_prompt": x_prompt, "x_sample": x_sample, "cache_k": cache_k, "cache_v": cache_v,
            "cache_logf": cache_logf, "state_pool": state_pool, "page_table": page_table,
            "norm_gain": norm_gain, "w_in": w_in, "b_f": b_f, "q_norm_gain": q_norm_gain,
            "k_norm_gain": k_norm_gain, "w_pool_map": w_pool_map, "pool_scale": pool_scale,
            "w_up_pool": w_up_pool, "w_up_attn": w_up_attn, "w_out": w_out}


def reference(x_prompt, x_sample, cache_k, cache_v, cache_logf, state_pool, page_table,
              norm_gain, w_in, b_f, q_norm_gain, k_norm_gain, w_pool_map, pool_scale,
              w_up_pool, w_up_attn, w_out):
    dec_b, n_pages = page_table.shape
    past_len = n_pages * PAGE_SIZE
    yp, ys = x_prompt, x_sample
    kp_l, vp_l, lfp_l, bp_l, ks_l, vs_l, lfs_l, bs_l = [], [], [], [], [], [], [], []
    for l in range(DEPTH):
        params = (norm_gain[l], w_in[l], b_f[l], q_norm_gain[l], k_norm_gain[l],
                  w_pool_map[l], pool_scale[l], w_up_pool[l], w_up_attn[l], w_out[l])
        zero_buf = jnp.zeros((x_prompt.shape[0], POOL_BUF, POOL_WIDTH), x_prompt.dtype)
        yp, kp, vp, lfp, bp = mixer_layer(yp, zero_buf, 0, None, *params)
        past_k = cache_k[l][page_table].reshape(dec_b, past_len, N_HEADS, HEAD_DIM)
        past_v = cache_v[l][page_table].reshape(dec_b, past_len, N_HEADS, HEAD_DIM)
        past_lf = cache_logf[l][page_table].reshape(dec_b, past_len, N_HEADS)
        ys, ksm, vsm, lfs, bs = mixer_layer(ys, state_pool[l], past_len, (past_k, past_v, past_lf), *params)
        kp_l.append(kp); vp_l.append(vp); lfp_l.append(lfp); bp_l.append(bp)
        ks_l.append(ksm); vs_l.append(vsm); lfs_l.append(lfs); bs_l.append(bs)
    return (yp, ys, jnp.stack(kp_l), jnp.stack(vp_l), jnp.stack(lfp_l), jnp.stack(bp_l),
            jnp.stack(ks_l), jnp.stack(vs_l), jnp.stack(lfs_l), jnp.stack(bs_l))
```

```python
import functools

import jax
import jax.numpy as jnp
import numpy as np
from jax import lax
from jax.experimental import pallas as pl
from jax.experimental.pallas import tpu as pltpu

F32 = jnp.float32
BF16 = jnp.bfloat16

HEAD_DIM = 64
POOL_WINDOWS = (2, 4, 8, 16)
POOL_BUF = 15
PAGE_SIZE = 128
EPS = 1e-6
NEG_INF = -1e30
ATTN_SCALE = HEAD_DIM ** -0.5

V7X_LANES = 128
V7X_SUBLANES = 8
V7X_VMEM_LIMIT_BYTES = 56 * 1024 * 1024

HALO = 16


def _cparams(n_grid_axes):
    return pltpu.CompilerParams(
        dimension_semantics=("arbitrary",) * n_grid_axes,
        vmem_limit_bytes=V7X_VMEM_LIMIT_BYTES,
    )


def _split3(x):
    hi = x.astype(BF16)
    r1 = x - hi.astype(F32)
    mid = r1.astype(BF16)
    lo = (r1 - mid.astype(F32)).astype(BF16)
    return hi, mid, lo


def _dot(a, b):
    return jnp.dot(a, b, preferred_element_type=F32)


def _dot_nt(a, b):
    return lax.dot_general(a, b, (((1,), (1,)), ((), ())), preferred_element_type=F32)


def _sigmoid(x):
    return 1.0 / (1.0 + jnp.exp(-x))


def _silu(x):
    return x * _sigmoid(x)


def _inproj_kernel(x_ref, ng_ref, w_ref, wfl_ref, bf_ref, gain_ref, bd_ref, h_ref, lf_ref, xn_sc,
                   *, qk_j0, qk_j1):
    j = pl.program_id(1)

    @pl.when(j == 0)
    def _():
        x = x_ref[...]
        ms = jnp.mean(x * x, axis=-1, keepdims=True)
        xn = x * lax.rsqrt(ms + EPS) * ng_ref[...]
        xn_sc[...] = xn.astype(BF16)
        z = _dot(xn_sc[...], wfl_ref[...]) + bf_ref[...]
        lf_ref[...] = jnp.minimum(z, 0.0) - jnp.log1p(jnp.exp(-jnp.abs(z)))

    acc = _dot(xn_sc[...], w_ref[...])
    is_qk = jnp.logical_and(j >= qk_j0, j < qk_j1)

    @pl.when(is_qk)
    def _():
        sq = acc * acc
        hi = sq.astype(BF16)
        lo = (sq - hi.astype(F32)).astype(BF16)
        ss = _dot(hi, bd_ref[...]) + _dot(lo, bd_ref[...])
        r = lax.rsqrt(ss * (1.0 / HEAD_DIM) + EPS)
        h_ref[...] = acc * r * gain_ref[...]

    @pl.when(jnp.logical_not(is_qk))
    def _():
        h_ref[...] = acc


def _inproj(x2d, ng, wc, wfl, bfp, gains, bd, *, tm, tn):
    m, d = x2d.shape
    n = wc.shape[1]
    nj = n // tn
    qk_j0, qk_j1 = 2048 // tn, 4096 // tn
    kern = functools.partial(_inproj_kernel, qk_j0=qk_j0, qk_j1=qk_j1)
    return pl.pallas_call(
        kern,
        grid=(m // tm, nj),
        in_specs=[
            pl.BlockSpec((tm, d), lambda i, j: (i, 0)),
            pl.BlockSpec((1, d), lambda i, j: (0, 0)),
            pl.BlockSpec((d, tn), lambda i, j: (0, j)),
            pl.BlockSpec((d, V7X_LANES), lambda i, j: (0, 0)),
            pl.BlockSpec((1, V7X_LANES), lambda i, j: (0, 0)),
            pl.BlockSpec((None, 1, tn), lambda i, j: (jnp.clip(j - qk_j0, 0, qk_j1 - qk_j0 - 1), 0, 0)),
            pl.BlockSpec((tn, tn), lambda i, j: (0, 0)),
        ],
        out_specs=[
            pl.BlockSpec((tm, tn), lambda i, j: (i, j)),
            pl.BlockSpec((tm, V7X_LANES), lambda i, j: (i, 0)),
        ],
        out_shape=[
            jax.ShapeDtypeStruct((m, n), F32),
            jax.ShapeDtypeStruct((m, V7X_LANES), F32),
        ],
        scratch_shapes=[pltpu.VMEM((tm, d), BF16)],
        compiler_params=_cparams(2),
        name="inproj",
    )(x2d, ng, wc, wfl, bfp, gains, bd)


def _fcum_kernel(lf_ref, fcol_ref, ft_ref, *, seq, n_heads):
    c = V7X_LANES
    row = lax.broadcasted_iota(jnp.int32, (c, c), 0)
    col = lax.broadcasted_iota(jnp.int32, (c, c), 1)
    tri = jnp.where(col <= row, 1.0, 0.0).astype(BF16)
    carry = jnp.zeros((1, c), F32)
    for ci in range(seq // c):
        x = lf_ref[ci * c:(ci + 1) * c, :]
        hi, mid, lo = _split3(x)
        fc = (_dot(tri, hi) + _dot(tri, mid)) + _dot(tri, lo) + carry
        fcol_ref[ci * c:(ci + 1) * c, :] = fc
        ft_ref[:, ci * c:(ci + 1) * c] = fc.T[:n_heads, :]
        carry = fc[c - 1:c, :]


def _fcum(lf2d, *, batch, seq, n_heads):
    kern = functools.partial(_fcum_kernel, seq=seq, n_heads=n_heads)
    return pl.pallas_call(
        kern,
        grid=(batch,),
        in_specs=[pl.BlockSpec((seq, V7X_LANES), lambda b: (b, 0))],
        out_specs=[
            pl.BlockSpec((seq, V7X_LANES), lambda b: (b, 0)),
            pl.BlockSpec((None, n_heads, seq), lambda b: (b, 0, 0)),
        ],
        out_shape=[
            jax.ShapeDtypeStruct((batch * seq, V7X_LANES), F32),
            jax.ShapeDtypeStruct((batch, n_heads, seq), F32),
        ],
        compiler_params=_cparams(1),
        name="forget_cumsum",
    )(lf2d)


def _attn_kernel(q_ref, k_ref, v_ref, fc_ref, ft_ref, o_ref, *, tq):
    p = pl.program_id(1)
    i = pl.program_id(2)
    lane = lax.broadcasted_iota(jnp.int32, (1, V7X_LANES), 1)
    is_lo = lane < HEAD_DIM
    q = q_ref[...] * ATTN_SCALE
    q0 = jnp.where(is_lo, q, 0.0).astype(BF16)
    q1 = jnp.where(is_lo, 0.0, q).astype(BF16)
    fc = fc_ref[...]
    fq0 = jnp.sum(jnp.where(lane == 2 * p, fc, 0.0), axis=1, keepdims=True)
    fq1 = jnp.sum(jnp.where(lane == 2 * p + 1, fc, 0.0), axis=1, keepdims=True)
    row = lax.broadcasted_iota(jnp.int32, (tq, tq), 0)
    col = lax.broadcasted_iota(jnp.int32, (tq, tq), 1)
    causal = col <= row

    def step(j, carry, masked):
        m0, l0, m1, l1, acc = carry
        ks = pl.multiple_of(j * tq, tq)
        kt = k_ref[pl.ds(ks, tq), :].astype(BF16)
        vt = v_ref[pl.ds(ks, tq), :].astype(BF16)
        fr = ft_ref[:, pl.ds(ks, tq)]
        s0 = _dot_nt(q0, kt) + (fq0 - fr[0:1, :])
        s1 = _dot_nt(q1, kt) + (fq1 - fr[1:2, :])
        if masked:
            s0 = jnp.where(causal, s0, NEG_INF)
            s1 = jnp.where(causal, s1, NEG_INF)
        n0 = jnp.maximum(m0, jnp.max(s0, axis=1, keepdims=True))
        n1 = jnp.maximum(m1, jnp.max(s1, axis=1, keepdims=True))
        a0 = jnp.exp(m0 - n0)
        a1 = jnp.exp(m1 - n1)
        p0 = jnp.exp(s0 - n0)
        p1 = jnp.exp(s1 - n1)
        l0 = a0 * l0 + jnp.sum(p0, axis=1, keepdims=True)
        l1 = a1 * l1 + jnp.sum(p1, axis=1, keepdims=True)
        pv0 = _dot(p0.astype(BF16), vt)
        pv1 = _dot(p1.astype(BF16), vt)
        acc = acc * jnp.where(is_lo, a0, a1) + jnp.where(is_lo, pv0, pv1)
        return n0, l0, n1, l1, acc

    init = (jnp.full((tq, 1), NEG_INF, F32), jnp.zeros((tq, 1), F32),
            jnp.full((tq, 1), NEG_INF, F32), jnp.zeros((tq, 1), F32),
            jnp.zeros((tq, V7X_LANES), F32))
    carry = lax.fori_loop(0, i, lambda j, c: step(j, c, False), init)
    _, l0, _, l1, acc = step(i, carry, True)
    o_ref[...] = acc / jnp.where(is_lo, l0, l1)


def _prompt_attention(h, fcol, ft_pairs, *, batch, seq, n_heads, tq, q_col0, k_col0, v_col0):
    n_pairs = n_heads // 2
    qt = seq // tq
    lanes = V7X_LANES
    kern = functools.partial(_attn_kernel, tq=tq)
    return pl.pallas_call(
        kern,
        grid=(batch, n_pairs, qt),
        in_specs=[
            pl.BlockSpec((tq, lanes), lambda b, p, i: (b * qt + i, q_col0 // lanes + p)),
            pl.BlockSpec((seq, lanes), lambda b, p, i: (b, k_col0 // lanes + p)),
            pl.BlockSpec((seq, lanes), lambda b, p, i: (b, v_col0 // lanes + p)),
            pl.BlockSpec((tq, lanes), lambda b, p, i: (b * qt + i, 0)),
            pl.BlockSpec((None, None, 2, seq), lambda b, p, i: (b, p, 0, 0)),
        ],
        out_specs=pl.BlockSpec((tq, lanes), lambda b, p, i: (b * qt + i, p)),
        out_shape=jax.ShapeDtypeStruct((batch * seq, n_heads * HEAD_DIM), F32),
        compiler_params=_cparams(3),
        name="prompt_attention",
    )(h, h, h, fcol, ft_pairs)


def _stride_scan_rev(x, stride):
    n = x.shape[-1]
    lane = lax.broadcasted_iota(jnp.int32, x.shape, x.ndim - 1)
    k = stride
    while k < n:
        shifted = pltpu.roll(x, n - k, x.ndim - 1)
        x = x + jnp.where(lane < n - k, shifted, 0.0)
        k *= 2
    return x


def _stride_scan_fwd(x, stride):
    n = x.shape[-1]
    lane = lax.broadcasted_iota(jnp.int32, x.shape, x.ndim - 1)
    k = stride
    while k < n:
        shifted = pltpu.roll(x, k, x.ndim - 1)
        x = x + jnp.where(lane >= k, shifted, 0.0)
        k *= 2
    return x


def _stride_total(x, stride):
    n = x.shape[-1]
    k = stride
    while k < n:
        x = x + pltpu.roll(x, k, x.ndim - 1)
        k *= 2
    return x


def _decode_kernel(pt_ref, q_ref, kc_ref, vc_ref, lf_ref, kn_ref, vn_ref, lfn_ref, mask_ref, maskn_ref, diag_ref,
                   o_ref, m_sc, l_sc, acc_sc, carry_sc, qb_sc, *, n_heads, n_new):
    del pt_ref
    j = pl.program_id(1)
    rows = n_heads * n_new
    q = (q_ref[...] * ATTN_SCALE).astype(BF16)

    def attend(k2, v2, bias):
        s = _dot_nt(q, k2.astype(BF16)) + bias
        m_prev = m_sc[...]
        m_new = jnp.maximum(m_prev, jnp.max(s, axis=1, keepdims=True))
        alpha = jnp.exp(m_prev - m_new)
        p = jnp.exp(s - m_new)
        l_sc[...] = alpha * l_sc[...] + jnp.sum(p, axis=1, keepdims=True)
        acc_sc[...] = alpha * acc_sc[...] + _dot(p.astype(BF16), v2.astype(BF16))
        m_sc[...] = m_new

    @pl.when(j == 0)
    def _():
        m_sc[...] = jnp.full(m_sc.shape, NEG_INF, F32)
        l_sc[...] = jnp.zeros(l_sc.shape, F32)
        acc_sc[...] = jnp.zeros(acc_sc.shape, F32)
        carry_sc[...] = jnp.zeros(carry_sc.shape, F32)
        f_new = _stride_scan_fwd(lfn_ref[...], n_heads)
        qb = jnp.sum(diag_ref[...] * f_new, axis=1, keepdims=True)
        qb_sc[...] = qb
        kn = kn_ref[...].reshape(n_new * n_heads, HEAD_DIM)
        vn = vn_ref[...].reshape(n_new * n_heads, HEAD_DIM)
        attend(kn, vn, (qb - f_new) + maskn_ref[...])

    lf = lf_ref[...]
    incl = _stride_scan_rev(lf, n_heads)
    carry = carry_sc[...]
    kb = (incl - lf) + carry
    carry_sc[...] = carry + _stride_total(lf, n_heads)
    kc = kc_ref[...].reshape(PAGE_SIZE * n_heads, HEAD_DIM)
    vc = vc_ref[...].reshape(PAGE_SIZE * n_heads, HEAD_DIM)
    attend(kc, vc, (qb_sc[...] + kb) + mask_ref[...])

    @pl.when(j == pl.num_programs(1) - 1)
    def _():
        o_ref[...] = acc_sc[...] / l_sc[...]


def _decode_attention(page_table, q_rows, cache_k, cache_v, lf_flat, k_new, v_new, lf_new, mask, mask_new, diag,
                      *, n_heads, n_new):
    dec_b, n_pages = page_table.shape
    rows = n_heads * n_new
    page_lanes = PAGE_SIZE * n_heads
    new_lanes = n_new * n_heads
    last = n_pages - 1
    kern = functools.partial(_decode_kernel, n_heads=n_heads, n_new=n_new)
    grid_spec = pltpu.PrefetchScalarGridSpec(
        num_scalar_prefetch=1,
        grid=(dec_b, n_pages),
        in_specs=[
            pl.BlockSpec((None, rows, HEAD_DIM), lambda b, j, pt: (b, 0, 0)),
            pl.BlockSpec((None, None, PAGE_SIZE, n_heads, HEAD_DIM), lambda b, j, pt: (0, pt[b, last - j], 0, 0, 0)),
            pl.BlockSpec((None, None, PAGE_SIZE, n_heads, HEAD_DIM), lambda b, j, pt: (0, pt[b, last - j], 0, 0, 0)),
            pl.BlockSpec((None, 1, page_lanes), lambda b, j, pt: (pt[b, last - j], 0, 0)),
            pl.BlockSpec((None, n_new, n_heads, HEAD_DIM), lambda b, j, pt: (b, 0, 0, 0)),
            pl.BlockSpec((None, n_new, n_heads, HEAD_DIM), lambda b, j, pt: (b, 0, 0, 0)),
            pl.BlockSpec((None, 1, new_lanes), lambda b, j, pt: (b, 0, 0)),
            pl.BlockSpec((rows, page_lanes), lambda b, j, pt: (0, 0)),
            pl.BlockSpec((rows, new_lanes), lambda b, j, pt: (0, 0)),
            pl.BlockSpec((rows, new_lanes), lambda b, j, pt: (0, 0)),
        ],
        out_specs=pl.BlockSpec((None, rows, HEAD_DIM), lambda b, j, pt: (b, 0, 0)),
        scratch_shapes=[
            pltpu.VMEM((rows, 1), F32),
            pltpu.VMEM((rows, 1), F32),
            pltpu.VMEM((rows, HEAD_DIM), F32),
            pltpu.VMEM((1, page_lanes), F32),
            pltpu.VMEM((rows, 1), F32),
        ],
    )
    return pl.pallas_call(
        kern,
        grid_spec=grid_spec,
        out_shape=jax.ShapeDtypeStruct((dec_b, rows, HEAD_DIM), F32),
        compiler_params=_cparams(2),
        name="decode_attention",
    )(page_table, q_rows, cache_k, cache_v, lf_flat, k_new, v_new, lf_new, mask, mask_new, diag)


def _pool_diff(z, u, pos, group_w):
    s = z
    k = 1
    while k < group_w:
        s = s + pltpu.roll(s, k, 0)
        k *= 2
    cnt = jnp.minimum(pos + 1, group_w).astype(F32)
    return s[HALO:, :] / cnt - u


def _mix_tail(d_groups, gp, ga, att, ma, mb, x, wpool_ref, ps_ref, wup_ref, wua_ref, wout_ref):
    g = d_groups[0].shape[1]
    mixed = [_dot(d.astype(BF16), wpool_ref[gi]) for gi, d in enumerate(d_groups)]
    mixed = jnp.concatenate(mixed, axis=1)
    del g
    branch_a = (mixed * ps_ref[...]) * _silu(gp)
    branch_b = att * _silu(ga)
    up_a = _dot(branch_a.astype(BF16), wup_ref[...])
    up_b = _dot(branch_b.astype(BF16), wua_ref[...])
    merged = _sigmoid(ma) * up_a + _sigmoid(mb) * up_b
    return x + _dot(merged.astype(BF16), wout_ref[...])


def _out_prompt_kernel(u_ref, halo_ref, gp_ref, ga_ref, ma_ref, mb_ref, att_ref, x_ref,
                       wpool_ref, ps_ref, wup_ref, wua_ref, wout_ref, y_ref, *, tm, seq):
    i = pl.program_id(0)
    pos0 = (i * tm) % seq
    u = u_ref[...]
    halo = jnp.where(pos0 == 0, 0.0, halo_ref[...])
    z = jnp.concatenate([halo, u], axis=0)
    pos = pos0 + lax.broadcasted_iota(jnp.int32, (tm, 1), 0)
    gw = u.shape[1] // len(POOL_WINDOWS)
    d_groups = [_pool_diff(z[:, gi * gw:(gi + 1) * gw], u[:, gi * gw:(gi + 1) * gw], pos, w)
                for gi, w in enumerate(POOL_WINDOWS)]
    y_ref[...] = _mix_tail(d_groups, gp_ref[...], ga_ref[...], att_ref[...], ma_ref[...], mb_ref[...], x_ref[...],
                           wpool_ref, ps_ref, wup_ref, wua_ref, wout_ref)


def _out_sample_kernel(z_ref, gp_ref, ga_ref, ma_ref, mb_ref, att_ref, x_ref,
                       wpool_ref, ps_ref, wup_ref, wua_ref, wout_ref, y_ref, *, dec_b, n_new, pos0):
    gw = z_ref.shape[2] // len(POOL_WINDOWS)
    pos = pos0 + lax.broadcasted_iota(jnp.int32, (n_new, 1), 0)
    per_seq = []
    for b in range(dec_b):
        z = z_ref[b]
        u = z[HALO:, :]
        per_seq.append([_pool_diff(z[:, gi * gw:(gi + 1) * gw], u[:, gi * gw:(gi + 1) * gw], pos, w)
                        for gi, w in enumerate(POOL_WINDOWS)])
    d_groups = [jnp.concatenate([per_seq[b][gi] for b in range(dec_b)], axis=0) for gi in range(len(POOL_WINDOWS))]
    y_ref[...] = _mix_tail(d_groups, gp_ref[...], ga_ref[...], att_ref[...], ma_ref[...], mb_ref[...], x_ref[...],
                           wpool_ref, ps_ref, wup_ref, wua_ref, wout_ref)


def _weight_specs(n_axes, pw, gw, dm):
    zeros = (0,) * 3
    idx3 = {1: lambda i: zeros}[n_axes]
    idx2 = {1: lambda i: (0, 0)}[n_axes]
    one = pl.Buffered(1)
    return [
        pl.BlockSpec((len(POOL_WINDOWS), gw, gw), idx3, pipeline_mode=one),
        pl.BlockSpec((1, pw), idx2, pipeline_mode=one),
        pl.BlockSpec((pw, dm), idx2, pipeline_mode=one),
        pl.BlockSpec((pw, dm), idx2, pipeline_mode=one),
        pl.BlockSpec((dm, dm), idx2, pipeline_mode=one),
    ]


def _out_prompt(h, att, x2d, wpool, ps, wup, wua, wout, *, seq, tm, cols):
    m, dm = x2d.shape
    pw = att.shape[1]
    gw = pw // len(POOL_WINDOWS)
    hb = tm // HALO
    kern = functools.partial(_out_prompt_kernel, tm=tm, seq=seq)
    return pl.pallas_call(
        kern,
        grid=(m // tm,),
        in_specs=[
            pl.BlockSpec((tm, pw), lambda i: (i, cols["u"] // pw)),
            pl.BlockSpec((HALO, pw), lambda i: (jnp.maximum(i * hb - 1, 0), cols["u"] // pw)),
            pl.BlockSpec((tm, pw), lambda i: (i, cols["gp"] // pw)),
            pl.BlockSpec((tm, pw), lambda i: (i, cols["ga"] // pw)),
            pl.BlockSpec((tm, dm), lambda i: (i, cols["ma"] // dm)),
            pl.BlockSpec((tm, dm), lambda i: (i, cols["mb"] // dm)),
            pl.BlockSpec((tm, pw), lambda i: (i, 0)),
            pl.BlockSpec((tm, dm), lambda i: (i, 0)),
        ] + _weight_specs(1, pw, gw, dm),
        out_specs=pl.BlockSpec((tm, dm), lambda i: (i, 0)),
        out_shape=jax.ShapeDtypeStruct((m, dm), F32),
        compiler_params=_cparams(1),
        name="out_prompt",
    )(h, h, h, h, h, h, att, x2d, wpool, ps, wup, wua, wout)


def _out_sample(z, h, att, x2d, wpool, ps, wup, wua, wout, *, dec_b, n_new, pos0, cols):
    m, dm = x2d.shape
    pw = att.shape[1]
    gw = pw // len(POOL_WINDOWS)
    kern = functools.partial(_out_sample_kernel, dec_b=dec_b, n_new=n_new, pos0=pos0)
    return pl.pallas_call(
        kern,
        grid=(1,),
        in_specs=[
            pl.BlockSpec((dec_b, HALO + n_new, pw), lambda i: (0, 0, 0)),
            pl.BlockSpec((m, pw), lambda i: (0, cols["gp"] // pw)),
            pl.BlockSpec((m, pw), lambda i: (0, cols["ga"] // pw)),
            pl.BlockSpec((m, dm), lambda i: (0, cols["ma"] // dm)),
            pl.BlockSpec((m, dm), lambda i: (0, cols["mb"] // dm)),
            pl.BlockSpec((m, pw), lambda i: (0, 0)),
            pl.BlockSpec((m, dm), lambda i: (0, 0)),
        ] + _weight_specs(1, pw, gw, dm),
        out_specs=pl.BlockSpec((m, dm), lambda i: (0, 0)),
        out_shape=jax.ShapeDtypeStruct((m, dm), F32),
        compiler_params=_cparams(1),
        name="out_sample",
    )(z, h, h, h, h, att, x2d, wpool, ps, wup, wua, wout)


def _decode_masks(n_heads, n_new):
    rows = np.arange(n_heads * n_new)
    row_head, row_tok = rows // n_new, rows % n_new
    page_lane = np.arange(PAGE_SIZE * n_heads)
    mask = np.where((page_lane % n_heads)[None, :] == row_head[:, None], 0.0, NEG_INF).astype(np.float32)
    new_lane = np.arange(n_new * n_heads)
    lane_tok, lane_head = new_lane // n_heads, new_lane % n_heads
    same_head = lane_head[None, :] == row_head[:, None]
    mask_new = np.where(same_head & (lane_tok[None, :] <= row_tok[:, None]), 0.0, NEG_INF).astype(np.float32)
    diag = (same_head & (lane_tok[None, :] == row_tok[:, None])).astype(np.float32)
    return jnp.asarray(mask), jnp.asarray(mask_new), jnp.asarray(diag)


def kernel(x_prompt, x_sample, cache_k, cache_v, cache_logf, state_pool, page_table, norm_gain, w_in, b_f,
           q_norm_gain, k_norm_gain, w_pool_map, pool_scale, w_up_pool, w_up_attn, w_out):
    batch, seq, dm = x_prompt.shape
    dec_b, n_new, _ = x_sample.shape
    depth = w_in.shape[0]
    assert depth == 1
    n_heads = b_f.shape[1]
    aw = n_heads * HEAD_DIM
    pw = w_up_pool.shape[1]
    n_pages = page_table.shape[1]
    past_len = n_pages * PAGE_SIZE
    assert pw == aw and dm == 2 * pw

    cols = {"u": 0, "gp": pw, "q": 2 * pw, "k": 2 * pw + aw, "v": 2 * pw + 2 * aw, "ga": 2 * pw + 3 * aw}
    n_main = 2 * pw + 4 * aw
    cols["ma"] = n_main
    cols["mb"] = n_main + dm
    w = w_in[0]
    wc = jnp.concatenate([w[:, :n_main], w[:, n_main + n_heads:]], axis=1).astype(BF16)
    wfl = jnp.pad(w[:, n_main:n_main + n_heads], ((0, 0), (0, V7X_LANES - n_heads))).astype(BF16)
    bfp = jnp.pad(b_f, ((0, 0), (0, V7X_LANES - n_heads)))
    tn = 512
    reps = tn // HEAD_DIM
    gains = jnp.stack([jnp.tile(q_norm_gain[0], reps)] * (aw // tn) + [jnp.tile(k_norm_gain[0], reps)] * (aw // tn))
    gains = gains.reshape(2 * aw // tn, 1, tn)
    seg = np.arange(tn) // HEAD_DIM
    bd = jnp.asarray((seg[:, None] == seg[None, :]).astype(np.float32)).astype(BF16)
    wpool = w_pool_map[0].astype(BF16)
    wup = w_up_pool[0].astype(BF16)
    wua = w_up_attn[0].astype(BF16)
    wout = w_out[0].astype(BF16)

    xp2 = x_prompt.reshape(batch * seq, dm)
    hp, lfp = _inproj(xp2, norm_gain, wc, wfl, bfp, gains, bd, tm=1024, tn=tn)
    fcol, ft = _fcum(lfp, batch=batch, seq=seq, n_heads=n_heads)
    ft_pairs = ft.reshape(batch, n_heads // 2, 2, seq)
    att_p = _prompt_attention(hp, fcol, ft_pairs, batch=batch, seq=seq, n_heads=n_heads, tq=256,
                              q_col0=cols["q"], k_col0=cols["k"], v_col0=cols["v"])
    yp = _out_prompt(hp, att_p, xp2, wpool, pool_scale, wup, wua, wout, seq=seq, tm=256, cols=cols)

    ms = dec_b * n_new
    xs2 = x_sample.reshape(ms, dm)
    hs, lfs = _inproj(xs2, norm_gain, wc, wfl, bfp, gains, bd, tm=ms, tn=tn)
    q_s = hs[:, cols["q"]:cols["q"] + aw].reshape(dec_b, n_new, n_heads, HEAD_DIM)
    k_s = hs[:, cols["k"]:cols["k"] + aw].reshape(dec_b, n_new, n_heads, HEAD_DIM)
    v_s = hs[:, cols["v"]:cols["v"] + aw].reshape(dec_b, n_new, n_heads, HEAD_DIM)
    u_s = hs[:, :pw].reshape(dec_b, n_new, pw)
    logf_s = lfs[:, :n_heads].reshape(dec_b, n_new, n_heads)
    q_rows = q_s.transpose(0, 2, 1, 3).reshape(dec_b, n_heads * n_new, HEAD_DIM)
    lf_flat = cache_logf[0].reshape(cache_logf.shape[1], 1, PAGE_SIZE * n_heads)
    mask, mask_new, diag = _decode_masks(n_heads, n_new)
    att_rows = _decode_attention(page_table, q_rows, cache_k, cache_v, lf_flat, k_s, v_s,
                                 logf_s.reshape(dec_b, 1, n_new * n_heads), mask, mask_new, diag,
                                 n_heads=n_heads, n_new=n_new)
    att_s = att_rows.reshape(dec_b, n_heads, n_new, HEAD_DIM).transpose(0, 2, 1, 3).reshape(ms, aw)
    z_s = jnp.concatenate([jnp.zeros((dec_b, HALO - POOL_BUF, pw), F32), state_pool[0], u_s], axis=1)
    ys = _out_sample(z_s, hs, att_s, xs2, wpool, pool_scale, wup, wua, wout,
                     dec_b=dec_b, n_new=n_new, pos0=past_len, cols=cols)

    k_p = hp[:, cols["k"]:cols["k"] + aw].reshape(1, batch, seq, n_heads, HEAD_DIM)
    v_p = hp[:, cols["v"]:cols["v"] + aw].reshape(1, batch, seq, n_heads, HEAD_DIM)
    logf_p = lfp[:, :n_heads].reshape(1, batch, seq, n_heads)
    pool_p = hp[:, :pw].reshape(batch, seq, pw)[:, seq - POOL_BUF:, :][None]
    pool_s = z_s[:, HALO + n_new - POOL_BUF:, :][None]
    return (yp.reshape(batch, seq, dm), ys.reshape(dec_b, n_new, dm), k_p, v_p, logf_p, pool_p,
            k_s[None], v_s[None], logf_s[None], pool_s)
```

```python
import functools

import jax
import jax.numpy as jnp
import numpy as np
from jax import lax
from jax.experimental import pallas as pl
from jax.experimental.pallas import tpu as pltpu

F32 = jnp.float32
BF16 = jnp.bfloat16

HEAD_DIM = 64
POOL_WINDOWS = (2, 4, 8, 16)
POOL_BUF = 15
PAGE_SIZE = 128
EPS = 1e-6
NEG_INF = -1e30
ATTN_SCALE = HEAD_DIM ** -0.5

V7X_LANES = 128
V7X_SUBLANES = 8
V7X_VMEM_LIMIT_BYTES = 56 * 1024 * 1024

HALO = 16
PAGES_PER_STEP = 4


def _cparams(n_grid_axes):
    return pltpu.CompilerParams(
        dimension_semantics=("arbitrary",) * n_grid_axes,
        vmem_limit_bytes=V7X_VMEM_LIMIT_BYTES,
    )


def _split3(x):
    hi = x.astype(BF16)
    r1 = x - hi.astype(F32)
    mid = r1.astype(BF16)
    lo = (r1 - mid.astype(F32)).astype(BF16)
    return hi, mid, lo


def _dot(a, b):
    return jnp.dot(a, b, preferred_element_type=F32)


def _dot_nt(a, b):
    return lax.dot_general(a, b, (((1,), (1,)), ((), ())), preferred_element_type=F32)


def _sigmoid(x):
    return 1.0 / (1.0 + jnp.exp(-x))


def _silu(x):
    return x * _sigmoid(x)


def _scan_lanes(x, *, reverse):
    n = x.shape[-1]
    ax = x.ndim - 1
    lane = lax.broadcasted_iota(jnp.int32, x.shape, ax)
    k = 1
    while k < n:
        if reverse:
            shifted = pltpu.roll(x, n - k, ax)
            x = x + jnp.where(lane < n - k, shifted, 0.0)
        else:
            shifted = pltpu.roll(x, k, ax)
            x = x + jnp.where(lane >= k, shifted, 0.0)
        k *= 2
    return x


def _scan_rows(x):
    n = x.shape[0]
    row = lax.broadcasted_iota(jnp.int32, x.shape, 0)
    k = 1
    while k < n:
        x = x + jnp.where(row >= k, pltpu.roll(x, k, 0), 0.0)
        k *= 2
    return x


def _inproj_kernel(x_ref, ng_ref, w_ref, wfl_ref, bf_ref, qg_ref, kg_ref, kgc_ref, bd_ref, *rest,
                   q_j, k_j, v_j, kv_feature_major):
    if kv_feature_major:
        h_ref, lf_ref, kt_ref, vt_ref, xn_sc = rest
    else:
        h_ref, lf_ref, xn_sc = rest
    j = pl.program_id(1)

    @pl.when(j == 0)
    def _():
        x = x_ref[...]
        ms = jnp.mean(x * x, axis=-1, keepdims=True)
        xn = x * lax.rsqrt(ms + EPS) * ng_ref[...]
        xn_sc[...] = xn.astype(BF16)
        z = _dot_nt(xn_sc[...], wfl_ref[...]) + bf_ref[...]
        lf_ref[...] = jnp.minimum(z, 0.0) - jnp.log1p(jnp.exp(-jnp.abs(z)))

    def in_range(r):
        return jnp.logical_and(j >= r[0], j < r[1])

    def head_norm_rows(acc, gain):
        sq = acc * acc
        hi = sq.astype(BF16)
        lo = (sq - hi.astype(F32)).astype(BF16)
        ss = _dot(hi, bd_ref[...]) + _dot(lo, bd_ref[...])
        return acc * lax.rsqrt(ss * (1.0 / HEAD_DIM) + EPS) * gain

    is_q = in_range(q_j)
    is_k = in_range(k_j)
    is_v = in_range(v_j)

    if kv_feature_major:
        is_plain = jnp.logical_not(is_q | is_k | is_v)
    else:
        is_plain = jnp.logical_not(is_q | is_k)

    @pl.when(is_plain)
    def _():
        h_ref[...] = _dot_nt(xn_sc[...], w_ref[...])

    @pl.when(is_q)
    def _():
        h_ref[...] = head_norm_rows(_dot_nt(xn_sc[...], w_ref[...]), qg_ref[...])

    if kv_feature_major:
        @pl.when(is_k)
        def _():
            acc = _dot_nt(w_ref[...], xn_sc[...])
            tn, tm = acc.shape
            a3 = acc.reshape(tn // HEAD_DIM, HEAD_DIM, tm)
            ms = jnp.mean(a3 * a3, axis=1, keepdims=True)
            g3 = kgc_ref[...].reshape(tn // HEAD_DIM, HEAD_DIM, V7X_LANES)[:, :, 0:1]
            kt_ref[...] = (a3 * lax.rsqrt(ms + EPS) * g3).reshape(tn, tm)

        @pl.when(is_v)
        def _():
            vt_ref[...] = _dot_nt(w_ref[...], xn_sc[...])
    else:
        @pl.when(is_k)
        def _():
            h_ref[...] = head_norm_rows(_dot_nt(xn_sc[...], w_ref[...]), kg_ref[...])


def _inproj(x2d, ng, wt, wfl, bfp, qg, kg, kgc, bd, *, tm, tn, rows, kv_feature_major, seq=None):
    m, d = x2d.shape
    n = wt.shape[0]
    nj = n // tn
    blk = lambda name: (rows[name][0] // tn, rows[name][1] // tn)
    q_j, k_j, v_j = blk("q"), blk("k"), blk("v")
    kern = functools.partial(_inproj_kernel, q_j=q_j, k_j=k_j, v_j=v_j, kv_feature_major=kv_feature_major)
    in_specs = [
        pl.BlockSpec((tm, d), lambda i, j: (i, 0)),
        pl.BlockSpec((1, d), lambda i, j: (0, 0)),
        pl.BlockSpec((tn, d), lambda i, j: (j, 0)),
        pl.BlockSpec((V7X_LANES, d), lambda i, j: (0, 0)),
        pl.BlockSpec((1, V7X_LANES), lambda i, j: (0, 0)),
        pl.BlockSpec((1, tn), lambda i, j: (0, 0)),
        pl.BlockSpec((1, tn), lambda i, j: (0, 0)),
        pl.BlockSpec((tn, V7X_LANES), lambda i, j: (0, 0)),
        pl.BlockSpec((tn, tn), lambda i, j: (0, 0)),
    ]
    if kv_feature_major:
        assert k_j[1] == v_j[0] and v_j[1] == nj
        n_h = k_j[0]
        nb = seq // tm
        width = rows["k"][1] - rows["k"][0]
        out_specs = [
            pl.BlockSpec((tm, tn), lambda i, j: (i, jnp.minimum(j, n_h - 1))),
            pl.BlockSpec((tm, V7X_LANES), lambda i, j: (i, 0)),
            pl.BlockSpec((None, tn, tm), lambda i, j: (i // nb, jnp.clip(j - k_j[0], 0, k_j[1] - k_j[0] - 1), i % nb)),
            pl.BlockSpec((None, tn, tm), lambda i, j: (i // nb, jnp.clip(j - v_j[0], 0, v_j[1] - v_j[0] - 1), i % nb)),
        ]
        out_shape = [
            jax.ShapeDtypeStruct((m, n_h * tn), F32),
            jax.ShapeDtypeStruct((m, V7X_LANES), F32),
            jax.ShapeDtypeStruct((m // seq, width, seq), F32),
            jax.ShapeDtypeStruct((m // seq, width, seq), F32),
        ]
    else:
        out_specs = [
            pl.BlockSpec((tm, tn), lambda i, j: (i, j)),
            pl.BlockSpec((tm, V7X_LANES), lambda i, j: (i, 0)),
        ]
        out_shape = [
            jax.ShapeDtypeStruct((m, n), F32),
            jax.ShapeDtypeStruct((m, V7X_LANES), F32),
        ]
    return pl.pallas_call(
        kern,
        grid=(m // tm, nj),
        in_specs=in_specs,
        out_specs=out_specs,
        out_shape=out_shape,
        scratch_shapes=[pltpu.VMEM((tm, d), BF16)],
        compiler_params=_cparams(2),
        name="inproj",
    )(x2d, ng, wt, wfl, bfp, qg, kg, kgc, bd)


def _fcum_kernel(lf_ref, fcol_ref, ft_ref, lft_ref, *, seq, n_heads):
    c = V7X_LANES
    row = lax.broadcasted_iota(jnp.int32, (c, c), 0)
    col = lax.broadcasted_iota(jnp.int32, (c, c), 1)
    tri = jnp.where(col <= row, 1.0, 0.0).astype(BF16)
    carry = jnp.zeros((1, c), F32)
    for ci in range(seq // c):
        x = lf_ref[ci * c:(ci + 1) * c, :]
        hi, mid, lo = _split3(x)
        fc = (_dot(tri, hi) + _dot(tri, mid)) + _dot(tri, lo) + carry
        fcol_ref[ci * c:(ci + 1) * c, :] = fc
        ft_ref[:, ci * c:(ci + 1) * c] = fc.T[:n_heads, :]
        lft_ref[:, ci * c:(ci + 1) * c] = x.T[:n_heads, :]
        carry = fc[c - 1:c, :]


def _fcum(lf2d, *, batch, seq, n_heads):
    kern = functools.partial(_fcum_kernel, seq=seq, n_heads=n_heads)
    return pl.pallas_call(
        kern,
        grid=(batch,),
        in_specs=[pl.BlockSpec((seq, V7X_LANES), lambda b: (b, 0))],
        out_specs=[
            pl.BlockSpec((seq, V7X_LANES), lambda b: (b, 0)),
            pl.BlockSpec((None, n_heads, seq), lambda b: (b, 0, 0)),
            pl.BlockSpec((None, n_heads, seq), lambda b: (b, 0, 0)),
        ],
        out_shape=[
            jax.ShapeDtypeStruct((batch * seq, V7X_LANES), F32),
            jax.ShapeDtypeStruct((batch, n_heads, seq), F32),
            jax.ShapeDtypeStruct((batch, n_heads, seq), F32),
        ],
        compiler_params=_cparams(1),
        name="forget_cumsum",
    )(lf2d)


def _attn_kernel(q_ref, kt_ref, vt_ref, fc_ref, ft_ref, o_ref, *, tq):
    p = pl.program_id(1)
    i = pl.program_id(2)
    lane = lax.broadcasted_iota(jnp.int32, (1, V7X_LANES), 1)
    is_lo = lane < HEAD_DIM
    q = q_ref[...] * ATTN_SCALE
    q0 = jnp.where(is_lo, q, 0.0).astype(BF16)
    q1 = jnp.where(is_lo, 0.0, q).astype(BF16)
    fc = fc_ref[...]
    fq0 = jnp.sum(jnp.where(lane == 2 * p, fc, 0.0), axis=1, keepdims=True)
    fq1 = jnp.sum(jnp.where(lane == 2 * p + 1, fc, 0.0), axis=1, keepdims=True)
    row = lax.broadcasted_iota(jnp.int32, (tq, tq), 0)
    col = lax.broadcasted_iota(jnp.int32, (tq, tq), 1)
    causal = col <= row

    def step(j, carry, masked):
        m0, l0, m1, l1, acc = carry
        ks = pl.multiple_of(j * tq, tq)
        kt = kt_ref[:, pl.ds(ks, tq)].astype(BF16)
        vt = vt_ref[:, pl.ds(ks, tq)].astype(BF16)
        fr = ft_ref[:, pl.ds(ks, tq)]
        s0 = _dot(q0, kt) + (fq0 - fr[0:1, :])
        s1 = _dot(q1, kt) + (fq1 - fr[1:2, :])
        if masked:
            s0 = jnp.where(causal, s0, NEG_INF)
            s1 = jnp.where(causal, s1, NEG_INF)
        n0 = jnp.maximum(m0, jnp.max(s0, axis=1, keepdims=True))
        n1 = jnp.maximum(m1, jnp.max(s1, axis=1, keepdims=True))
        a0 = jnp.exp(m0 - n0)
        a1 = jnp.exp(m1 - n1)
        p0 = jnp.exp(s0 - n0)
        p1 = jnp.exp(s1 - n1)
        l0 = a0 * l0 + jnp.sum(p0, axis=1, keepdims=True)
        l1 = a1 * l1 + jnp.sum(p1, axis=1, keepdims=True)
        pv0 = _dot_nt(p0.astype(BF16), vt)
        pv1 = _dot_nt(p1.astype(BF16), vt)
        acc = acc * jnp.where(is_lo, a0, a1) + jnp.where(is_lo, pv0, pv1)
        return n0, l0, n1, l1, acc

    init = (jnp.full((tq, 1), NEG_INF, F32), jnp.zeros((tq, 1), F32),
            jnp.full((tq, 1), NEG_INF, F32), jnp.zeros((tq, 1), F32),
            jnp.zeros((tq, V7X_LANES), F32))
    carry = lax.fori_loop(0, i, lambda j, c: step(j, c, False), init)
    _, l0, _, l1, acc = step(i, carry, True)
    o_ref[...] = acc / jnp.where(is_lo, l0, l1)


def _prompt_attention(h, kt, vt, fcol, ft_pairs, *, batch, seq, n_heads, tq, q_col0):
    n_pairs = n_heads // 2
    qt = seq // tq
    lanes = V7X_LANES
    kern = functools.partial(_attn_kernel, tq=tq)
    return pl.pallas_call(
        kern,
        grid=(batch, n_pairs, qt),
        in_specs=[
            pl.BlockSpec((tq, lanes), lambda b, p, i: (b * qt + i, q_col0 // lanes + p)),
            pl.BlockSpec((None, lanes, seq), lambda b, p, i: (b, p, 0)),
            pl.BlockSpec((None, lanes, seq), lambda b, p, i: (b, p, 0)),
            pl.BlockSpec((tq, lanes), lambda b, p, i: (b * qt + i, 0)),
            pl.BlockSpec((None, None, 2, seq), lambda b, p, i: (b, p, 0, 0)),
        ],
        out_specs=pl.BlockSpec((tq, lanes), lambda b, p, i: (b * qt + i, p)),
        out_shape=jax.ShapeDtypeStruct((batch * seq, n_heads * HEAD_DIM), F32),
        compiler_params=_cparams(3),
        name="prompt_attention",
    )(h, kt, vt, fcol, ft_pairs)


def _decode_kernel(pt_ref, q_ref, *rest, n_heads, n_new, n_pg):
    del pt_ref
    kc_refs = rest[0:n_pg]
    vc_refs = rest[n_pg:2 * n_pg]
    lf_refs = rest[2 * n_pg:3 * n_pg]
    kn_ref, vn_ref, lfn_ref, lfnt_ref, o_ref, m_sc, l_sc, acc_sc, carry_sc = rest[3 * n_pg:]
    j = pl.program_id(1)
    q = (q_ref[...] * ATTN_SCALE).astype(BF16)
    f_new = _scan_rows(lfn_ref[...])

    def attend(pages, mask):
        cols = []
        for kt_ref, _, kb in pages:
            rows = []
            for h in range(n_heads):
                s = _dot(q[h], kt_ref[h].astype(BF16))
                s = (s + f_new[:, h:h + 1]) + kb[h:h + 1, :]
                rows.append(s if mask is None else s + mask)
            cols.append(jnp.concatenate(rows, axis=0))
        s_all = jnp.concatenate(cols, axis=1)
        m_prev = m_sc[...]
        m_new = jnp.maximum(m_prev, jnp.max(s_all, axis=1, keepdims=True))
        alpha = jnp.exp(m_prev - m_new)
        p_all = jnp.exp(s_all - m_new)
        l_sc[...] = alpha * l_sc[...] + jnp.sum(p_all, axis=1, keepdims=True)
        m_sc[...] = m_new
        pv_rows = []
        for h in range(n_heads):
            pv = None
            for g, (_, vt_ref, _) in enumerate(pages):
                p_hg = p_all[h * n_new:(h + 1) * n_new, g * PAGE_SIZE:(g + 1) * PAGE_SIZE].astype(BF16)
                t = _dot_nt(p_hg, vt_ref[h].astype(BF16))
                pv = t if pv is None else pv + t
            pv_rows.append(pv)
        acc_sc[...] = alpha * acc_sc[...] + jnp.concatenate(pv_rows, axis=0)

    @pl.when(j == 0)
    def _():
        m_sc[...] = jnp.full(m_sc.shape, NEG_INF, F32)
        l_sc[...] = jnp.zeros(l_sc.shape, F32)
        acc_sc[...] = jnp.zeros(acc_sc.shape, F32)
        carry_sc[...] = jnp.zeros(carry_sc.shape, F32)
        f_new_t = _scan_lanes(lfnt_ref[...], reverse=False)
        tok = lax.broadcasted_iota(jnp.int32, (n_new, PAGE_SIZE), 0)
        pos = lax.broadcasted_iota(jnp.int32, (n_new, PAGE_SIZE), 1)
        attend([(kn_ref, vn_ref, -f_new_t)], jnp.where(pos <= tok, 0.0, NEG_INF))

    carry = carry_sc[...]
    pages = []
    for g in range(n_pg):
        lf = lf_refs[g][...]
        incl = _scan_lanes(lf, reverse=True)
        pages.append((kc_refs[g], vc_refs[g], (incl - lf) + carry))
        carry = carry + incl[:, 0:1]
    carry_sc[...] = carry
    attend(pages, None)

    @pl.when(j == pl.num_programs(1) - 1)
    def _():
        o_ref[...] = acc_sc[...] / l_sc[...]


def _decode_attention(page_table, q_hm, cache_kt, cache_vt, cache_lft, kn_t, vn_t, lf_new, lf_new_t,
                      *, n_heads, n_new, n_pg):
    dec_b, n_pages = page_table.shape
    rows = n_heads * n_new
    last = n_pages - 1
    kern = functools.partial(_decode_kernel, n_heads=n_heads, n_new=n_new, n_pg=n_pg)

    def page_spec(g, tail):
        return pl.BlockSpec((None, None, n_heads) + tail,
                            lambda b, j, pt: (0, pt[b, last - (j * n_pg + g)]) + (0,) * (1 + len(tail)))

    kv_tail = (HEAD_DIM, PAGE_SIZE)
    in_specs = [pl.BlockSpec((None, n_heads, n_new, HEAD_DIM), lambda b, j, pt: (b, 0, 0, 0))]
    in_specs += [page_spec(g, kv_tail) for g in range(n_pg)]
    in_specs += [page_spec(g, kv_tail) for g in range(n_pg)]
    in_specs += [page_spec(g, (PAGE_SIZE,)) for g in range(n_pg)]
    in_specs += [
        pl.BlockSpec((None, n_heads, HEAD_DIM, PAGE_SIZE), lambda b, j, pt: (b, 0, 0, 0)),
        pl.BlockSpec((None, n_heads, HEAD_DIM, PAGE_SIZE), lambda b, j, pt: (b, 0, 0, 0)),
        pl.BlockSpec((None, n_new, V7X_LANES), lambda b, j, pt: (b, 0, 0)),
        pl.BlockSpec((None, n_heads, PAGE_SIZE), lambda b, j, pt: (b, 0, 0)),
    ]
    grid_spec = pltpu.PrefetchScalarGridSpec(
        num_scalar_prefetch=1,
        grid=(dec_b, n_pages // n_pg),
        in_specs=in_specs,
        out_specs=pl.BlockSpec((None, rows, HEAD_DIM), lambda b, j, pt: (b, 0, 0)),
        scratch_shapes=[
            pltpu.VMEM((rows, 1), F32),
            pltpu.VMEM((rows, 1), F32),
            pltpu.VMEM((rows, HEAD_DIM), F32),
            pltpu.VMEM((n_heads, PAGE_SIZE), F32),
        ],
    )
    args = [page_table, q_hm] + [cache_kt] * n_pg + [cache_vt] * n_pg + [cache_lft] * n_pg
    args += [kn_t, vn_t, lf_new, lf_new_t]
    return pl.pallas_call(
        kern,
        grid_spec=grid_spec,
        out_shape=jax.ShapeDtypeStruct((dec_b, rows, HEAD_DIM), F32),
        compiler_params=_cparams(2),
        name="decode_attention",
    )(*args)


def _pool_diff(z, u, pos, group_w):
    s = z
    k = 1
    while k < group_w:
        s = s + pltpu.roll(s, k, 0)
        k *= 2
    cnt = jnp.minimum(pos + 1, group_w).astype(F32)
    return s[HALO:, :] / cnt - u


def _mix_tail(d_groups, gp, ga, att, ma, mb, x, wpool_ref, ps_ref, wup_ref, wua_ref, wout_ref):
    mixed = [_dot(d.astype(BF16), wpool_ref[gi]) for gi, d in enumerate(d_groups)]
    mixed = jnp.concatenate(mixed, axis=1)
    branch_a = (mixed * ps_ref[...]) * _silu(gp)
    branch_b = att * _silu(ga)
    up_a = _dot(branch_a.astype(BF16), wup_ref[...])
    up_b = _dot(branch_b.astype(BF16), wua_ref[...])
    merged = _sigmoid(ma) * up_a + _sigmoid(mb) * up_b
    return x + _dot(merged.astype(BF16), wout_ref[...])


def _out_prompt_kernel(u_ref, halo_ref, gp_ref, ga_ref, ma_ref, mb_ref, att_ref, x_ref,
                       wpool_ref, ps_ref, wup_ref, wua_ref, wout_ref, y_ref, *, tm, seq):
    i = pl.program_id(0)
    pos0 = (i * tm) % seq
    u = u_ref[...]
    halo = jnp.where(pos0 == 0, 0.0, halo_ref[...])
    z = jnp.concatenate([halo, u], axis=0)
    pos = pos0 + lax.broadcasted_iota(jnp.int32, (tm, 1), 0)
    gw = u.shape[1] // len(POOL_WINDOWS)
    d_groups = [_pool_diff(z[:, gi * gw:(gi + 1) * gw], u[:, gi * gw:(gi + 1) * gw], pos, w)
                for gi, w in enumerate(POOL_WINDOWS)]
    y_ref[...] = _mix_tail(d_groups, gp_ref[...], ga_ref[...], att_ref[...], ma_ref[...], mb_ref[...], x_ref[...],
                           wpool_ref, ps_ref, wup_ref, wua_ref, wout_ref)


def _out_sample_kernel(z_ref, gp_ref, ga_ref, ma_ref, mb_ref, att_ref, x_ref,
                       wpool_ref, ps_ref, wup_ref, wua_ref, wout_ref, y_ref, *, dec_b, n_new, pos0):
    gw = z_ref.shape[2] // len(POOL_WINDOWS)
    pos = pos0 + lax.broadcasted_iota(jnp.int32, (n_new, 1), 0)
    per_seq = []
    for b in range(dec_b):
        z = z_ref[b]
        u = z[HALO:, :]
        per_seq.append([_pool_diff(z[:, gi * gw:(gi + 1) * gw], u[:, gi * gw:(gi + 1) * gw], pos, w)
                        for gi, w in enumerate(POOL_WINDOWS)])
    d_groups = [jnp.concatenate([per_seq[b][gi] for b in range(dec_b)], axis=0) for gi in range(len(POOL_WINDOWS))]
    y_ref[...] = _mix_tail(d_groups, gp_ref[...], ga_ref[...], att_ref[...], ma_ref[...], mb_ref[...], x_ref[...],
                           wpool_ref, ps_ref, wup_ref, wua_ref, wout_ref)


def _weight_specs(pw, gw, dm):
    one = pl.Buffered(1)
    return [
        pl.BlockSpec((len(POOL_WINDOWS), gw, gw), lambda i: (0, 0, 0), pipeline_mode=one),
        pl.BlockSpec((1, pw), lambda i: (0, 0), pipeline_mode=one),
        pl.BlockSpec((pw, dm), lambda i: (0, 0), pipeline_mode=one),
        pl.BlockSpec((pw, dm), lambda i: (0, 0), pipeline_mode=one),
        pl.BlockSpec((dm, dm), lambda i: (0, 0), pipeline_mode=one),
    ]


def _out_prompt(h, att, x2d, wpool, ps, wup, wua, wout, *, seq, tm, cols):
    m, dm = x2d.shape
    pw = att.shape[1]
    gw = pw // len(POOL_WINDOWS)
    hb = tm // HALO
    kern = functools.partial(_out_prompt_kernel, tm=tm, seq=seq)
    return pl.pallas_call(
        kern,
        grid=(m // tm,),
        in_specs=[
            pl.BlockSpec((tm, pw), lambda i: (i, cols["u"] // pw)),
            pl.BlockSpec((HALO, pw), lambda i: (jnp.maximum(i * hb - 1, 0), cols["u"] // pw)),
            pl.BlockSpec((tm, pw), lambda i: (i, cols["gp"] // pw)),
            pl.BlockSpec((tm, pw), lambda i: (i, cols["ga"] // pw)),
            pl.BlockSpec((tm, dm), lambda i: (i, cols["ma"] // dm)),
            pl.BlockSpec((tm, dm), lambda i: (i, cols["mb"] // dm)),
            pl.BlockSpec((tm, pw), lambda i: (i, 0)),
            pl.BlockSpec((tm, dm), lambda i: (i, 0)),
        ] + _weight_specs(pw, gw, dm),
        out_specs=pl.BlockSpec((tm, dm), lambda i: (i, 0)),
        out_shape=jax.ShapeDtypeStruct((m, dm), F32),
        compiler_params=_cparams(1),
        name="out_prompt",
    )(h, h, h, h, h, h, att, x2d, wpool, ps, wup, wua, wout)


def _out_sample(z, h, att, x2d, wpool, ps, wup, wua, wout, *, dec_b, n_new, pos0, cols):
    m, dm = x2d.shape
    pw = att.shape[1]
    gw = pw // len(POOL_WINDOWS)
    kern = functools.partial(_out_sample_kernel, dec_b=dec_b, n_new=n_new, pos0=pos0)
    return pl.pallas_call(
        kern,
        grid=(1,),
        in_specs=[
            pl.BlockSpec((dec_b, HALO + n_new, pw), lambda i: (0, 0, 0)),
            pl.BlockSpec((m, pw), lambda i: (0, cols["gp"] // pw)),
            pl.BlockSpec((m, pw), lambda i: (0, cols["ga"] // pw)),
            pl.BlockSpec((m, dm), lambda i: (0, cols["ma"] // dm)),
            pl.BlockSpec((m, dm), lambda i: (0, cols["mb"] // dm)),
            pl.BlockSpec((m, pw), lambda i: (0, 0)),
            pl.BlockSpec((m, dm), lambda i: (0, 0)),
        ] + _weight_specs(pw, gw, dm),
        out_specs=pl.BlockSpec((m, dm), lambda i: (0, 0)),
        out_shape=jax.ShapeDtypeStruct((m, dm), F32),
        compiler_params=_cparams(1),
        name="out_sample",
    )(z, h, h, h, h, att, x2d, wpool, ps, wup, wua, wout)


def kernel(x_prompt, x_sample, cache_k, cache_v, cache_logf, state_pool, page_table, norm_gain, w_in, b_f,
           q_norm_gain, k_norm_gain, w_pool_map, pool_scale, w_up_pool, w_up_attn, w_out):
    batch, seq, dm = x_prompt.shape
    dec_b, n_new, _ = x_sample.shape
    assert w_in.shape[0] == 1
    n_heads = b_f.shape[1]
    aw = n_heads * HEAD_DIM
    pw = w_up_pool.shape[1]
    n_pages = page_table.shape[1]
    past_len = n_pages * PAGE_SIZE
    assert pw == aw and dm == 2 * pw and n_pages % PAGES_PER_STEP == 0

    src = {}
    off = 0
    for name, size in (("u", pw), ("gp", pw), ("q", aw), ("k", aw), ("v", aw), ("ga", aw), ("fl", n_heads),
                       ("ma", dm), ("mb", dm)):
        src[name] = (off, off + size)
        off += size
    order = ("u", "gp", "q", "ga", "ma", "mb", "k", "v")
    rows = {}
    off = 0
    for name in order:
        size = src[name][1] - src[name][0]
        rows[name] = (off, off + size)
        off += size
    cols = {name: rows[name][0] for name in order}
    w_t = w_in[0].T
    wt = jnp.concatenate([w_t[src[n][0]:src[n][1]] for n in order], axis=0).astype(BF16)
    wfl = jnp.pad(w_t[src["fl"][0]:src["fl"][1]], ((0, V7X_LANES - n_heads), (0, 0))).astype(BF16)
    bfp = jnp.pad(b_f, ((0, 0), (0, V7X_LANES - n_heads)))
    tn = 512
    reps = tn // HEAD_DIM
    qg = jnp.tile(q_norm_gain[0], reps)[None, :]
    kg = jnp.tile(k_norm_gain[0], reps)[None, :]
    kgc = jnp.broadcast_to(jnp.tile(k_norm_gain[0], reps)[:, None], (tn, V7X_LANES))
    seg = np.arange(tn) // HEAD_DIM
    bd = jnp.asarray((seg[:, None] == seg[None, :]).astype(np.float32)).astype(BF16)
    wpool = w_pool_map[0].astype(BF16)
    wup = w_up_pool[0].astype(BF16)
    wua = w_up_attn[0].astype(BF16)
    wout = w_out[0].astype(BF16)

    xp2 = x_prompt.reshape(batch * seq, dm)
    hp, lfp, kt_p, vt_p = _inproj(xp2, norm_gain, wt, wfl, bfp, qg, kg, kgc, bd, tm=1024, tn=tn, rows=rows,
                                  kv_feature_major=True, seq=seq)
    fcol, ft, lft = _fcum(lfp, batch=batch, seq=seq, n_heads=n_heads)
    ft_pairs = ft.reshape(batch, n_heads // 2, 2, seq)
    att_p = _prompt_attention(hp, kt_p, vt_p, fcol, ft_pairs, batch=batch, seq=seq, n_heads=n_heads, tq=256,
                              q_col0=cols["q"])
    yp = _out_prompt(hp, att_p, xp2, wpool, pool_scale, wup, wua, wout, seq=seq, tm=256, cols=cols)

    ms = dec_b * n_new
    xs2 = x_sample.reshape(ms, dm)
    hs, lfs = _inproj(xs2, norm_gain, wt, wfl, bfp, qg, kg, kgc, bd, tm=ms, tn=tn, rows=rows,
                      kv_feature_major=False)
    q_s = hs[:, cols["q"]:cols["q"] + aw].reshape(dec_b, n_new, n_heads, HEAD_DIM)
    k_s = hs[:, cols["k"]:cols["k"] + aw].reshape(dec_b, n_new, n_heads, HEAD_DIM)
    v_s = hs[:, cols["v"]:cols["v"] + aw].reshape(dec_b, n_new, n_heads, HEAD_DIM)
    u_s = hs[:, :pw].reshape(dec_b, n_new, pw)
    lf_new = lfs.reshape(dec_b, n_new, V7X_LANES)
    logf_s = lf_new[:, :, :n_heads]
    pad_pos = ((0, 0), (0, 0), (0, 0), (0, PAGE_SIZE - n_new))
    kn_t = jnp.pad(k_s.transpose(0, 2, 3, 1), pad_pos)
    vn_t = jnp.pad(v_s.transpose(0, 2, 3, 1), pad_pos)
    lf_new_t = jnp.pad(logf_s.transpose(0, 2, 1), ((0, 0), (0, 0), (0, PAGE_SIZE - n_new)))
    q_hm = q_s.transpose(0, 2, 1, 3)
    cache_kt = cache_k.transpose(0, 1, 3, 4, 2)
    cache_vt = cache_v.transpose(0, 1, 3, 4, 2)
    cache_lft = cache_logf.transpose(0, 1, 3, 2)
    att_rows = _decode_attention(page_table, q_hm, cache_kt, cache_vt, cache_lft, kn_t, vn_t, lf_new, lf_new_t,
                                 n_heads=n_heads, n_new=n_new, n_pg=PAGES_PER_STEP)
    att_s = att_rows.reshape(dec_b, n_heads, n_new, HEAD_DIM).transpose(0, 2, 1, 3).reshape(ms, aw)
    z_s = jnp.concatenate([jnp.zeros((dec_b, HALO - POOL_BUF, pw), F32), state_pool[0], u_s], axis=1)
    ys = _out_sample(z_s, hs, att_s, xs2, wpool, pool_scale, wup, wua, wout,
                     dec_b=dec_b, n_new=n_new, pos0=past_len, cols=cols)

    k_p = kt_p.reshape(1, batch, n_heads, HEAD_DIM, seq).transpose(0, 1, 4, 2, 3)
    v_p = vt_p.reshape(1, batch, n_heads, HEAD_DIM, seq).transpose(0, 1, 4, 2, 3)
    logf_p = lft.transpose(0, 2, 1)[None]
    pool_p = hp[:, :pw].reshape(batch, seq, pw)[:, seq - POOL_BUF:, :][None]
    pool_s = z_s[:, HALO + n_new - POOL_BUF:, :][None]
    return (yp.reshape(batch, seq, dm), ys.reshape(dec_b, n_new, dm), k_p, v_p, logf_p, pool_p,
            k_s[None], v_s[None], logf_s[None], pool_s)
```

```python
import functools

import jax
import jax.numpy as jnp
import numpy as np
from jax import lax
from jax.experimental import pallas as pl
from jax.experimental.pallas import tpu as pltpu

F32 = jnp.float32
BF16 = jnp.bfloat16

HEAD_DIM = 64
POOL_WINDOWS = (2, 4, 8, 16)
POOL_BUF = 15
PAGE_SIZE = 128
EPS = 1e-6
NEG_INF = -1e30
ATTN_SCALE = HEAD_DIM ** -0.5
LOG2E = 1.4426950408889634

V7X_LANES = 128
V7X_SUBLANES = 8
V7X_VMEM_LIMIT_BYTES = 56 * 1024 * 1024

HALO = 16
PAGES_PER_STEP = 8
N_AUG = 3
ATTN_PAIRS_PER_STEP = 4


def _cparams(n_grid_axes):
    return pltpu.CompilerParams(
        dimension_semantics=("arbitrary",) * n_grid_axes,
        vmem_limit_bytes=V7X_VMEM_LIMIT_BYTES,
    )


def _split3(x):
    hi = x.astype(BF16)
    r1 = x - hi.astype(F32)
    mid = r1.astype(BF16)
    lo = (r1 - mid.astype(F32)).astype(BF16)
    return hi, mid, lo


def _dot(a, b):
    return jnp.dot(a, b, preferred_element_type=F32)


def _dot_nt(a, b):
    return lax.dot_general(a, b, (((1,), (1,)), ((), ())), preferred_element_type=F32)


def _sigmoid(x):
    return 1.0 / (1.0 + jnp.exp(-x))


def _silu(x):
    return x * _sigmoid(x)


def _scan_lanes(x, *, reverse):
    n = x.shape[-1]
    ax = x.ndim - 1
    lane = lax.broadcasted_iota(jnp.int32, x.shape, ax)
    k = 1
    while k < n:
        if reverse:
            shifted = pltpu.roll(x, n - k, ax)
            x = x + jnp.where(lane < n - k, shifted, 0.0)
        else:
            shifted = pltpu.roll(x, k, ax)
            x = x + jnp.where(lane >= k, shifted, 0.0)
        k *= 2
    return x


def _scan_rows(x):
    n = x.shape[0]
    row = lax.broadcasted_iota(jnp.int32, x.shape, 0)
    k = 1
    while k < n:
        x = x + jnp.where(row >= k, pltpu.roll(x, k, 0), 0.0)
        k *= 2
    return x


def _inproj_kernel(tm_ref, tg_ref, x_ref, ng_ref, wm_ref, wg_ref, wfl_ref, bf_ref, qg_ref, kg_ref, qgc_ref, kgc_ref,
                   bd_ref, *rest, steps, feature_major):
    del tm_ref, tg_ref
    if feature_major:
        h_ref, lf_ref, qt_ref, kt_ref, vt_ref, xn_sc = rest
    else:
        h_ref, lf_ref, xn_sc = rest
    j = pl.program_id(1)

    @pl.when(j == 0)
    def _():
        x = x_ref[...]
        ms = jnp.mean(x * x, axis=-1, keepdims=True)
        xn = x * lax.rsqrt(ms + EPS) * ng_ref[...]
        xn_sc[...] = xn.astype(BF16)
        z = _dot_nt(xn_sc[...], wfl_ref[...]) + bf_ref[...]
        lf_ref[...] = jnp.minimum(z, 0.0) - jnp.log1p(jnp.exp(-jnp.abs(z)))

    def in_range(name):
        lo, hi = steps[name]
        return jnp.logical_and(j >= lo, j < hi)

    def head_norm_rows(acc, gain):
        sq = acc * acc
        hi = sq.astype(BF16)
        lo = (sq - hi.astype(F32)).astype(BF16)
        ss = _dot(hi, bd_ref[...]) + _dot(lo, bd_ref[...])
        return acc * lax.rsqrt(ss * (1.0 / HEAD_DIM) + EPS) * gain

    def head_norm_cols(acc, gain_col_ref):
        tn, tm = acc.shape
        a3 = acc.reshape(tn // HEAD_DIM, HEAD_DIM, tm)
        ms = jnp.mean(a3 * a3, axis=1, keepdims=True)
        g3 = gain_col_ref[...].reshape(tn // HEAD_DIM, HEAD_DIM, V7X_LANES)[:, :, 0:1]
        return (a3 * lax.rsqrt(ms + EPS) * g3).reshape(tn, tm)

    @pl.when(in_range("gate"))
    def _():
        h_ref[...] = _dot_nt(xn_sc[...], wg_ref[...])

    @pl.when(in_range("plain"))
    def _():
        h_ref[...] = _dot_nt(xn_sc[...], wm_ref[...])

    if feature_major:
        @pl.when(in_range("q"))
        def _():
            qt_ref[...] = head_norm_cols(_dot_nt(wm_ref[...], xn_sc[...]), qgc_ref)

        @pl.when(in_range("k"))
        def _():
            kt_ref[...] = head_norm_cols(_dot_nt(wm_ref[...], xn_sc[...]), kgc_ref)

        @pl.when(in_range("v"))
        def _():
            vt_ref[...] = _dot_nt(wm_ref[...], xn_sc[...])
    else:
        @pl.when(in_range("q"))
        def _():
            h_ref[...] = head_norm_rows(_dot_nt(xn_sc[...], wm_ref[...]), qg_ref[...])

        @pl.when(in_range("k"))
        def _():
            h_ref[...] = head_norm_rows(_dot_nt(xn_sc[...], wm_ref[...]), kg_ref[...])

        @pl.when(in_range("v"))
        def _():
            h_ref[...] = _dot_nt(xn_sc[...], wm_ref[...])


def _inproj(plan, x2d, ng, wm, wg, wfl, bfp, qg, kg, qgc, kgc, bd, *, tm, feature_major, seq=None):
    m, d = x2d.shape
    tn = plan["tn"]
    steps = plan["steps"]
    nj = plan["n_steps"]
    kern = functools.partial(_inproj_kernel, steps=steps, feature_major=feature_major)
    in_specs = [
        pl.BlockSpec((tm, d), lambda i, j, t0, t1: (i, 0)),
        pl.BlockSpec((1, d), lambda i, j, t0, t1: (0, 0)),
        pl.BlockSpec((tn, d), lambda i, j, t0, t1: (t0[j], 0)),
        pl.BlockSpec((tn, d), lambda i, j, t0, t1: (t1[j], 0)),
        pl.BlockSpec((V7X_LANES, d), lambda i, j, t0, t1: (0, 0)),
        pl.BlockSpec((1, V7X_LANES), lambda i, j, t0, t1: (0, 0)),
        pl.BlockSpec((1, tn), lambda i, j, t0, t1: (0, 0)),
        pl.BlockSpec((1, tn), lambda i, j, t0, t1: (0, 0)),
        pl.BlockSpec((tn, V7X_LANES), lambda i, j, t0, t1: (0, 0)),
        pl.BlockSpec((tn, V7X_LANES), lambda i, j, t0, t1: (0, 0)),
        pl.BlockSpec((tn, tn), lambda i, j, t0, t1: (0, 0)),
    ]
    if feature_major:
        n_h = steps["q"][0]
        nb = seq // tm
        width = (steps["q"][1] - steps["q"][0]) * tn

        def fm_spec(name):
            lo, hi = steps[name]
            return pl.BlockSpec((None, tn, tm),
                                lambda i, j, t0, t1: (i // nb, jnp.clip(j - lo, 0, hi - lo - 1), i % nb))

        out_specs = [
            pl.BlockSpec((tm, tn), lambda i, j, t0, t1: (i, jnp.minimum(j, n_h - 1))),
            pl.BlockSpec((tm, V7X_LANES), lambda i, j, t0, t1: (i, 0)),
            fm_spec("q"), fm_spec("k"), fm_spec("v"),
        ]
        out_shape = [
            jax.ShapeDtypeStruct((m, n_h * tn), F32),
            jax.ShapeDtypeStruct((m, V7X_LANES), F32),
        ] + [jax.ShapeDtypeStruct((m // seq, width, seq), F32)] * 3
    else:
        out_specs = [
            pl.BlockSpec((tm, tn), lambda i, j, t0, t1: (i, j)),
            pl.BlockSpec((tm, V7X_LANES), lambda i, j, t0, t1: (i, 0)),
        ]
        out_shape = [
            jax.ShapeDtypeStruct((m, nj * tn), F32),
            jax.ShapeDtypeStruct((m, V7X_LANES), F32),
        ]
    grid_spec = pltpu.PrefetchScalarGridSpec(
        num_scalar_prefetch=2,
        grid=(m // tm, nj),
        in_specs=in_specs,
        out_specs=out_specs,
        scratch_shapes=[pltpu.VMEM((tm, d), BF16)],
    )
    return pl.pallas_call(
        kern,
        grid_spec=grid_spec,
        out_shape=out_shape,
        compiler_params=_cparams(2),
        name="inproj",
    )(plan["tbl_main"], plan["tbl_gate"], x2d, ng, wm, wg, wfl, bfp, qg, kg, qgc, kgc, bd)


def _inproj_plan(pw, aw, dm, tn):
    main_rows = {"u": 0, "gp": pw, "q": 2 * pw, "k": 2 * pw + aw, "v": 2 * pw + 2 * aw, "ga": 2 * pw + 3 * aw}
    seq_steps = []
    for name, size in (("ma", dm), ("mb", dm)):
        base = {"ma": 0, "mb": dm}[name]
        seq_steps += [("gate", (base + o) // tn) for o in range(0, size, tn)]
    n_gate = len(seq_steps)
    for name, size in (("u", pw), ("gp", pw), ("ga", aw), ("q", aw), ("k", aw), ("v", aw)):
        seq_steps += [("main", (main_rows[name] + o) // tn) for o in range(0, size, tn)]
    tbl_main, tbl_gate = [], []
    first_main = next(b for op, b in seq_steps if op == "main")
    cur_m, cur_g = first_main, 0
    for op, b in seq_steps:
        if op == "main":
            cur_m = b
        else:
            cur_g = b
        tbl_main.append(cur_m)
        tbl_gate.append(cur_g)
    n_plain = (2 * pw + aw) // tn
    q0 = n_gate + n_plain
    per = aw // tn
    steps = {"gate": (0, n_gate), "plain": (n_gate, q0), "q": (q0, q0 + per), "k": (q0 + per, q0 + 2 * per),
             "v": (q0 + 2 * per, q0 + 3 * per)}
    cols = {"ma": 0, "mb": dm, "u": 2 * dm, "gp": 2 * dm + pw, "ga": 2 * dm + 2 * pw,
            "q": q0 * tn, "k": (q0 + per) * tn, "v": (q0 + 2 * per) * tn}
    return {"tn": tn, "steps": steps, "n_steps": len(seq_steps), "cols": cols,
            "tbl_main": jnp.asarray(tbl_main, jnp.int32), "tbl_gate": jnp.asarray(tbl_gate, jnp.int32)}


def _fcum_kernel(lf_ref, fcol_ref, ft_ref, lft_ref, *, seq, n_heads):
    c = V7X_LANES
    row = lax.broadcasted_iota(jnp.int32, (c, c), 0)
    col = lax.broadcasted_iota(jnp.int32, (c, c), 1)
    tri = jnp.where(col <= row, 1.0, 0.0).astype(BF16)
    carry = jnp.zeros((1, c), F32)
    for ci in range(seq // c):
        x = lf_ref[ci * c:(ci + 1) * c, :]
        hi, mid, lo = _split3(x)
        fc = (_dot(tri, hi) + _dot(tri, mid)) + _dot(tri, lo) + carry
        fcol_ref[ci * c:(ci + 1) * c, :] = fc
        ft_ref[:, ci * c:(ci + 1) * c] = fc.T[:n_heads, :]
        lft_ref[:, ci * c:(ci + 1) * c] = x.T[:n_heads, :]
        carry = fc[c - 1:c, :]


def _fcum(lf2d, *, batch, seq, n_heads):
    kern = functools.partial(_fcum_kernel, seq=seq, n_heads=n_heads)
    return pl.pallas_call(
        kern,
        grid=(batch,),
        in_specs=[pl.BlockSpec((seq, V7X_LANES), lambda b: (b, 0))],
        out_specs=[
            pl.BlockSpec((seq, V7X_LANES), lambda b: (b, 0)),
            pl.BlockSpec((None, n_heads, seq), lambda b: (b, 0, 0)),
            pl.BlockSpec((None, n_heads, seq), lambda b: (b, 0, 0)),
        ],
        out_shape=[
            jax.ShapeDtypeStruct((batch * seq, V7X_LANES), F32),
            jax.ShapeDtypeStruct((batch, n_heads, seq), F32),
            jax.ShapeDtypeStruct((batch, n_heads, seq), F32),
        ],
        compiler_params=_cparams(1),
        name="forget_cumsum",
    )(lf2d)


def _attn_kernel(qt_ref, kt_ref, vt_ref, fcol_ref, ft_ref, o_ref, ka_sc, *, tq, seq, n_pair):
    g = pl.program_id(1)
    i = pl.program_id(2)
    d = HEAD_DIM
    lanes = V7X_LANES
    lane1 = lax.broadcasted_iota(jnp.int32, (1, lanes), 1)

    @pl.when(i == 0)
    def _():
        hrow = lax.broadcasted_iota(jnp.int32, (lanes, lanes), 0)
        lane = lax.broadcasted_iota(jnp.int32, (lanes, lanes), 1)

        def sel(head, lane0, piece):
            return jnp.where((hrow == head) & (lane == lane0 + N_AUG + piece), -1.0, 0.0).astype(BF16)

        ones0 = jnp.where((lane1 >= d) & (lane1 < d + N_AUG), 1.0, 0.0)
        ones1 = jnp.where(lane1 < N_AUG, 1.0, 0.0)
        ck = 256
        for c in range(seq // ck):
            pieces = _split3(fcol_ref[c * ck:(c + 1) * ck, :] * LOG2E)
            for pr in range(n_pair):
                head0 = 2 * (g * n_pair + pr)
                k_rows = kt_ref[pr * lanes:(pr + 1) * lanes, c * ck:(c + 1) * ck].T
                aug0 = ones0
                aug1 = ones1
                for pi, piece in enumerate(pieces):
                    aug0 = aug0 + _dot(piece, sel(head0, d, pi))
                    aug1 = aug1 + _dot(piece, sel(head0 + 1, 0, pi))
                ka_sc[2 * pr, c * ck:(c + 1) * ck, :] = jnp.where(lane1 < d, k_rows, aug0).astype(BF16)
                ka_sc[2 * pr + 1, c * ck:(c + 1) * ck, :] = jnp.where(lane1 >= d, k_rows, aug1).astype(BF16)

    qs = pl.multiple_of(i * tq, tq)
    rowi = lax.broadcasted_iota(jnp.int32, (d, tq), 0)

    def aug_rows(f_row):
        hi, mid, lo = (x.astype(F32) for x in _split3(f_row))
        return jnp.where(rowi == 0, hi, jnp.where(rowi == 1, mid, jnp.where(rowi == 2, lo,
                         jnp.where(rowi < 2 * N_AUG, 1.0, 0.0))))

    qa = []
    for pr in range(n_pair):
        qt = qt_ref[pr * lanes:(pr + 1) * lanes, :] * (ATTN_SCALE * LOG2E)
        fq = ft_ref[pr, :, pl.ds(qs, tq)] * LOG2E
        qa.append(jnp.concatenate([qt[:d, :], aug_rows(fq[0:1, :])], axis=0).astype(BF16))
        qa.append(jnp.concatenate([aug_rows(fq[1:2, :]), qt[d:, :]], axis=0).astype(BF16))
    krow = lax.broadcasted_iota(jnp.int32, (tq, tq), 0)
    qcol = lax.broadcasted_iota(jnp.int32, (tq, tq), 1)
    causal = krow <= qcol

    def step(j, carry, masked):
        ks = pl.multiple_of(j * tq, tq)
        heads = range(2 * n_pair)
        scores = [_dot(ka_sc[hh, pl.ds(ks, tq), :], qa[hh]) for hh in heads]
        stats, probs = [], []
        for hh in heads:
            m, l, _ = carry[hh]
            s = jnp.where(causal, scores[hh], NEG_INF) if masked else scores[hh]
            m_new = jnp.maximum(m, jnp.max(s, axis=0, keepdims=True))
            alpha = jnp.exp2(m - m_new)
            pt = jnp.exp2(s - m_new)
            stats.append((m_new, alpha * l + jnp.sum(pt, axis=0, keepdims=True), alpha))
            probs.append(pt.astype(BF16))
        pvs = [_dot(vt_ref[hh * d:(hh + 1) * d, pl.ds(ks, tq)].astype(BF16), probs[hh]) for hh in heads]
        return tuple((stats[hh][0], stats[hh][1], stats[hh][2] * carry[hh][2] + pvs[hh]) for hh in heads)

    init = tuple((jnp.full((1, tq), NEG_INF, F32), jnp.zeros((1, tq), F32), jnp.zeros((d, tq), F32))
                 for _ in range(2 * n_pair))
    carry = lax.fori_loop(0, i, lambda j, c: step(j, c, False), init)
    final = step(i, carry, True)
    o_ref[...] = jnp.concatenate([acc / l for (_, l, acc) in final], axis=0).T


def _prompt_attention(qt, kt, vt, fcol, ft_pairs, *, batch, seq, n_heads, tq, n_pair):
    n_groups = n_heads // (2 * n_pair)
    qn = seq // tq
    rows = n_pair * V7X_LANES
    kern = functools.partial(_attn_kernel, tq=tq, seq=seq, n_pair=n_pair)
    return pl.pallas_call(
        kern,
        grid=(batch, n_groups, qn),
        in_specs=[
            pl.BlockSpec((None, rows, tq), lambda b, g, i: (b, g, i)),
            pl.BlockSpec((None, rows, seq), lambda b, g, i: (b, g, 0)),
            pl.BlockSpec((None, rows, seq), lambda b, g, i: (b, g, 0)),
            pl.BlockSpec((seq, V7X_LANES), lambda b, g, i: (b, 0)),
            pl.BlockSpec((None, n_pair, 2, seq), lambda b, g, i: (b, g, 0, 0)),
        ],
        out_specs=pl.BlockSpec((tq, rows), lambda b, g, i: (b * qn + i, g)),
        out_shape=jax.ShapeDtypeStruct((batch * seq, n_heads * HEAD_DIM), F32),
        scratch_shapes=[pltpu.VMEM((2 * n_pair, seq, V7X_LANES), BF16)],
        compiler_params=_cparams(3),
        name="prompt_attention",
    )(qt, kt, vt, fcol, ft_pairs)


def _decode_kernel(pt_ref, q_ref, kn_ref, vn_ref, lfn_ref, *rest, n_heads, n_new, n_pg):
    del pt_ref
    kc_refs = rest[0:n_pg]
    vc_refs = rest[n_pg:2 * n_pg]
    lf_refs = rest[2 * n_pg:3 * n_pg]
    o_ref, m_sc, l_sc, acc_sc, carry_sc, qbd_sc, qb_sc = rest[3 * n_pg:]
    j = pl.program_id(1)
    rows = n_heads * n_new
    feat = n_heads * HEAD_DIM

    def tile_rows(x):
        return jnp.concatenate([x] * n_heads, axis=0)

    def rep_rows(x):
        return jnp.concatenate([jnp.broadcast_to(x[h:h + 1, :], (n_new, x.shape[1])) for h in range(n_heads)], axis=0)

    def to_pages(x):
        xp = jnp.concatenate([x, jnp.zeros((PAGE_SIZE - n_new, feat), F32)], axis=0)
        return jnp.concatenate([xp[:, c * PAGE_SIZE:(c + 1) * PAGE_SIZE].T for c in range(feat // PAGE_SIZE)], axis=0)

    def attend(k_pages, v_pages, bias):
        s = _dot(qbd_sc[...], k_pages) + bias
        m_prev = m_sc[...]
        m_new = jnp.maximum(m_prev, jnp.max(s, axis=1, keepdims=True))
        alpha = jnp.exp(m_prev - m_new)
        p = jnp.exp(s - m_new)
        l_sc[...] = alpha * l_sc[...] + jnp.sum(p, axis=1, keepdims=True)
        m_sc[...] = m_new
        alpha_row = jnp.broadcast_to(alpha, (rows, rows)).T[0:1, :]
        pv = _dot(v_pages, p.T.astype(BF16))
        acc_sc[...] = alpha_row * acc_sc[...] + pv

    @pl.when(j == 0)
    def _():
        m_sc[...] = jnp.full(m_sc.shape, NEG_INF, F32)
        l_sc[...] = jnp.zeros(l_sc.shape, F32)
        acc_sc[...] = jnp.zeros(acc_sc.shape, F32)
        carry_sc[...] = jnp.zeros(carry_sc.shape, F32)
        row_head = lax.broadcasted_iota(jnp.int32, (rows, feat), 0) // n_new
        col_head = lax.broadcasted_iota(jnp.int32, (rows, feat), 1) // HEAD_DIM
        qbd_sc[...] = jnp.where(row_head == col_head, tile_rows(q_ref[...] * ATTN_SCALE), 0.0).astype(BF16)
        f_new = _scan_rows(lfn_ref[...])
        rh = lax.broadcasted_iota(jnp.int32, (rows, V7X_LANES), 0) // n_new
        ln = lax.broadcasted_iota(jnp.int32, (rows, V7X_LANES), 1)
        qb = jnp.sum(jnp.where(rh == ln, tile_rows(f_new), 0.0), axis=1, keepdims=True)
        qb_sc[...] = qb
        f_pad = jnp.concatenate([f_new, jnp.zeros((PAGE_SIZE - n_new, V7X_LANES), F32)], axis=0)
        f_new_t = f_pad.T[:n_heads, :]
        tok = lax.broadcasted_iota(jnp.int32, (rows, PAGE_SIZE), 0) % n_new
        pos = lax.broadcasted_iota(jnp.int32, (rows, PAGE_SIZE), 1)
        bias = (qb - rep_rows(f_new_t)) + jnp.where(pos <= tok, 0.0, NEG_INF)
        attend(to_pages(kn_ref[...]).astype(BF16), to_pages(vn_ref[...]).astype(BF16), bias)

    carry = carry_sc[...]
    biases = []
    for g in range(n_pg):
        lf = lf_refs[g][...]
        incl = _scan_lanes(lf, reverse=True)
        biases.append(rep_rows((incl - lf) + carry))
        carry = carry + incl[:, 0:1]
    carry_sc[...] = carry
    k_pages = jnp.concatenate([r[...].reshape(feat, PAGE_SIZE).astype(BF16) for r in kc_refs], axis=1)
    v_pages = jnp.concatenate([r[...].reshape(feat, PAGE_SIZE).astype(BF16) for r in vc_refs], axis=1)
    attend(k_pages, v_pages, qb_sc[...] + jnp.concatenate(biases, axis=1))

    @pl.when(j == pl.num_programs(1) - 1)
    def _():
        outs = []
        for h in range(n_heads):
            blk = acc_sc[h * HEAD_DIM:(h + 1) * HEAD_DIM, :].T
            outs.append(blk[h * n_new:(h + 1) * n_new, :])
        o_ref[...] = jnp.concatenate(outs, axis=0) / l_sc[...]


def _decode_attention(page_table, hs, lf_new, cache_kt, cache_vt, cache_lft, *, n_heads, n_new, n_pg, cols):
    dec_b, n_pages = page_table.shape
    rows = n_heads * n_new
    feat = n_heads * HEAD_DIM
    last = n_pages - 1
    kern = functools.partial(_decode_kernel, n_heads=n_heads, n_new=n_new, n_pg=n_pg)

    def page_spec(g, tail):
        return pl.BlockSpec((None, None, n_heads) + tail,
                            lambda b, j, pt: (0, pt[b, last - (j * n_pg + g)]) + (0,) * (1 + len(tail)))

    kv_tail = (HEAD_DIM, PAGE_SIZE)
    in_specs = [
        pl.BlockSpec((n_new, feat), lambda b, j, pt: (b, cols["q"] // feat)),
        pl.BlockSpec((n_new, feat), lambda b, j, pt: (b, cols["k"] // feat)),
        pl.BlockSpec((n_new, feat), lambda b, j, pt: (b, cols["v"] // feat)),
        pl.BlockSpec((n_new, V7X_LANES), lambda b, j, pt: (b, 0)),
    ]
    in_specs += [page_spec(g, kv_tail) for g in range(n_pg)]
    in_specs += [page_spec(g, kv_tail) for g in range(n_pg)]
    in_specs += [page_spec(g, (PAGE_SIZE,)) for g in range(n_pg)]
    grid_spec = pltpu.PrefetchScalarGridSpec(
        num_scalar_prefetch=1,
        grid=(dec_b, n_pages // n_pg),
        in_specs=in_specs,
        out_specs=pl.BlockSpec((None, rows, HEAD_DIM), lambda b, j, pt: (b, 0, 0)),
        scratch_shapes=[
            pltpu.VMEM((rows, 1), F32),
            pltpu.VMEM((rows, 1), F32),
            pltpu.VMEM((feat, rows), F32),
            pltpu.VMEM((n_heads, PAGE_SIZE), F32),
            pltpu.VMEM((rows, feat), BF16),
            pltpu.VMEM((rows, 1), F32),
        ],
    )
    args = [page_table, hs, hs, hs, lf_new] + [cache_kt] * n_pg + [cache_vt] * n_pg + [cache_lft] * n_pg
    return pl.pallas_call(
        kern,
        grid_spec=grid_spec,
        out_shape=jax.ShapeDtypeStruct((dec_b, rows, HEAD_DIM), F32),
        compiler_params=_cparams(2),
        name="decode_attention",
    )(*args)


def _pool_diff(z, u, pos, group_w):
    s = z
    k = 1
    while k < group_w:
        s = s + pltpu.roll(s, k, 0)
        k *= 2
    cnt = jnp.minimum(pos + 1, group_w).astype(F32)
    return s[HALO:, :] / cnt - u


def _mix_tail(d_groups, gp, ga, att, ma, mb, x, wpool_ref, ps_ref, wup_ref, wua_ref, wout_ref):
    mixed = [_dot(d.astype(BF16), wpool_ref[gi]) for gi, d in enumerate(d_groups)]
    mixed = jnp.concatenate(mixed, axis=1)
    branch_a = (mixed * ps_ref[...]) * _silu(gp)
    branch_b = att * _silu(ga)
    up_a = _dot(branch_a.astype(BF16), wup_ref[...])
    up_b = _dot(branch_b.astype(BF16), wua_ref[...])
    merged = _sigmoid(ma) * up_a + _sigmoid(mb) * up_b
    return x + _dot(merged.astype(BF16), wout_ref[...])


def _out_prompt_kernel(u_ref, halo_ref, gp_ref, ga_ref, ma_ref, mb_ref, att_ref, x_ref,
                       wpool_ref, ps_ref, wup_ref, wua_ref, wout_ref, y_ref, *, tm, seq):
    i = pl.program_id(0)
    pos0 = (i * tm) % seq
    u = u_ref[...]
    halo = jnp.where(pos0 == 0, 0.0, halo_ref[...])
    z = jnp.concatenate([halo, u], axis=0)
    pos = pos0 + lax.broadcasted_iota(jnp.int32, (tm, 1), 0)
    gw = u.shape[1] // len(POOL_WINDOWS)
    d_groups = [_pool_diff(z[:, gi * gw:(gi + 1) * gw], u[:, gi * gw:(gi + 1) * gw], pos, w)
                for gi, w in enumerate(POOL_WINDOWS)]
    y_ref[...] = _mix_tail(d_groups, gp_ref[...], ga_ref[...], att_ref[...], ma_ref[...], mb_ref[...], x_ref[...],
                           wpool_ref, ps_ref, wup_ref, wua_ref, wout_ref)


def _out_sample_kernel(z_ref, gp_ref, ga_ref, ma_ref, mb_ref, att_ref, x_ref,
                       wpool_ref, ps_ref, wup_ref, wua_ref, wout_ref, y_ref, *, dec_b, n_new, pos0):
    gw = z_ref.shape[2] // len(POOL_WINDOWS)
    pos = pos0 + lax.broadcasted_iota(jnp.int32, (n_new, 1), 0)
    per_seq = []
    for b in range(dec_b):
        z = z_ref[b]
        u = z[HALO:, :]
        per_seq.append([_pool_diff(z[:, gi * gw:(gi + 1) * gw], u[:, gi * gw:(gi + 1) * gw], pos, w)
                        for gi, w in enumerate(POOL_WINDOWS)])
    d_groups = [jnp.concatenate([per_seq[b][gi] for b in range(dec_b)], axis=0) for gi in range(len(POOL_WINDOWS))]
    y_ref[...] = _mix_tail(d_groups, gp_ref[...], ga_ref[...], att_ref[...], ma_ref[...], mb_ref[...], x_ref[...],
                           wpool_ref, ps_ref, wup_ref, wua_ref, wout_ref)


def _weight_specs(pw, gw, dm):
    one = pl.Buffered(1)
    return [
        pl.BlockSpec((len(POOL_WINDOWS), gw, gw), lambda i: (0, 0, 0), pipeline_mode=one),
        pl.BlockSpec((1, pw), lambda i: (0, 0), pipeline_mode=one),
        pl.BlockSpec((pw, dm), lambda i: (0, 0), pipeline_mode=one),
        pl.BlockSpec((pw, dm), lambda i: (0, 0), pipeline_mode=one),
        pl.BlockSpec((dm, dm), lambda i: (0, 0), pipeline_mode=one),
    ]


def _out_prompt(h, att, x2d, wpool, ps, wup, wua, wout, *, seq, tm, cols):
    m, dm = x2d.shape
    pw = att.shape[1]
    gw = pw // len(POOL_WINDOWS)
    hb = tm // HALO
    kern = functools.partial(_out_prompt_kernel, tm=tm, seq=seq)
    return pl.pallas_call(
        kern,
        grid=(m // tm,),
        in_specs=[
            pl.BlockSpec((tm, pw), lambda i: (i, cols["u"] // pw)),
            pl.BlockSpec((HALO, pw), lambda i: (jnp.maximum(i * hb - 1, 0), cols["u"] // pw)),
            pl.BlockSpec((tm, pw), lambda i: (i, cols["gp"] // pw)),
            pl.BlockSpec((tm, pw), lambda i: (i, cols["ga"] // pw)),
            pl.BlockSpec((tm, dm), lambda i: (i, cols["ma"] // dm)),
            pl.BlockSpec((tm, dm), lambda i: (i, cols["mb"] // dm)),
            pl.BlockSpec((tm, pw), lambda i: (i, 0)),
            pl.BlockSpec((tm, dm), lambda i: (i, 0)),
        ] + _weight_specs(pw, gw, dm),
        out_specs=pl.BlockSpec((tm, dm), lambda i: (i, 0)),
        out_shape=jax.ShapeDtypeStruct((m, dm), F32),
        compiler_params=_cparams(1),
        name="out_prompt",
    )(h, h, h, h, h, h, att, x2d, wpool, ps, wup, wua, wout)


def _out_sample(z, h, att, x2d, wpool, ps, wup, wua, wout, *, dec_b, n_new, pos0, cols):
    m, dm = x2d.shape
    pw = att.shape[1]
    gw = pw // len(POOL_WINDOWS)
    kern = functools.partial(_out_sample_kernel, dec_b=dec_b, n_new=n_new, pos0=pos0)
    return pl.pallas_call(
        kern,
        grid=(1,),
        in_specs=[
            pl.BlockSpec((dec_b, HALO + n_new, pw), lambda i: (0, 0, 0)),
            pl.BlockSpec((m, pw), lambda i: (0, cols["gp"] // pw)),
            pl.BlockSpec((m, pw), lambda i: (0, cols["ga"] // pw)),
            pl.BlockSpec((m, dm), lambda i: (0, cols["ma"] // dm)),
            pl.BlockSpec((m, dm), lambda i: (0, cols["mb"] // dm)),
            pl.BlockSpec((m, pw), lambda i: (0, 0)),
            pl.BlockSpec((m, dm), lambda i: (0, 0)),
        ] + _weight_specs(pw, gw, dm),
        out_specs=pl.BlockSpec((m, dm), lambda i: (0, 0)),
        out_shape=jax.ShapeDtypeStruct((m, dm), F32),
        compiler_params=_cparams(1),
        name="out_sample",
    )(z, h, h, h, h, att, x2d, wpool, ps, wup, wua, wout)


def kernel(x_prompt, x_sample, cache_k, cache_v, cache_logf, state_pool, page_table, norm_gain, w_in, b_f,
           q_norm_gain, k_norm_gain, w_pool_map, pool_scale, w_up_pool, w_up_attn, w_out):
    batch, seq, dm = x_prompt.shape
    dec_b, n_new, _ = x_sample.shape
    assert w_in.shape[0] == 1
    n_heads = b_f.shape[1]
    aw = n_heads * HEAD_DIM
    pw = w_up_pool.shape[1]
    n_pages = page_table.shape[1]
    past_len = n_pages * PAGE_SIZE
    assert pw == aw and dm == 2 * pw and n_pages % PAGES_PER_STEP == 0

    tn = 512
    plan = _inproj_plan(pw, aw, dm, tn)
    cols = plan["cols"]
    n_main = 2 * pw + 4 * aw
    w_t = w_in[0].T
    wm = w_t[:n_main].astype(BF16)
    wg = w_t[n_main + n_heads:].astype(BF16)
    wfl = jnp.pad(w_t[n_main:n_main + n_heads], ((0, V7X_LANES - n_heads), (0, 0))).astype(BF16)
    bfp = jnp.pad(b_f, ((0, 0), (0, V7X_LANES - n_heads)))
    reps = tn // HEAD_DIM
    qg = jnp.tile(q_norm_gain[0], reps)[None, :]
    kg = jnp.tile(k_norm_gain[0], reps)[None, :]
    qgc = jnp.broadcast_to(jnp.tile(q_norm_gain[0], reps)[:, None], (tn, V7X_LANES))
    kgc = jnp.broadcast_to(jnp.tile(k_norm_gain[0], reps)[:, None], (tn, V7X_LANES))
    seg = np.arange(tn) // HEAD_DIM
    bd = jnp.asarray((seg[:, None] == seg[None, :]).astype(np.float32)).astype(BF16)
    wpool = w_pool_map[0].astype(BF16)
    wup = w_up_pool[0].astype(BF16)
    wua = w_up_attn[0].astype(BF16)
    wout = w_out[0].astype(BF16)

    xp2 = x_prompt.reshape(batch * seq, dm)
    hp, lfp, qt_p, kt_p, vt_p = _inproj(plan, xp2, norm_gain, wm, wg, wfl, bfp, qg, kg, qgc, kgc, bd, tm=1024,
                                        feature_major=True, seq=seq)
    fcol, ft, lft = _fcum(lfp, batch=batch, seq=seq, n_heads=n_heads)
    ft_pairs = ft.reshape(batch, n_heads // 2, 2, seq)
    att_p = _prompt_attention(qt_p, kt_p, vt_p, fcol, ft_pairs, batch=batch, seq=seq, n_heads=n_heads, tq=256,
                              n_pair=ATTN_PAIRS_PER_STEP)
    yp = _out_prompt(hp, att_p, xp2, wpool, pool_scale, wup, wua, wout, seq=seq, tm=256, cols=cols)

    ms = dec_b * n_new
    xs2 = x_sample.reshape(ms, dm)
    hs, lfs = _inproj(plan, xs2, norm_gain, wm, wg, wfl, bfp, qg, kg, qgc, kgc, bd, tm=ms, feature_major=False)
    k_s = hs[:, cols["k"]:cols["k"] + aw].reshape(dec_b, n_new, n_heads, HEAD_DIM)
    v_s = hs[:, cols["v"]:cols["v"] + aw].reshape(dec_b, n_new, n_heads, HEAD_DIM)
    u_s = hs[:, cols["u"]:cols["u"] + pw].reshape(dec_b, n_new, pw)
    logf_s = lfs[:, :n_heads].reshape(dec_b, n_new, n_heads)
    cache_kt = cache_k.transpose(0, 1, 3, 4, 2)
    cache_vt = cache_v.transpose(0, 1, 3, 4, 2)
    cache_lft = cache_logf.transpose(0, 1, 3, 2)
    att_rows = _decode_attention(page_table, hs, lfs, cache_kt, cache_vt, cache_lft,
                                 n_heads=n_heads, n_new=n_new, n_pg=PAGES_PER_STEP, cols=cols)
    att_s = att_rows.reshape(dec_b, n_heads, n_new, HEAD_DIM).transpose(0, 2, 1, 3).reshape(ms, aw)
    z_s = jnp.concatenate([jnp.zeros((dec_b, HALO - POOL_BUF, pw), F32), state_pool[0], u_s], axis=1)
    ys = _out_sample(z_s, hs, att_s, xs2, wpool, pool_scale, wup, wua, wout,
                     dec_b=dec_b, n_new=n_new, pos0=past_len, cols=cols)

    k_p = kt_p.reshape(1, batch, n_heads, HEAD_DIM, seq).transpose(0, 1, 4, 2, 3)
    v_p = vt_p.reshape(1, batch, n_heads, HEAD_DIM, seq).transpose(0, 1, 4, 2, 3)
    logf_p = lft.transpose(0, 2, 1)[None]
    pool_p = hp.reshape(batch, seq, hp.shape[1])[:, seq - POOL_BUF:, cols["u"]:cols["u"] + pw][None]
    pool_s = z_s[:, HALO + n_new - POOL_BUF:, :][None]
    return (yp.reshape(batch, seq, dm), ys.reshape(dec_b, n_new, dm), k_p, v_p, logf_p, pool_p,
            k_s[None], v_s[None], logf_s[None], pool_s)
```

```python
import functools

import jax
import jax.numpy as jnp
import numpy as np
from jax import lax
from jax.experimental import pallas as pl
from jax.experimental.pallas import tpu as pltpu

F32 = jnp.float32
BF16 = jnp.bfloat16

HEAD_DIM = 64
POOL_WINDOWS = (2, 4, 8, 16)
POOL_BUF = 15
PAGE_SIZE = 128
EPS = 1e-6
NEG_INF = -1e30
ATTN_SCALE = HEAD_DIM ** -0.5
LOG2E = 1.4426950408889634

V7X_LANES = 128
V7X_SUBLANES = 8
V7X_VMEM_LIMIT_BYTES = 56 * 1024 * 1024

HALO = 16
PAGES_PER_STEP = 16
N_AUG = 3
ATTN_PAIRS_PER_STEP = 8


def _cparams(n_grid_axes):
    return pltpu.CompilerParams(
        dimension_semantics=("arbitrary",) * n_grid_axes,
        vmem_limit_bytes=V7X_VMEM_LIMIT_BYTES,
    )


def _split3(x):
    hi = x.astype(BF16)
    r1 = x - hi.astype(F32)
    mid = r1.astype(BF16)
    lo = (r1 - mid.astype(F32)).astype(BF16)
    return hi, mid, lo


def _dot(a, b):
    return jnp.dot(a, b, preferred_element_type=F32)


def _dot_nt(a, b):
    return lax.dot_general(a, b, (((1,), (1,)), ((), ())), preferred_element_type=F32)


def _sigmoid(x):
    return 1.0 / (1.0 + jnp.exp(-x))


def _silu(x):
    return x * _sigmoid(x)


def _scan_lanes(x, *, reverse):
    n = x.shape[-1]
    ax = x.ndim - 1
    lane = lax.broadcasted_iota(jnp.int32, x.shape, ax)
    k = 1
    while k < n:
        if reverse:
            shifted = pltpu.roll(x, n - k, ax)
            x = x + jnp.where(lane < n - k, shifted, 0.0)
        else:
            shifted = pltpu.roll(x, k, ax)
            x = x + jnp.where(lane >= k, shifted, 0.0)
        k *= 2
    return x


def _scan_rows(x):
    n = x.shape[0]
    row = lax.broadcasted_iota(jnp.int32, x.shape, 0)
    k = 1
    while k < n:
        x = x + jnp.where(row >= k, pltpu.roll(x, k, 0), 0.0)
        k *= 2
    return x


def _inproj_kernel(tm_ref, tg_ref, x_ref, ng_ref, wm_ref, wg_ref, wfl_ref, bf_ref, qg_ref, kg_ref, qgc_ref, kgc_ref,
                   bd_ref, *rest, steps, feature_major):
    del tm_ref, tg_ref
    if feature_major:
        h_ref, lf_ref, qt_ref, kt_ref, vt_ref, xn_sc = rest
    else:
        h_ref, lf_ref, xn_sc = rest
    j = pl.program_id(1)

    @pl.when(j == 0)
    def _():
        x = x_ref[...]
        ms = jnp.mean(x * x, axis=-1, keepdims=True)
        xn = x * lax.rsqrt(ms + EPS) * ng_ref[...]
        xn_sc[...] = xn.astype(BF16)
        z = _dot_nt(xn_sc[...], wfl_ref[...]) + bf_ref[...]
        lf_ref[...] = jnp.minimum(z, 0.0) - jnp.log1p(jnp.exp(-jnp.abs(z)))

    def in_range(name):
        lo, hi = steps[name]
        return jnp.logical_and(j >= lo, j < hi)

    def head_norm_rows(acc, gain):
        sq = acc * acc
        hi = sq.astype(BF16)
        lo = (sq - hi.astype(F32)).astype(BF16)
        ss = _dot(hi, bd_ref[...]) + _dot(lo, bd_ref[...])
        return acc * lax.rsqrt(ss * (1.0 / HEAD_DIM) + EPS) * gain

    def head_norm_cols(acc, gain_col_ref):
        tn, tm = acc.shape
        a3 = acc.reshape(tn // HEAD_DIM, HEAD_DIM, tm)
        ms = jnp.mean(a3 * a3, axis=1, keepdims=True)
        g3 = gain_col_ref[...].reshape(tn // HEAD_DIM, HEAD_DIM, V7X_LANES)[:, :, 0:1]
        return (a3 * lax.rsqrt(ms + EPS) * g3).reshape(tn, tm)

    @pl.when(in_range("gate"))
    def _():
        h_ref[...] = _dot_nt(xn_sc[...], wg_ref[...])

    @pl.when(in_range("plain"))
    def _():
        h_ref[...] = _dot_nt(xn_sc[...], wm_ref[...])

    if feature_major:
        @pl.when(in_range("q"))
        def _():
            qt_ref[...] = head_norm_cols(_dot_nt(wm_ref[...], xn_sc[...]), qgc_ref)

        @pl.when(in_range("k"))
        def _():
            kt_ref[...] = head_norm_cols(_dot_nt(wm_ref[...], xn_sc[...]), kgc_ref)

        @pl.when(in_range("v"))
        def _():
            vt_ref[...] = _dot_nt(wm_ref[...], xn_sc[...])
    else:
        @pl.when(in_range("q"))
        def _():
            h_ref[...] = head_norm_rows(_dot_nt(xn_sc[...], wm_ref[...]), qg_ref[...])

        @pl.when(in_range("k"))
        def _():
            h_ref[...] = head_norm_rows(_dot_nt(xn_sc[...], wm_ref[...]), kg_ref[...])

        @pl.when(in_range("v"))
        def _():
            h_ref[...] = _dot_nt(xn_sc[...], wm_ref[...])


def _inproj(plan, x2d, ng, wm, wg, wfl, bfp, qg, kg, qgc, kgc, bd, *, tm, feature_major, seq=None):
    m, d = x2d.shape
    tn = plan["tn"]
    steps = plan["steps"]
    nj = plan["n_steps"]
    kern = functools.partial(_inproj_kernel, steps=steps, feature_major=feature_major)
    in_specs = [
        pl.BlockSpec((tm, d), lambda i, j, t0, t1: (i, 0)),
        pl.BlockSpec((1, d), lambda i, j, t0, t1: (0, 0)),
        pl.BlockSpec((tn, d), lambda i, j, t0, t1: (t0[j], 0)),
        pl.BlockSpec((tn, d), lambda i, j, t0, t1: (t1[j], 0)),
        pl.BlockSpec((V7X_LANES, d), lambda i, j, t0, t1: (0, 0)),
        pl.BlockSpec((1, V7X_LANES), lambda i, j, t0, t1: (0, 0)),
        pl.BlockSpec((1, tn), lambda i, j, t0, t1: (0, 0)),
        pl.BlockSpec((1, tn), lambda i, j, t0, t1: (0, 0)),
        pl.BlockSpec((tn, V7X_LANES), lambda i, j, t0, t1: (0, 0)),
        pl.BlockSpec((tn, V7X_LANES), lambda i, j, t0, t1: (0, 0)),
        pl.BlockSpec((tn, tn), lambda i, j, t0, t1: (0, 0)),
    ]
    if feature_major:
        n_h = steps["q"][0]
        nb = seq // tm
        width = (steps["q"][1] - steps["q"][0]) * tn

        def fm_spec(name):
            lo, hi = steps[name]
            return pl.BlockSpec((None, tn, tm),
                                lambda i, j, t0, t1: (i // nb, jnp.clip(j - lo, 0, hi - lo - 1), i % nb))

        out_specs = [
            pl.BlockSpec((tm, tn), lambda i, j, t0, t1: (i, jnp.minimum(j, n_h - 1))),
            pl.BlockSpec((tm, V7X_LANES), lambda i, j, t0, t1: (i, 0)),
            fm_spec("q"), fm_spec("k"), fm_spec("v"),
        ]
        out_shape = [
            jax.ShapeDtypeStruct((m, n_h * tn), F32),
            jax.ShapeDtypeStruct((m, V7X_LANES), F32),
        ] + [jax.ShapeDtypeStruct((m // seq, width, seq), F32)] * 3
    else:
        out_specs = [
            pl.BlockSpec((tm, tn), lambda i, j, t0, t1: (i, j)),
            pl.BlockSpec((tm, V7X_LANES), lambda i, j, t0, t1: (i, 0)),
        ]
        out_shape = [
            jax.ShapeDtypeStruct((m, nj * tn), F32),
            jax.ShapeDtypeStruct((m, V7X_LANES), F32),
        ]
    grid_spec = pltpu.PrefetchScalarGridSpec(
        num_scalar_prefetch=2,
        grid=(m // tm, nj),
        in_specs=in_specs,
        out_specs=out_specs,
        scratch_shapes=[pltpu.VMEM((tm, d), BF16)],
    )
    return pl.pallas_call(
        kern,
        grid_spec=grid_spec,
        out_shape=out_shape,
        compiler_params=_cparams(2),
        name="inproj",
    )(plan["tbl_main"], plan["tbl_gate"], x2d, ng, wm, wg, wfl, bfp, qg, kg, qgc, kgc, bd)


def _inproj_plan(pw, aw, dm, tn):
    main_rows = {"u": 0, "gp": pw, "q": 2 * pw, "k": 2 * pw + aw, "v": 2 * pw + 2 * aw, "ga": 2 * pw + 3 * aw}
    seq_steps = []
    for name, size in (("ma", dm), ("mb", dm)):
        base = {"ma": 0, "mb": dm}[name]
        seq_steps += [("gate", (base + o) // tn) for o in range(0, size, tn)]
    n_gate = len(seq_steps)
    for name, size in (("u", pw), ("gp", pw), ("ga", aw), ("q", aw), ("k", aw), ("v", aw)):
        seq_steps += [("main", (main_rows[name] + o) // tn) for o in range(0, size, tn)]
    tbl_main, tbl_gate = [], []
    first_main = next(b for op, b in seq_steps if op == "main")
    cur_m, cur_g = first_main, 0
    for op, b in seq_steps:
        if op == "main":
            cur_m = b
        else:
            cur_g = b
        tbl_main.append(cur_m)
        tbl_gate.append(cur_g)
    n_plain = (2 * pw + aw) // tn
    q0 = n_gate + n_plain
    per = aw // tn
    steps = {"gate": (0, n_gate), "plain": (n_gate, q0), "q": (q0, q0 + per), "k": (q0 + per, q0 + 2 * per),
             "v": (q0 + 2 * per, q0 + 3 * per)}
    cols = {"ma": 0, "mb": dm, "u": 2 * dm, "gp": 2 * dm + pw, "ga": 2 * dm + 2 * pw,
            "q": q0 * tn, "k": (q0 + per) * tn, "v": (q0 + 2 * per) * tn}
    return {"tn": tn, "steps": steps, "n_steps": len(seq_steps), "cols": cols,
            "tbl_main": jnp.asarray(tbl_main, jnp.int32), "tbl_gate": jnp.asarray(tbl_gate, jnp.int32)}


def _fcum_kernel(lf_ref, fcol_ref, ft_ref, lft_ref, *, seq, n_heads):
    c = V7X_LANES
    row = lax.broadcasted_iota(jnp.int32, (c, c), 0)
    col = lax.broadcasted_iota(jnp.int32, (c, c), 1)
    tri = jnp.where(col <= row, 1.0, 0.0).astype(BF16)
    carry = jnp.zeros((1, c), F32)
    for ci in range(seq // c):
        x = lf_ref[ci * c:(ci + 1) * c, :]
        hi, mid, lo = _split3(x)
        fc = (_dot(tri, hi) + _dot(tri, mid)) + _dot(tri, lo) + carry
        fcol_ref[ci * c:(ci + 1) * c, :] = fc
        ft_ref[:, ci * c:(ci + 1) * c] = fc.T[:n_heads, :]
        lft_ref[:, ci * c:(ci + 1) * c] = x.T[:n_heads, :]
        carry = fc[c - 1:c, :]


def _fcum(lf2d, *, batch, seq, n_heads):
    kern = functools.partial(_fcum_kernel, seq=seq, n_heads=n_heads)
    return pl.pallas_call(
        kern,
        grid=(batch,),
        in_specs=[pl.BlockSpec((seq, V7X_LANES), lambda b: (b, 0))],
        out_specs=[
            pl.BlockSpec((seq, V7X_LANES), lambda b: (b, 0)),
            pl.BlockSpec((None, n_heads, seq), lambda b: (b, 0, 0)),
            pl.BlockSpec((None, n_heads, seq), lambda b: (b, 0, 0)),
        ],
        out_shape=[
            jax.ShapeDtypeStruct((batch * seq, V7X_LANES), F32),
            jax.ShapeDtypeStruct((batch, n_heads, seq), F32),
            jax.ShapeDtypeStruct((batch, n_heads, seq), F32),
        ],
        compiler_params=_cparams(1),
        name="forget_cumsum",
    )(lf2d)


def _attn_kernel(qt_ref, kt_ref, vt_ref, fcol_ref, ft_ref, o_ref, ka_sc, *, tq, seq, n_pair):
    g = pl.program_id(1)
    i = pl.program_id(2)
    d = HEAD_DIM
    lanes = V7X_LANES
    lane1 = lax.broadcasted_iota(jnp.int32, (1, lanes), 1)

    @pl.when(i == 0)
    def _():
        hrow = lax.broadcasted_iota(jnp.int32, (lanes, lanes), 0)
        lane = lax.broadcasted_iota(jnp.int32, (lanes, lanes), 1)

        def sel(head, lane0, piece):
            return jnp.where((hrow == head) & (lane == lane0 + N_AUG + piece), -1.0, 0.0).astype(BF16)

        ones0 = jnp.where((lane1 >= d) & (lane1 < d + N_AUG), 1.0, 0.0)
        ones1 = jnp.where(lane1 < N_AUG, 1.0, 0.0)
        ck = 256
        for c in range(seq // ck):
            pieces = _split3(fcol_ref[c * ck:(c + 1) * ck, :] * LOG2E)
            for pr in range(n_pair):
                head0 = 2 * (g * n_pair + pr)
                k_rows = kt_ref[pr * lanes:(pr + 1) * lanes, c * ck:(c + 1) * ck].T
                aug0 = ones0
                aug1 = ones1
                for pi, piece in enumerate(pieces):
                    aug0 = aug0 + _dot(piece, sel(head0, d, pi))
                    aug1 = aug1 + _dot(piece, sel(head0 + 1, 0, pi))
                ka_sc[2 * pr, c * ck:(c + 1) * ck, :] = jnp.where(lane1 < d, k_rows, aug0).astype(BF16)
                ka_sc[2 * pr + 1, c * ck:(c + 1) * ck, :] = jnp.where(lane1 >= d, k_rows, aug1).astype(BF16)

    qs = pl.multiple_of(i * tq, tq)
    rowi = lax.broadcasted_iota(jnp.int32, (d, tq), 0)

    def aug_rows(f_row):
        hi, mid, lo = (x.astype(F32) for x in _split3(f_row))
        return jnp.where(rowi == 0, hi, jnp.where(rowi == 1, mid, jnp.where(rowi == 2, lo,
                         jnp.where(rowi < 2 * N_AUG, 1.0, 0.0))))

    qa = []
    for pr in range(n_pair):
        qt = qt_ref[pr * lanes:(pr + 1) * lanes, :] * (ATTN_SCALE * LOG2E)
        fq = ft_ref[pr, :, pl.ds(qs, tq)] * LOG2E
        qa.append(jnp.concatenate([qt[:d, :], aug_rows(fq[0:1, :])], axis=0).astype(BF16))
        qa.append(jnp.concatenate([aug_rows(fq[1:2, :]), qt[d:, :]], axis=0).astype(BF16))
    krow = lax.broadcasted_iota(jnp.int32, (tq, tq), 0)
    qcol = lax.broadcasted_iota(jnp.int32, (tq, tq), 1)
    causal = krow <= qcol

    def step(j, carry, masked):
        ks = pl.multiple_of(j * tq, tq)
        heads = range(2 * n_pair)
        scores = [_dot(ka_sc[hh, pl.ds(ks, tq), :], qa[hh]) for hh in heads]
        stats, probs = [], []
        for hh in heads:
            m, l, _ = carry[hh]
            s = jnp.where(causal, scores[hh], NEG_INF) if masked else scores[hh]
            m_new = jnp.maximum(m, jnp.max(s, axis=0, keepdims=True))
            alpha = jnp.exp2(m - m_new)
            pt = jnp.exp2(s - m_new)
            stats.append((m_new, alpha * l + jnp.sum(pt, axis=0, keepdims=True), alpha))
            probs.append(pt.astype(BF16))
        pvs = [_dot(vt_ref[hh * d:(hh + 1) * d, pl.ds(ks, tq)].astype(BF16), probs[hh]) for hh in heads]
        return tuple((stats[hh][0], stats[hh][1], stats[hh][2] * carry[hh][2] + pvs[hh]) for hh in heads)

    init = tuple((jnp.full((1, tq), NEG_INF, F32), jnp.zeros((1, tq), F32), jnp.zeros((d, tq), F32))
                 for _ in range(2 * n_pair))
    carry = lax.fori_loop(0, i, lambda j, c: step(j, c, False), init)
    final = step(i, carry, True)
    o_ref[...] = jnp.concatenate([acc / l for (_, l, acc) in final], axis=0).T


def _prompt_attention(qt, kt, vt, fcol, ft_pairs, *, batch, seq, n_heads, tq, n_pair):
    n_groups = n_heads // (2 * n_pair)
    qn = seq // tq
    rows = n_pair * V7X_LANES
    kern = functools.partial(_attn_kernel, tq=tq, seq=seq, n_pair=n_pair)
    return pl.pallas_call(
        kern,
        grid=(batch, n_groups, qn),
        in_specs=[
            pl.BlockSpec((None, rows, tq), lambda b, g, i: (b, g, i)),
            pl.BlockSpec((None, rows, seq), lambda b, g, i: (b, g, 0)),
            pl.BlockSpec((None, rows, seq), lambda b, g, i: (b, g, 0)),
            pl.BlockSpec((seq, V7X_LANES), lambda b, g, i: (b, 0)),
            pl.BlockSpec((None, n_pair, 2, seq), lambda b, g, i: (b, g, 0, 0)),
        ],
        out_specs=pl.BlockSpec((tq, rows), lambda b, g, i: (b * qn + i, g)),
        out_shape=jax.ShapeDtypeStruct((batch * seq, n_heads * HEAD_DIM), F32),
        scratch_shapes=[pltpu.VMEM((2 * n_pair, seq, V7X_LANES), BF16)],
        compiler_params=_cparams(3),
        name="prompt_attention",
    )(qt, kt, vt, fcol, ft_pairs)


def _decode_kernel(pt_ref, q_ref, kn_ref, vn_ref, lfn_ref, *rest, n_heads, n_new, n_pg):
    del pt_ref
    kc_refs = rest[0:n_pg]
    vc_refs = rest[n_pg:2 * n_pg]
    lf_refs = rest[2 * n_pg:3 * n_pg]
    o_ref, m_sc, l_sc, acc_sc, carry_sc, qbd_sc, qb_sc = rest[3 * n_pg:]
    j = pl.program_id(1)
    rows = n_heads * n_new
    feat = n_heads * HEAD_DIM

    def tile_rows(x):
        return jnp.concatenate([x] * n_heads, axis=0)

    def rep_rows(x):
        return jnp.concatenate([jnp.broadcast_to(x[h:h + 1, :], (n_new, x.shape[1])) for h in range(n_heads)], axis=0)

    def to_pages(x):
        xp = jnp.concatenate([x, jnp.zeros((PAGE_SIZE - n_new, feat), F32)], axis=0)
        return jnp.concatenate([xp[:, c * PAGE_SIZE:(c + 1) * PAGE_SIZE].T for c in range(feat // PAGE_SIZE)], axis=0)

    def attend(k_pages, v_pages, bias):
        s = _dot(qbd_sc[...], k_pages) + bias
        m_prev = m_sc[...]
        m_new = jnp.maximum(m_prev, jnp.max(s, axis=1, keepdims=True))
        alpha = jnp.exp(m_prev - m_new)
        p = jnp.exp(s - m_new)
        l_sc[...] = alpha * l_sc[...] + jnp.sum(p, axis=1, keepdims=True)
        m_sc[...] = m_new
        alpha_row = jnp.broadcast_to(alpha, (rows, rows)).T[0:1, :]
        pv = _dot(v_pages, p.T.astype(BF16))
        acc_sc[...] = alpha_row * acc_sc[...] + pv

    @pl.when(j == 0)
    def _():
        m_sc[...] = jnp.full(m_sc.shape, NEG_INF, F32)
        l_sc[...] = jnp.zeros(l_sc.shape, F32)
        acc_sc[...] = jnp.zeros(acc_sc.shape, F32)
        carry_sc[...] = jnp.zeros(carry_sc.shape, F32)
        row_head = lax.broadcasted_iota(jnp.int32, (rows, feat), 0) // n_new
        col_head = lax.broadcasted_iota(jnp.int32, (rows, feat), 1) // HEAD_DIM
        qbd_sc[...] = jnp.where(row_head == col_head, tile_rows(q_ref[...] * ATTN_SCALE), 0.0).astype(BF16)
        f_new = _scan_rows(lfn_ref[...])
        rh = lax.broadcasted_iota(jnp.int32, (rows, V7X_LANES), 0) // n_new
        ln = lax.broadcasted_iota(jnp.int32, (rows, V7X_LANES), 1)
        qb = jnp.sum(jnp.where(rh == ln, tile_rows(f_new), 0.0), axis=1, keepdims=True)
        qb_sc[...] = qb
        f_pad = jnp.concatenate([f_new, jnp.zeros((PAGE_SIZE - n_new, V7X_LANES), F32)], axis=0)
        f_new_t = f_pad.T[:n_heads, :]
        tok = lax.broadcasted_iota(jnp.int32, (rows, PAGE_SIZE), 0) % n_new
        pos = lax.broadcasted_iota(jnp.int32, (rows, PAGE_SIZE), 1)
        bias = (qb - rep_rows(f_new_t)) + jnp.where(pos <= tok, 0.0, NEG_INF)
        attend(to_pages(kn_ref[...]).astype(BF16), to_pages(vn_ref[...]).astype(BF16), bias)

    carry = carry_sc[...]
    biases = []
    for g in range(n_pg):
        lf = lf_refs[g][...]
        incl = _scan_lanes(lf, reverse=True)
        biases.append(rep_rows((incl - lf) + carry))
        carry = carry + incl[:, 0:1]
    carry_sc[...] = carry
    k_pages = jnp.concatenate([r[...].reshape(feat, PAGE_SIZE).astype(BF16) for r in kc_refs], axis=1)
    v_pages = jnp.concatenate([r[...].reshape(feat, PAGE_SIZE).astype(BF16) for r in vc_refs], axis=1)
    attend(k_pages, v_pages, qb_sc[...] + jnp.concatenate(biases, axis=1))

    @pl.when(j == pl.num_programs(1) - 1)
    def _():
        outs = []
        for h in range(n_heads):
            blk = acc_sc[h * HEAD_DIM:(h + 1) * HEAD_DIM, :].T
            outs.append(blk[h * n_new:(h + 1) * n_new, :])
        o_ref[...] = jnp.concatenate(outs, axis=0) / l_sc[...]


def _decode_attention(page_table, hs, lf_new, cache_kt, cache_vt, cache_lft, *, n_heads, n_new, n_pg, cols):
    dec_b, n_pages = page_table.shape
    rows = n_heads * n_new
    feat = n_heads * HEAD_DIM
    last = n_pages - 1
    kern = functools.partial(_decode_kernel, n_heads=n_heads, n_new=n_new, n_pg=n_pg)

    def page_spec(g, tail):
        return pl.BlockSpec((None, None, n_heads) + tail,
                            lambda b, j, pt: (0, pt[b, last - (j * n_pg + g)]) + (0,) * (1 + len(tail)))

    kv_tail = (HEAD_DIM, PAGE_SIZE)
    in_specs = [
        pl.BlockSpec((n_new, feat), lambda b, j, pt: (b, cols["q"] // feat)),
        pl.BlockSpec((n_new, feat), lambda b, j, pt: (b, cols["k"] // feat)),
        pl.BlockSpec((n_new, feat), lambda b, j, pt: (b, cols["v"] // feat)),
        pl.BlockSpec((n_new, V7X_LANES), lambda b, j, pt: (b, 0)),
    ]
    in_specs += [page_spec(g, kv_tail) for g in range(n_pg)]
    in_specs += [page_spec(g, kv_tail) for g in range(n_pg)]
    in_specs += [page_spec(g, (PAGE_SIZE,)) for g in range(n_pg)]
    grid_spec = pltpu.PrefetchScalarGridSpec(
        num_scalar_prefetch=1,
        grid=(dec_b, n_pages // n_pg),
        in_specs=in_specs,
        out_specs=pl.BlockSpec((None, rows, HEAD_DIM), lambda b, j, pt: (b, 0, 0)),
        scratch_shapes=[
            pltpu.VMEM((rows, 1), F32),
            pltpu.VMEM((rows, 1), F32),
            pltpu.VMEM((feat, rows), F32),
            pltpu.VMEM((n_heads, PAGE_SIZE), F32),
            pltpu.VMEM((rows, feat), BF16),
            pltpu.VMEM((rows, 1), F32),
        ],
    )
    args = [page_table, hs, hs, hs, lf_new] + [cache_kt] * n_pg + [cache_vt] * n_pg + [cache_lft] * n_pg
    return pl.pallas_call(
        kern,
        grid_spec=grid_spec,
        out_shape=jax.ShapeDtypeStruct((dec_b, rows, HEAD_DIM), F32),
        compiler_params=_cparams(2),
        name="decode_attention",
    )(*args)


def _pool_diff(z, u, pos, group_w):
    s = z
    k = 1
    while k < group_w:
        s = s + pltpu.roll(s, k, 0)
        k *= 2
    cnt = jnp.minimum(pos + 1, group_w).astype(F32)
    return s[HALO:, :] / cnt - u


def _mix_tail(d_groups, gp, ga, att, ma, mb, x, wpool_ref, ps_ref, wup_ref, wua_ref, wout_ref):
    mixed = [_dot(d.astype(BF16), wpool_ref[gi]) for gi, d in enumerate(d_groups)]
    mixed = jnp.concatenate(mixed, axis=1)
    branch_a = (mixed * ps_ref[...]) * _silu(gp)
    branch_b = att * _silu(ga)
    up_a = _dot(branch_a.astype(BF16), wup_ref[...])
    up_b = _dot(branch_b.astype(BF16), wua_ref[...])
    merged = _sigmoid(ma) * up_a + _sigmoid(mb) * up_b
    return x + _dot(merged.astype(BF16), wout_ref[...])


def _out_prompt_kernel(u_ref, halo_ref, gp_ref, ga_ref, ma_ref, mb_ref, att_ref, x_ref,
                       wpool_ref, ps_ref, wup_ref, wua_ref, wout_ref, y_ref, *, tm, seq):
    i = pl.program_id(0)
    pos0 = (i * tm) % seq
    u = u_ref[...]
    halo = jnp.where(pos0 == 0, 0.0, halo_ref[...])
    z = jnp.concatenate([halo, u], axis=0)
    pos = pos0 + lax.broadcasted_iota(jnp.int32, (tm, 1), 0)
    gw = u.shape[1] // len(POOL_WINDOWS)
    d_groups = [_pool_diff(z[:, gi * gw:(gi + 1) * gw], u[:, gi * gw:(gi + 1) * gw], pos, w)
                for gi, w in enumerate(POOL_WINDOWS)]
    y_ref[...] = _mix_tail(d_groups, gp_ref[...], ga_ref[...], att_ref[...], ma_ref[...], mb_ref[...], x_ref[...],
                           wpool_ref, ps_ref, wup_ref, wua_ref, wout_ref)


def _out_sample_kernel(z_ref, gp_ref, ga_ref, ma_ref, mb_ref, att_ref, x_ref,
                       wpool_ref, ps_ref, wup_ref, wua_ref, wout_ref, y_ref, *, dec_b, n_new, pos0):
    gw = z_ref.shape[2] // len(POOL_WINDOWS)
    pos = pos0 + lax.broadcasted_iota(jnp.int32, (n_new, 1), 0)
    per_seq = []
    for b in range(dec_b):
        z = z_ref[b]
        u = z[HALO:, :]
        per_seq.append([_pool_diff(z[:, gi * gw:(gi + 1) * gw], u[:, gi * gw:(gi + 1) * gw], pos, w)
                        for gi, w in enumerate(POOL_WINDOWS)])
    d_groups = [jnp.concatenate([per_seq[b][gi] for b in range(dec_b)], axis=0) for gi in range(len(POOL_WINDOWS))]
    y_ref[...] = _mix_tail(d_groups, gp_ref[...], ga_ref[...], att_ref[...], ma_ref[...], mb_ref[...], x_ref[...],
                           wpool_ref, ps_ref, wup_ref, wua_ref, wout_ref)


def _weight_specs(pw, gw, dm):
    one = pl.Buffered(1)
    return [
        pl.BlockSpec((len(POOL_WINDOWS), gw, gw), lambda i: (0, 0, 0), pipeline_mode=one),
        pl.BlockSpec((1, pw), lambda i: (0, 0), pipeline_mode=one),
        pl.BlockSpec((pw, dm), lambda i: (0, 0), pipeline_mode=one),
        pl.BlockSpec((pw, dm), lambda i: (0, 0), pipeline_mode=one),
        pl.BlockSpec((dm, dm), lambda i: (0, 0), pipeline_mode=one),
    ]


def _out_prompt(h, att, x2d, wpool, ps, wup, wua, wout, *, seq, tm, cols):
    m, dm = x2d.shape
    pw = att.shape[1]
    gw = pw // len(POOL_WINDOWS)
    hb = tm // HALO
    kern = functools.partial(_out_prompt_kernel, tm=tm, seq=seq)
    return pl.pallas_call(
        kern,
        grid=(m // tm,),
        in_specs=[
            pl.BlockSpec((tm, pw), lambda i: (i, cols["u"] // pw)),
            pl.BlockSpec((HALO, pw), lambda i: (jnp.maximum(i * hb - 1, 0), cols["u"] // pw)),
            pl.BlockSpec((tm, pw), lambda i: (i, cols["gp"] // pw)),
            pl.BlockSpec((tm, pw), lambda i: (i, cols["ga"] // pw)),
            pl.BlockSpec((tm, dm), lambda i: (i, cols["ma"] // dm)),
            pl.BlockSpec((tm, dm), lambda i: (i, cols["mb"] // dm)),
            pl.BlockSpec((tm, pw), lambda i: (i, 0)),
            pl.BlockSpec((tm, dm), lambda i: (i, 0)),
        ] + _weight_specs(pw, gw, dm),
        out_specs=pl.BlockSpec((tm, dm), lambda i: (i, 0)),
        out_shape=jax.ShapeDtypeStruct((m, dm), F32),
        compiler_params=_cparams(1),
        name="out_prompt",
    )(h, h, h, h, h, h, att, x2d, wpool, ps, wup, wua, wout)


def _out_sample(z, h, att, x2d, wpool, ps, wup, wua, wout, *, dec_b, n_new, pos0, cols):
    m, dm = x2d.shape
    pw = att.shape[1]
    gw = pw // len(POOL_WINDOWS)
    kern = functools.partial(_out_sample_kernel, dec_b=dec_b, n_new=n_new, pos0=pos0)
    return pl.pallas_call(
        kern,
        grid=(1,),
        in_specs=[
            pl.BlockSpec((dec_b, HALO + n_new, pw), lambda i: (0, 0, 0)),
            pl.BlockSpec((m, pw), lambda i: (0, cols["gp"] // pw)),
            pl.BlockSpec((m, pw), lambda i: (0, cols["ga"] // pw)),
            pl.BlockSpec((m, dm), lambda i: (0, cols["ma"] // dm)),
            pl.BlockSpec((m, dm), lambda i: (0, cols["mb"] // dm)),
            pl.BlockSpec((m, pw), lambda i: (0, 0)),
            pl.BlockSpec((m, dm), lambda i: (0, 0)),
        ] + _weight_specs(pw, gw, dm),
        out_specs=pl.BlockSpec((m, dm), lambda i: (0, 0)),
        out_shape=jax.ShapeDtypeStruct((m, dm), F32),
        compiler_params=_cparams(1),
        name="out_sample",
    )(z, h, h, h, h, att, x2d, wpool, ps, wup, wua, wout)


def kernel(x_prompt, x_sample, cache_k, cache_v, cache_logf, state_pool, page_table, norm_gain, w_in, b_f,
           q_norm_gain, k_norm_gain, w_pool_map, pool_scale, w_up_pool, w_up_attn, w_out):
    batch, seq, dm = x_prompt.shape
    dec_b, n_new, _ = x_sample.shape
    assert w_in.shape[0] == 1
    n_heads = b_f.shape[1]
    aw = n_heads * HEAD_DIM
    pw = w_up_pool.shape[1]
    n_pages = page_table.shape[1]
    past_len = n_pages * PAGE_SIZE
    assert pw == aw and dm == 2 * pw and n_pages % PAGES_PER_STEP == 0

    tn = 512
    plan = _inproj_plan(pw, aw, dm, tn)
    cols = plan["cols"]
    n_main = 2 * pw + 4 * aw
    w_t = w_in[0].T
    wm = w_t.astype(BF16)
    wg = wm[n_main + n_heads:]
    wfl = jnp.pad(wm[n_main:n_main + n_heads], ((0, V7X_LANES - n_heads), (0, 0)))
    bfp = jnp.pad(b_f, ((0, 0), (0, V7X_LANES - n_heads)))
    reps = tn // HEAD_DIM
    qg = jnp.tile(q_norm_gain[0], reps)[None, :]
    kg = jnp.tile(k_norm_gain[0], reps)[None, :]
    qgc = jnp.broadcast_to(jnp.tile(q_norm_gain[0], reps)[:, None], (tn, V7X_LANES))
    kgc = jnp.broadcast_to(jnp.tile(k_norm_gain[0], reps)[:, None], (tn, V7X_LANES))
    seg = np.arange(tn) // HEAD_DIM
    bd = jnp.asarray((seg[:, None] == seg[None, :]).astype(np.float32)).astype(BF16)
    wpool = w_pool_map[0].astype(BF16)
    wup = w_up_pool[0].astype(BF16)
    wua = w_up_attn[0].astype(BF16)
    wout = w_out[0].astype(BF16)

    xp2 = x_prompt.reshape(batch * seq, dm)
    hp, lfp, qt_p, kt_p, vt_p = _inproj(plan, xp2, norm_gain, wm, wg, wfl, bfp, qg, kg, qgc, kgc, bd, tm=1024,
                                        feature_major=True, seq=seq)
    fcol, ft, lft = _fcum(lfp, batch=batch, seq=seq, n_heads=n_heads)
    ft_pairs = ft.reshape(batch, n_heads // 2, 2, seq)
    att_p = _prompt_attention(qt_p, kt_p, vt_p, fcol, ft_pairs, batch=batch, seq=seq, n_heads=n_heads, tq=256,
                              n_pair=ATTN_PAIRS_PER_STEP)
    yp = _out_prompt(hp, att_p, xp2, wpool, pool_scale, wup, wua, wout, seq=seq, tm=256, cols=cols)

    ms = dec_b * n_new
    xs2 = x_sample.reshape(ms, dm)
    hs, lfs = _inproj(plan, xs2, norm_gain, wm, wg, wfl, bfp, qg, kg, qgc, kgc, bd, tm=ms, feature_major=False)
    k_s = hs[:, cols["k"]:cols["k"] + aw].reshape(dec_b, n_new, n_heads, HEAD_DIM)
    v_s = hs[:, cols["v"]:cols["v"] + aw].reshape(dec_b, n_new, n_heads, HEAD_DIM)
    u_s = hs[:, cols["u"]:cols["u"] + pw].reshape(dec_b, n_new, pw)
    logf_s = lfs[:, :n_heads].reshape(dec_b, n_new, n_heads)
    cache_kt = cache_k.transpose(0, 1, 3, 4, 2)
    cache_vt = cache_v.transpose(0, 1, 3, 4, 2)
    cache_lft = cache_logf.transpose(0, 1, 3, 2)
    att_rows = _decode_attention(page_table, hs, lfs, cache_kt, cache_vt, cache_lft,
                                 n_heads=n_heads, n_new=n_new, n_pg=PAGES_PER_STEP, cols=cols)
    att_s = att_rows.reshape(dec_b, n_heads, n_new, HEAD_DIM).transpose(0, 2, 1, 3).reshape(ms, aw)
    z_s = jnp.concatenate([jnp.zeros((dec_b, HALO - POOL_BUF, pw), F32), state_pool[0], u_s], axis=1)
    ys = _out_sample(z_s, hs, att_s, xs2, wpool, pool_scale, wup, wua, wout,
                     dec_b=dec_b, n_new=n_new, pos0=past_len, cols=cols)

    k_p = kt_p.reshape(1, batch, n_heads, HEAD_DIM, seq).transpose(0, 1, 4, 2, 3)
    v_p = vt_p.reshape(1, batch, n_heads, HEAD_DIM, seq).transpose(0, 1, 4, 2, 3)
    logf_p = lft.transpose(0, 2, 1)[None]
    pool_p = hp.reshape(batch, seq, hp.shape[1])[:, seq - POOL_BUF:, cols["u"]:cols["u"] + pw][None]
    pool_s = z_s[:, HALO + n_new - POOL_BUF:, :][None]
    return (yp.reshape(batch, seq, dm), ys.reshape(dec_b, n_new, dm), k_p, v_p, logf_p, pool_p,
            k_s[None], v_s[None], logf_s[None], pool_s)
```

```python
import functools

import jax
import jax.numpy as jnp
import numpy as np
from jax import lax
from jax.experimental import pallas as pl
from jax.experimental.pallas import tpu as pltpu

F32 = jnp.float32
BF16 = jnp.bfloat16

HEAD_DIM = 64
POOL_WINDOWS = (2, 4, 8, 16)
POOL_BUF = 15
PAGE_SIZE = 128
EPS = 1e-6
NEG_INF = -1e30
ATTN_SCALE = HEAD_DIM ** -0.5
LOG2E = 1.4426950408889634

V7X_LANES = 128
V7X_SUBLANES = 8
V7X_VMEM_LIMIT_BYTES = 56 * 1024 * 1024

ROW_ALIGN = 16
HALO = 16
PAGES_PER_STEP = 16
N_AUG = 3
ATTN_PAIRS_PER_STEP = 8


def _cparams(n_grid_axes):
    return pltpu.CompilerParams(
        dimension_semantics=("arbitrary",) * n_grid_axes,
        vmem_limit_bytes=V7X_VMEM_LIMIT_BYTES,
    )


def _split3(x):
    hi = x.astype(BF16)
    r1 = x - hi.astype(F32)
    mid = r1.astype(BF16)
    lo = (r1 - mid.astype(F32)).astype(BF16)
    return hi, mid, lo


def _dot(a, b):
    return jnp.dot(a, b, preferred_element_type=F32)


def _dot_nt(a, b):
    return lax.dot_general(a, b, (((1,), (1,)), ((), ())), preferred_element_type=F32)


def _sigmoid(x):
    return 1.0 / (1.0 + jnp.exp(-x))


def _silu(x):
    return x * _sigmoid(x)


def _scan_lanes(x, *, reverse):
    n = x.shape[-1]
    ax = x.ndim - 1
    lane = lax.broadcasted_iota(jnp.int32, x.shape, ax)
    k = 1
    while k < n:
        if reverse:
            shifted = pltpu.roll(x, n - k, ax)
            x = x + jnp.where(lane < n - k, shifted, 0.0)
        else:
            shifted = pltpu.roll(x, k, ax)
            x = x + jnp.where(lane >= k, shifted, 0.0)
        k *= 2
    return x


def _scan_rows(x):
    n = x.shape[0]
    row = lax.broadcasted_iota(jnp.int32, x.shape, 0)
    k = 1
    while k < n:
        x = x + jnp.where(row >= k, pltpu.roll(x, k, 0), 0.0)
        k *= 2
    return x


def _norm_and_logf(x_ref, ng_ref, wfl_ref, bf_ref, xn_sc, lf_ref):
    x = x_ref[...]
    ms = jnp.mean(x * x, axis=-1, keepdims=True)
    xn = x * lax.rsqrt(ms + EPS) * ng_ref[...]
    xn_sc[...] = xn.astype(BF16)
    z = _dot_nt(xn_sc[...], wfl_ref[...]) + bf_ref[...]
    lf_ref[...] = jnp.minimum(z, 0.0) - jnp.log1p(jnp.exp(-jnp.abs(z)))


def _step_in(j, rng):
    return jnp.logical_and(j >= rng[0], j < rng[1])


def _inproj_new_kernel(off_ref, x_ref, ng_ref, w_ref, wfl_ref, bf_ref, qg_ref, kg_ref, bd_ref,
                       h_ref, lf_ref, wbf_ref, xn_sc, *, steps):
    del off_ref
    j = pl.program_id(0)

    @pl.when(j == 0)
    def _():
        _norm_and_logf(x_ref, ng_ref, wfl_ref, bf_ref, xn_sc, lf_ref)

    w = w_ref[...].astype(BF16)
    wbf_ref[...] = w
    acc = _dot_nt(xn_sc[...], w)

    def head_norm(gain):
        sq = acc * acc
        hi = sq.astype(BF16)
        lo = (sq - hi.astype(F32)).astype(BF16)
        ss = _dot(hi, bd_ref[...]) + _dot(lo, bd_ref[...])
        return acc * lax.rsqrt(ss * (1.0 / HEAD_DIM) + EPS) * gain

    is_q = _step_in(j, steps["q"])
    is_k = _step_in(j, steps["k"])

    @pl.when(is_q)
    def _():
        h_ref[...] = head_norm(qg_ref[...])

    @pl.when(is_k)
    def _():
        h_ref[...] = head_norm(kg_ref[...])

    @pl.when(jnp.logical_not(is_q | is_k))
    def _():
        h_ref[...] = acc


def _inproj_new(plan, x2d, ng, w_t, wfl, bfp, qg, kg, bd):
    m, d = x2d.shape
    tn = plan["tn"]
    nj = plan["n_steps"]
    kern = functools.partial(_inproj_new_kernel, steps=plan["steps"])
    grid_spec = pltpu.PrefetchScalarGridSpec(
        num_scalar_prefetch=1,
        grid=(nj,),
        in_specs=[
            pl.BlockSpec((m, d), lambda j, off: (0, 0)),
            pl.BlockSpec((1, d), lambda j, off: (0, 0)),
            pl.BlockSpec((pl.Element(tn), pl.Element(d)), lambda j, off: (off[j] * ROW_ALIGN, 0)),
            pl.BlockSpec((V7X_LANES, d), lambda j, off: (0, 0)),
            pl.BlockSpec((1, V7X_LANES), lambda j, off: (0, 0)),
            pl.BlockSpec((1, tn), lambda j, off: (0, 0)),
            pl.BlockSpec((1, tn), lambda j, off: (0, 0)),
            pl.BlockSpec((tn, tn), lambda j, off: (0, 0)),
        ],
        out_specs=[
            pl.BlockSpec((m, tn), lambda j, off: (0, j)),
            pl.BlockSpec((m, V7X_LANES), lambda j, off: (0, 0)),
            pl.BlockSpec((tn, d), lambda j, off: (j, 0)),
        ],
        scratch_shapes=[pltpu.VMEM((m, d), BF16)],
    )
    return pl.pallas_call(
        kern,
        grid_spec=grid_spec,
        out_shape=[
            jax.ShapeDtypeStruct((m, nj * tn), F32),
            jax.ShapeDtypeStruct((m, V7X_LANES), F32),
            jax.ShapeDtypeStruct((nj * tn, d), BF16),
        ],
        compiler_params=_cparams(1),
        name="inproj_new",
    )(plan["row_offsets"], x2d, ng, w_t, wfl, bfp, qg, kg, bd)


def _inproj_prompt_kernel(x_ref, ng_ref, w_ref, wfl_ref, bf_ref, qgc_ref, kgc_ref,
                          h_ref, lf_ref, qt_ref, kt_ref, vt_ref, xn_sc, *, steps):
    j = pl.program_id(1)

    @pl.when(j == 0)
    def _():
        _norm_and_logf(x_ref, ng_ref, wfl_ref, bf_ref, xn_sc, lf_ref)

    def head_norm(acc, gain_col_ref):
        tn, tm = acc.shape
        a3 = acc.reshape(tn // HEAD_DIM, HEAD_DIM, tm)
        ms = jnp.mean(a3 * a3, axis=1, keepdims=True)
        g3 = gain_col_ref[...].reshape(tn // HEAD_DIM, HEAD_DIM, V7X_LANES)[:, :, 0:1]
        return (a3 * lax.rsqrt(ms + EPS) * g3).reshape(tn, tm)

    @pl.when(j < steps["q"][0])
    def _():
        h_ref[...] = _dot_nt(xn_sc[...], w_ref[...])

    @pl.when(_step_in(j, steps["q"]))
    def _():
        qt_ref[...] = head_norm(_dot_nt(w_ref[...], xn_sc[...]), qgc_ref)

    @pl.when(_step_in(j, steps["k"]))
    def _():
        kt_ref[...] = head_norm(_dot_nt(w_ref[...], xn_sc[...]), kgc_ref)

    @pl.when(_step_in(j, steps["v"]))
    def _():
        vt_ref[...] = _dot_nt(w_ref[...], xn_sc[...])


def _inproj_prompt(plan, x2d, ng, wbf, wfl, bfp, qgc, kgc, *, tm, seq):
    m, d = x2d.shape
    tn = plan["tn"]
    steps = plan["steps"]
    nj = plan["n_steps"]
    kern = functools.partial(_inproj_prompt_kernel, steps=steps)
    n_h = steps["q"][0]
    nb = seq // tm
    width = (steps["q"][1] - steps["q"][0]) * tn

    def fm_spec(name):
        lo, hi = steps[name]
        return pl.BlockSpec((None, tn, tm), lambda i, j: (i // nb, jnp.clip(j - lo, 0, hi - lo - 1), i % nb))

    return pl.pallas_call(
        kern,
        grid=(m // tm, nj),
        in_specs=[
            pl.BlockSpec((tm, d), lambda i, j: (i, 0)),
            pl.BlockSpec((1, d), lambda i, j: (0, 0)),
            pl.BlockSpec((tn, d), lambda i, j: (j, 0)),
            pl.BlockSpec((V7X_LANES, d), lambda i, j: (0, 0)),
            pl.BlockSpec((1, V7X_LANES), lambda i, j: (0, 0)),
            pl.BlockSpec((tn, V7X_LANES), lambda i, j: (0, 0)),
            pl.BlockSpec((tn, V7X_LANES), lambda i, j: (0, 0)),
        ],
        out_specs=[
            pl.BlockSpec((tm, tn), lambda i, j: (i, jnp.minimum(j, n_h - 1))),
            pl.BlockSpec((tm, V7X_LANES), lambda i, j: (i, 0)),
            fm_spec("q"), fm_spec("k"), fm_spec("v"),
        ],
        out_shape=[
            jax.ShapeDtypeStruct((m, n_h * tn), F32),
            jax.ShapeDtypeStruct((m, V7X_LANES), F32),
        ] + [jax.ShapeDtypeStruct((m // seq, width, seq), F32)] * 3,
        scratch_shapes=[pltpu.VMEM((tm, d), BF16)],
        compiler_params=_cparams(2),
        name="inproj_prompt",
    )(x2d, ng, wbf, wfl, bfp, qgc, kgc)


def _inproj_plan(pw, aw, dm, n_heads, tn):
    src, off = {}, 0
    for name, size in (("u", pw), ("gp", pw), ("q", aw), ("k", aw), ("v", aw), ("ga", aw), ("fl", n_heads),
                       ("ma", dm), ("mb", dm)):
        src[name] = (off, size)
        off += size
    order = ("ma", "mb", "u", "gp", "ga", "q", "k", "v")
    row_offsets, steps, cols = [], {}, {}
    for name in order:
        base, size = src[name]
        steps[name] = (len(row_offsets), len(row_offsets) + size // tn)
        cols[name] = len(row_offsets) * tn
        row_offsets += [base + o for o in range(0, size, tn)]
    assert all(r % ROW_ALIGN == 0 for r in row_offsets)
    return {"tn": tn, "steps": steps, "n_steps": len(row_offsets), "cols": cols, "fl_rows": src["fl"],
            "row_offsets": jnp.asarray([r // ROW_ALIGN for r in row_offsets], jnp.int32)}


def _fcum_kernel(lf_ref, fcol_ref, ft_ref, lft_ref, *, seq, n_heads):
    c = V7X_LANES
    row = lax.broadcasted_iota(jnp.int32, (c, c), 0)
    col = lax.broadcasted_iota(jnp.int32, (c, c), 1)
    tri = jnp.where(col <= row, 1.0, 0.0).astype(BF16)
    carry = jnp.zeros((1, c), F32)
    for ci in range(seq // c):
        x = lf_ref[ci * c:(ci + 1) * c, :]
        hi, mid, lo = _split3(x)
        fc = (_dot(tri, hi) + _dot(tri, mid)) + _dot(tri, lo) + carry
        fcol_ref[ci * c:(ci + 1) * c, :] = fc
        ft_ref[:, ci * c:(ci + 1) * c] = fc.T[:n_heads, :]
        lft_ref[:, ci * c:(ci + 1) * c] = x.T[:n_heads, :]
        carry = fc[c - 1:c, :]


def _fcum(lf2d, *, batch, seq, n_heads):
    kern = functools.partial(_fcum_kernel, seq=seq, n_heads=n_heads)
    return pl.pallas_call(
        kern,
        grid=(batch,),
        in_specs=[pl.BlockSpec((seq, V7X_LANES), lambda b: (b, 0))],
        out_specs=[
            pl.BlockSpec((seq, V7X_LANES), lambda b: (b, 0)),
            pl.BlockSpec((None, n_heads, seq), lambda b: (b, 0, 0)),
            pl.BlockSpec((None, n_heads, seq), lambda b: (b, 0, 0)),
        ],
        out_shape=[
            jax.ShapeDtypeStruct((batch * seq, V7X_LANES), F32),
            jax.ShapeDtypeStruct((batch, n_heads, seq), F32),
            jax.ShapeDtypeStruct((batch, n_heads, seq), F32),
        ],
        compiler_params=_cparams(1),
        name="forget_cumsum",
    )(lf2d)


def _attn_kernel(qt_ref, kt_ref, vt_ref, fcol_ref, ft_ref, o_ref, ka_sc, *, tq, seq, n_pair):
    g = pl.program_id(1)
    i = pl.program_id(2)
    d = HEAD_DIM
    lanes = V7X_LANES
    lane1 = lax.broadcasted_iota(jnp.int32, (1, lanes), 1)

    @pl.when(i == 0)
    def _():
        hrow = lax.broadcasted_iota(jnp.int32, (lanes, lanes), 0)
        lane = lax.broadcasted_iota(jnp.int32, (lanes, lanes), 1)

        def sel(head, lane0, piece):
            return jnp.where((hrow == head) & (lane == lane0 + N_AUG + piece), -1.0, 0.0).astype(BF16)

        ones0 = jnp.where((lane1 >= d) & (lane1 < d + N_AUG), 1.0, 0.0)
        ones1 = jnp.where(lane1 < N_AUG, 1.0, 0.0)
        ck = 256
        for c in range(seq // ck):
            pieces = _split3(fcol_ref[c * ck:(c + 1) * ck, :] * LOG2E)
            for pr in range(n_pair):
                head0 = 2 * (g * n_pair + pr)
                k_rows = kt_ref[pr * lanes:(pr + 1) * lanes, c * ck:(c + 1) * ck].T
                aug0 = ones0
                aug1 = ones1
                for pi, piece in enumerate(pieces):
                    aug0 = aug0 + _dot(piece, sel(head0, d, pi))
                    aug1 = aug1 + _dot(piece, sel(head0 + 1, 0, pi))
                ka_sc[2 * pr, c * ck:(c + 1) * ck, :] = jnp.where(lane1 < d, k_rows, aug0).astype(BF16)
                ka_sc[2 * pr + 1, c * ck:(c + 1) * ck, :] = jnp.where(lane1 >= d, k_rows, aug1).astype(BF16)

    qs = pl.multiple_of(i * tq, tq)
    rowi = lax.broadcasted_iota(jnp.int32, (d, tq), 0)

    def aug_rows(f_row):
        hi, mid, lo = (x.astype(F32) for x in _split3(f_row))
        return jnp.where(rowi == 0, hi, jnp.where(rowi == 1, mid, jnp.where(rowi == 2, lo,
                         jnp.where(rowi < 2 * N_AUG, 1.0, 0.0))))

    qa = []
    for pr in range(n_pair):
        qt = qt_ref[pr * lanes:(pr + 1) * lanes, :] * (ATTN_SCALE * LOG2E)
        fq = ft_ref[pr, :, pl.ds(qs, tq)] * LOG2E
        qa.append(jnp.concatenate([qt[:d, :], aug_rows(fq[0:1, :])], axis=0).astype(BF16))
        qa.append(jnp.concatenate([aug_rows(fq[1:2, :]), qt[d:, :]], axis=0).astype(BF16))
    krow = lax.broadcasted_iota(jnp.int32, (tq, tq), 0)
    qcol = lax.broadcasted_iota(jnp.int32, (tq, tq), 1)
    causal = krow <= qcol

    def step(j, carry, masked):
        ks = pl.multiple_of(j * tq, tq)
        heads = range(2 * n_pair)
        scores = [_dot(ka_sc[hh, pl.ds(ks, tq), :], qa[hh]) for hh in heads]
        stats, probs = [], []
        for hh in heads:
            m, l, _ = carry[hh]
            s = jnp.where(causal, scores[hh], NEG_INF) if masked else scores[hh]
            m_new = jnp.maximum(m, jnp.max(s, axis=0, keepdims=True))
            alpha = jnp.exp2(m - m_new)
            pt = jnp.exp2(s - m_new)
            stats.append((m_new, alpha * l + jnp.sum(pt, axis=0, keepdims=True), alpha))
            probs.append(pt.astype(BF16))
        pvs = [_dot(vt_ref[hh * d:(hh + 1) * d, pl.ds(ks, tq)].astype(BF16), probs[hh]) for hh in heads]
        return tuple((stats[hh][0], stats[hh][1], stats[hh][2] * carry[hh][2] + pvs[hh]) for hh in heads)

    init = tuple((jnp.full((1, tq), NEG_INF, F32), jnp.zeros((1, tq), F32), jnp.zeros((d, tq), F32))
                 for _ in range(2 * n_pair))
    carry = lax.fori_loop(0, i, lambda j, c: step(j, c, False), init)
    final = step(i, carry, True)
    o_ref[...] = jnp.concatenate([acc / l for (_, l, acc) in final], axis=0).T


def _prompt_attention(qt, kt, vt, fcol, ft_pairs, *, batch, seq, n_heads, tq, n_pair):
    n_groups = n_heads // (2 * n_pair)
    qn = seq // tq
    rows = n_pair * V7X_LANES
    kern = functools.partial(_attn_kernel, tq=tq, seq=seq, n_pair=n_pair)
    return pl.pallas_call(
        kern,
        grid=(batch, n_groups, qn),
        in_specs=[
            pl.BlockSpec((None, rows, tq), lambda b, g, i: (b, g, i)),
            pl.BlockSpec((None, rows, seq), lambda b, g, i: (b, g, 0)),
            pl.BlockSpec((None, rows, seq), lambda b, g, i: (b, g, 0)),
            pl.BlockSpec((seq, V7X_LANES), lambda b, g, i: (b, 0)),
            pl.BlockSpec((None, n_pair, 2, seq), lambda b, g, i: (b, g, 0, 0)),
        ],
        out_specs=pl.BlockSpec((tq, rows), lambda b, g, i: (b * qn + i, g)),
        out_shape=jax.ShapeDtypeStruct((batch * seq, n_heads * HEAD_DIM), F32),
        scratch_shapes=[pltpu.VMEM((2 * n_pair, seq, V7X_LANES), BF16)],
        compiler_params=_cparams(3),
        name="prompt_attention",
    )(qt, kt, vt, fcol, ft_pairs)


def _decode_kernel(pt_ref, q_ref, kn_ref, vn_ref, lfn_ref, *rest, n_heads, n_new, n_pg):
    del pt_ref
    kc_refs = rest[0:n_pg]
    vc_refs = rest[n_pg:2 * n_pg]
    lf_refs = rest[2 * n_pg:3 * n_pg]
    o_ref, m_sc, l_sc, acc_sc, carry_sc, qbd_sc, qb_sc = rest[3 * n_pg:]
    j = pl.program_id(1)
    rows = n_heads * n_new
    feat = n_heads * HEAD_DIM

    def tile_rows(x):
        return jnp.concatenate([x] * n_heads, axis=0)

    def rep_rows(x):
        return jnp.concatenate([jnp.broadcast_to(x[h:h + 1, :], (n_new, x.shape[1])) for h in range(n_heads)], axis=0)

    def to_pages(x):
        xp = jnp.concatenate([x, jnp.zeros((PAGE_SIZE - n_new, feat), F32)], axis=0)
        return jnp.concatenate([xp[:, c * PAGE_SIZE:(c + 1) * PAGE_SIZE].T for c in range(feat // PAGE_SIZE)], axis=0)

    def attend(k_pages, v_pages, bias):
        s = _dot(qbd_sc[...], k_pages) + bias
        m_prev = m_sc[...]
        m_new = jnp.maximum(m_prev, jnp.max(s, axis=1, keepdims=True))
        alpha = jnp.exp(m_prev - m_new)
        p = jnp.exp(s - m_new)
        l_sc[...] = alpha * l_sc[...] + jnp.sum(p, axis=1, keepdims=True)
        m_sc[...] = m_new
        alpha_row = jnp.broadcast_to(alpha, (rows, rows)).T[0:1, :]
        pv = _dot(v_pages, p.T.astype(BF16))
        acc_sc[...] = alpha_row * acc_sc[...] + pv

    @pl.when(j == 0)
    def _():
        m_sc[...] = jnp.full(m_sc.shape, NEG_INF, F32)
        l_sc[...] = jnp.zeros(l_sc.shape, F32)
        acc_sc[...] = jnp.zeros(acc_sc.shape, F32)
        carry_sc[...] = jnp.zeros(carry_sc.shape, F32)
        row_head = lax.broadcasted_iota(jnp.int32, (rows, feat), 0) // n_new
        col_head = lax.broadcasted_iota(jnp.int32, (rows, feat), 1) // HEAD_DIM
        qbd_sc[...] = jnp.where(row_head == col_head, tile_rows(q_ref[...] * ATTN_SCALE), 0.0).astype(BF16)
        f_new = _scan_rows(lfn_ref[...])
        rh = lax.broadcasted_iota(jnp.int32, (rows, V7X_LANES), 0) // n_new
        ln = lax.broadcasted_iota(jnp.int32, (rows, V7X_LANES), 1)
        qb = jnp.sum(jnp.where(rh == ln, tile_rows(f_new), 0.0), axis=1, keepdims=True)
        qb_sc[...] = qb
        f_pad = jnp.concatenate([f_new, jnp.zeros((PAGE_SIZE - n_new, V7X_LANES), F32)], axis=0)
        f_new_t = f_pad.T[:n_heads, :]
        tok = lax.broadcasted_iota(jnp.int32, (rows, PAGE_SIZE), 0) % n_new
        pos = lax.broadcasted_iota(jnp.int32, (rows, PAGE_SIZE), 1)
        bias = (qb - rep_rows(f_new_t)) + jnp.where(pos <= tok, 0.0, NEG_INF)
        attend(to_pages(kn_ref[...]).astype(BF16), to_pages(vn_ref[...]).astype(BF16), bias)

    carry = carry_sc[...]
    biases = []
    for g in range(n_pg):
        lf = lf_refs[g][...]
        incl = _scan_lanes(lf, reverse=True)
        biases.append(rep_rows((incl - lf) + carry))
        carry = carry + incl[:, 0:1]
    carry_sc[...] = carry
    k_pages = jnp.concatenate([r[...].reshape(feat, PAGE_SIZE).astype(BF16) for r in kc_refs], axis=1)
    v_pages = jnp.concatenate([r[...].reshape(feat, PAGE_SIZE).astype(BF16) for r in vc_refs], axis=1)
    attend(k_pages, v_pages, qb_sc[...] + jnp.concatenate(biases, axis=1))

    @pl.when(j == pl.num_programs(1) - 1)
    def _():
        outs = []
        for h in range(n_heads):
            blk = acc_sc[h * HEAD_DIM:(h + 1) * HEAD_DIM, :].T
            outs.append(blk[h * n_new:(h + 1) * n_new, :])
        o_ref[...] = jnp.concatenate(outs, axis=0) / l_sc[...]


def _decode_attention(page_table, hs, lf_new, cache_kt, cache_vt, cache_lft, *, n_heads, n_new, n_pg, cols):
    dec_b, n_pages = page_table.shape
    rows = n_heads * n_new
    feat = n_heads * HEAD_DIM
    last = n_pages - 1
    kern = functools.partial(_decode_kernel, n_heads=n_heads, n_new=n_new, n_pg=n_pg)

    def page_spec(g, tail):
        return pl.BlockSpec((None, None, n_heads) + tail,
                            lambda b, j, pt: (0, pt[b, last - (j * n_pg + g)]) + (0,) * (1 + len(tail)))

    kv_tail = (HEAD_DIM, PAGE_SIZE)
    in_specs = [
        pl.BlockSpec((n_new, feat), lambda b, j, pt: (b, cols["q"] // feat)),
        pl.BlockSpec((n_new, feat), lambda b, j, pt: (b, cols["k"] // feat)),
        pl.BlockSpec((n_new, feat), lambda b, j, pt: (b, cols["v"] // feat)),
        pl.BlockSpec((n_new, V7X_LANES), lambda b, j, pt: (b, 0)),
    ]
    in_specs += [page_spec(g, kv_tail) for g in range(n_pg)]
    in_specs += [page_spec(g, kv_tail) for g in range(n_pg)]
    in_specs += [page_spec(g, (PAGE_SIZE,)) for g in range(n_pg)]
    grid_spec = pltpu.PrefetchScalarGridSpec(
        num_scalar_prefetch=1,
        grid=(dec_b, n_pages // n_pg),
        in_specs=in_specs,
        out_specs=pl.BlockSpec((None, rows, HEAD_DIM), lambda b, j, pt: (b, 0, 0)),
        scratch_shapes=[
            pltpu.VMEM((rows, 1), F32),
            pltpu.VMEM((rows, 1), F32),
            pltpu.VMEM((feat, rows), F32),
            pltpu.VMEM((n_heads, PAGE_SIZE), F32),
            pltpu.VMEM((rows, feat), BF16),
            pltpu.VMEM((rows, 1), F32),
        ],
    )
    args = [page_table, hs, hs, hs, lf_new] + [cache_kt] * n_pg + [cache_vt] * n_pg + [cache_lft] * n_pg
    return pl.pallas_call(
        kern,
        grid_spec=grid_spec,
        out_shape=jax.ShapeDtypeStruct((dec_b, rows, HEAD_DIM), F32),
        compiler_params=_cparams(2),
        name="decode_attention",
    )(*args)


def _pool_diff(z, u, pos, group_w):
    s = z
    k = 1
    while k < group_w:
        s = s + pltpu.roll(s, k, 0)
        k *= 2
    cnt = jnp.minimum(pos + 1, group_w).astype(F32)
    return s[HALO:, :] / cnt - u


def _mix_tail(d_groups, gp, ga, att, ma, mb, x, wpool_ref, ps_ref, wup_ref, wua_ref, wout_ref):
    mixed = [_dot(d.astype(BF16), wpool_ref[gi]) for gi, d in enumerate(d_groups)]
    mixed = jnp.concatenate(mixed, axis=1)
    branch_a = (mixed * ps_ref[...]) * _silu(gp)
    branch_b = att * _silu(ga)
    up_a = _dot(branch_a.astype(BF16), wup_ref[...])
    up_b = _dot(branch_b.astype(BF16), wua_ref[...])
    merged = _sigmoid(ma) * up_a + _sigmoid(mb) * up_b
    return x + _dot(merged.astype(BF16), wout_ref[...])


def _out_prompt_kernel(u_ref, halo_ref, gp_ref, ga_ref, ma_ref, mb_ref, att_ref, x_ref,
                       wpool_ref, ps_ref, wup_ref, wua_ref, wout_ref, y_ref, *, tm, seq):
    i = pl.program_id(0)
    pos0 = (i * tm) % seq
    u = u_ref[...]
    halo = jnp.where(pos0 == 0, 0.0, halo_ref[...])
    z = jnp.concatenate([halo, u], axis=0)
    pos = pos0 + lax.broadcasted_iota(jnp.int32, (tm, 1), 0)
    gw = u.shape[1] // len(POOL_WINDOWS)
    d_groups = [_pool_diff(z[:, gi * gw:(gi + 1) * gw], u[:, gi * gw:(gi + 1) * gw], pos, w)
                for gi, w in enumerate(POOL_WINDOWS)]
    y_ref[...] = _mix_tail(d_groups, gp_ref[...], ga_ref[...], att_ref[...], ma_ref[...], mb_ref[...], x_ref[...],
                           wpool_ref, ps_ref, wup_ref, wua_ref, wout_ref)


def _out_sample_kernel(z_ref, gp_ref, ga_ref, ma_ref, mb_ref, att_ref, x_ref,
                       wpool_ref, ps_ref, wup_ref, wua_ref, wout_ref, y_ref, *, dec_b, n_new, pos0):
    gw = z_ref.shape[2] // len(POOL_WINDOWS)
    pos = pos0 + lax.broadcasted_iota(jnp.int32, (n_new, 1), 0)
    per_seq = []
    for b in range(dec_b):
        z = z_ref[b]
        u = z[HALO:, :]
        per_seq.append([_pool_diff(z[:, gi * gw:(gi + 1) * gw], u[:, gi * gw:(gi + 1) * gw], pos, w)
                        for gi, w in enumerate(POOL_WINDOWS)])
    d_groups = [jnp.concatenate([per_seq[b][gi] for b in range(dec_b)], axis=0) for gi in range(len(POOL_WINDOWS))]
    y_ref[...] = _mix_tail(d_groups, gp_ref[...], ga_ref[...], att_ref[...], ma_ref[...], mb_ref[...], x_ref[...],
                           wpool_ref, ps_ref, wup_ref, wua_ref, wout_ref)


def _weight_specs(pw, gw, dm):
    one = pl.Buffered(1)
    return [
        pl.BlockSpec((len(POOL_WINDOWS), gw, gw), lambda i: (0, 0, 0), pipeline_mode=one),
        pl.BlockSpec((1, pw), lambda i: (0, 0), pipeline_mode=one),
        pl.BlockSpec((pw, dm), lambda i: (0, 0), pipeline_mode=one),
        pl.BlockSpec((pw, dm), lambda i: (0, 0), pipeline_mode=one),
        pl.BlockSpec((dm, dm), lambda i: (0, 0), pipeline_mode=one),
    ]


def _out_prompt(h, att, x2d, wpool, ps, wup, wua, wout, *, seq, tm, cols):
    m, dm = x2d.shape
    pw = att.shape[1]
    gw = pw // len(POOL_WINDOWS)
    hb = tm // HALO
    kern = functools.partial(_out_prompt_kernel, tm=tm, seq=seq)
    return pl.pallas_call(
        kern,
        grid=(m // tm,),
        in_specs=[
            pl.BlockSpec((tm, pw), lambda i: (i, cols["u"] // pw)),
            pl.BlockSpec((HALO, pw), lambda i: (jnp.maximum(i * hb - 1, 0), cols["u"] // pw)),
            pl.BlockSpec((tm, pw), lambda i: (i, cols["gp"] // pw)),
            pl.BlockSpec((tm, pw), lambda i: (i, cols["ga"] // pw)),
            pl.BlockSpec((tm, dm), lambda i: (i, cols["ma"] // dm)),
            pl.BlockSpec((tm, dm), lambda i: (i, cols["mb"] // dm)),
            pl.BlockSpec((tm, pw), lambda i: (i, 0)),
            pl.BlockSpec((tm, dm), lambda i: (i, 0)),
        ] + _weight_specs(pw, gw, dm),
        out_specs=pl.BlockSpec((tm, dm), lambda i: (i, 0)),
        out_shape=jax.ShapeDtypeStruct((m, dm), F32),
        compiler_params=_cparams(1),
        name="out_prompt",
    )(h, h, h, h, h, h, att, x2d, wpool, ps, wup, wua, wout)


def _out_sample(z, h, att, x2d, wpool, ps, wup, wua, wout, *, dec_b, n_new, pos0, cols):
    m, dm = x2d.shape
    pw = att.shape[1]
    gw = pw // len(POOL_WINDOWS)
    kern = functools.partial(_out_sample_kernel, dec_b=dec_b, n_new=n_new, pos0=pos0)
    return pl.pallas_call(
        kern,
        grid=(1,),
        in_specs=[
            pl.BlockSpec((dec_b, HALO + n_new, pw), lambda i: (0, 0, 0)),
            pl.BlockSpec((m, pw), lambda i: (0, cols["gp"] // pw)),
            pl.BlockSpec((m, pw), lambda i: (0, cols["ga"] // pw)),
            pl.BlockSpec((m, dm), lambda i: (0, cols["ma"] // dm)),
            pl.BlockSpec((m, dm), lambda i: (0, cols["mb"] // dm)),
            pl.BlockSpec((m, pw), lambda i: (0, 0)),
            pl.BlockSpec((m, dm), lambda i: (0, 0)),
        ] + _weight_specs(pw, gw, dm),
        out_specs=pl.BlockSpec((m, dm), lambda i: (0, 0)),
        out_shape=jax.ShapeDtypeStruct((m, dm), F32),
        compiler_params=_cparams(1),
        name="out_sample",
    )(z, h, h, h, h, att, x2d, wpool, ps, wup, wua, wout)


def kernel(x_prompt, x_sample, cache_k, cache_v, cache_logf, state_pool, page_table, norm_gain, w_in, b_f,
           q_norm_gain, k_norm_gain, w_pool_map, pool_scale, w_up_pool, w_up_attn, w_out):
    batch, seq, dm = x_prompt.shape
    dec_b, n_new, _ = x_sample.shape
    assert w_in.shape[0] == 1
    n_heads = b_f.shape[1]
    aw = n_heads * HEAD_DIM
    pw = w_up_pool.shape[1]
    n_pages = page_table.shape[1]
    past_len = n_pages * PAGE_SIZE
    assert pw == aw and dm == 2 * pw and n_pages % PAGES_PER_STEP == 0

    tn = 512
    plan = _inproj_plan(pw, aw, dm, n_heads, tn)
    cols = plan["cols"]
    w_t = w_in[0].T
    fl0, fl_n = plan["fl_rows"]
    wfl = jnp.pad(w_t[fl0:fl0 + fl_n], ((0, V7X_LANES - n_heads), (0, 0))).astype(BF16)
    bfp = jnp.pad(b_f, ((0, 0), (0, V7X_LANES - n_heads)))
    reps = tn // HEAD_DIM
    qg = jnp.tile(q_norm_gain[0], reps)[None, :]
    kg = jnp.tile(k_norm_gain[0], reps)[None, :]
    qgc = jnp.broadcast_to(jnp.tile(q_norm_gain[0], reps)[:, None], (tn, V7X_LANES))
    kgc = jnp.broadcast_to(jnp.tile(k_norm_gain[0], reps)[:, None], (tn, V7X_LANES))
    seg = np.arange(tn) // HEAD_DIM
    bd = jnp.asarray((seg[:, None] == seg[None, :]).astype(np.float32)).astype(BF16)
    wpool = w_pool_map[0].astype(BF16)
    wup = w_up_pool[0].astype(BF16)
    wua = w_up_attn[0].astype(BF16)
    wout = w_out[0].astype(BF16)

    ms = dec_b * n_new
    xs2 = x_sample.reshape(ms, dm)
    hs, lfs, wbf = _inproj_new(plan, xs2, norm_gain, w_t, wfl, bfp, qg, kg, bd)

    xp2 = x_prompt.reshape(batch * seq, dm)
    hp, lfp, qt_p, kt_p, vt_p = _inproj_prompt(plan, xp2, norm_gain, wbf, wfl, bfp, qgc, kgc, tm=1024, seq=seq)
    fcol, ft, lft = _fcum(lfp, batch=batch, seq=seq, n_heads=n_heads)
    ft_pairs = ft.reshape(batch, n_heads // 2, 2, seq)
    att_p = _prompt_attention(qt_p, kt_p, vt_p, fcol, ft_pairs, batch=batch, seq=seq, n_heads=n_heads, tq=256,
                              n_pair=ATTN_PAIRS_PER_STEP)
    yp = _out_prompt(hp, att_p, xp2, wpool, pool_scale, wup, wua, wout, seq=seq, tm=256, cols=cols)

    k_s = hs[:, cols["k"]:cols["k"] + aw].reshape(dec_b, n_new, n_heads, HEAD_DIM)
    v_s = hs[:, cols["v"]:cols["v"] + aw].reshape(dec_b, n_new, n_heads, HEAD_DIM)
    u_s = hs[:, cols["u"]:cols["u"] + pw].reshape(dec_b, n_new, pw)
    logf_s = lfs[:, :n_heads].reshape(dec_b, n_new, n_heads)
    cache_kt = cache_k.transpose(0, 1, 3, 4, 2)
    cache_vt = cache_v.transpose(0, 1, 3, 4, 2)
    cache_lft = cache_logf.transpose(0, 1, 3, 2)
    att_rows = _decode_attention(page_table, hs, lfs, cache_kt, cache_vt, cache_lft,
                                 n_heads=n_heads, n_new=n_new, n_pg=PAGES_PER_STEP, cols=cols)
    att_s = att_rows.reshape(dec_b, n_heads, n_new, HEAD_DIM).transpose(0, 2, 1, 3).reshape(ms, aw)
    z_s = jnp.concatenate([jnp.zeros((dec_b, HALO - POOL_BUF, pw), F32), state_pool[0], u_s], axis=1)
    ys = _out_sample(z_s, hs, att_s, xs2, wpool, pool_scale, wup, wua, wout,
                     dec_b=dec_b, n_new=n_new, pos0=past_len, cols=cols)

    k_p = kt_p.reshape(1, batch, n_heads, HEAD_DIM, seq).transpose(0, 1, 4, 2, 3)
    v_p = vt_p.reshape(1, batch, n_heads, HEAD_DIM, seq).transpose(0, 1, 4, 2, 3)
    logf_p = lft.transpose(0, 2, 1)[None]
    pool_p = hp.reshape(batch, seq, hp.shape[1])[:, seq - POOL_BUF:, cols["u"]:cols["u"] + pw][None]
    pool_s = z_s[:, HALO + n_new - POOL_BUF:, :][None]
    return (yp.reshape(batch, seq, dm), ys.reshape(dec_b, n_new, dm), k_p, v_p, logf_p, pool_p,
            k_s[None], v_s[None], logf_s[None], pool_s)
```

```python
import functools

import jax
import jax.numpy as jnp
import numpy as np
from jax import lax
from jax.experimental import pallas as pl
from jax.experimental.pallas import tpu as pltpu

F32 = jnp.float32
BF16 = jnp.bfloat16

HEAD_DIM = 64
POOL_WINDOWS = (2, 4, 8, 16)
POOL_BUF = 15
PAGE_SIZE = 128
EPS = 1e-6
NEG_INF = -1e30
ATTN_SCALE = HEAD_DIM ** -0.5
LOG2E = 1.4426950408889634

V7X_LANES = 128
V7X_SUBLANES = 8
V7X_VMEM_LIMIT_BYTES = 56 * 1024 * 1024

ROW_ALIGN = 16
HALO = 16
PAGES_PER_STEP = 16
DECODE_PAGES_FUSED = 8
N_AUG = 3
ATTN_PAIRS_PER_STEP = 8


def _cparams(n_grid_axes):
    return pltpu.CompilerParams(
        dimension_semantics=("arbitrary",) * n_grid_axes,
        vmem_limit_bytes=V7X_VMEM_LIMIT_BYTES,
    )


def _split3(x):
    hi = x.astype(BF16)
    r1 = x - hi.astype(F32)
    mid = r1.astype(BF16)
    lo = (r1 - mid.astype(F32)).astype(BF16)
    return hi, mid, lo


def _dot(a, b):
    return jnp.dot(a, b, preferred_element_type=F32)


def _dot_nt(a, b):
    return lax.dot_general(a, b, (((1,), (1,)), ((), ())), preferred_element_type=F32)


def _sigmoid(x):
    return 1.0 / (1.0 + jnp.exp(-x))


def _silu(x):
    return x * _sigmoid(x)


def _scan_lanes(x, *, reverse):
    n = x.shape[-1]
    ax = x.ndim - 1
    lane = lax.broadcasted_iota(jnp.int32, x.shape, ax)
    k = 1
    while k < n:
        if reverse:
            shifted = pltpu.roll(x, n - k, ax)
            x = x + jnp.where(lane < n - k, shifted, 0.0)
        else:
            shifted = pltpu.roll(x, k, ax)
            x = x + jnp.where(lane >= k, shifted, 0.0)
        k *= 2
    return x


def _scan_rows(x):
    n = x.shape[0]
    row = lax.broadcasted_iota(jnp.int32, x.shape, 0)
    k = 1
    while k < n:
        x = x + jnp.where(row >= k, pltpu.roll(x, k, 0), 0.0)
        k *= 2
    return x


def _norm_and_logf(x_ref, ng_ref, wfl_ref, bf_ref, xn_sc, lf_ref):
    x = x_ref[...]
    ms = jnp.mean(x * x, axis=-1, keepdims=True)
    xn = x * lax.rsqrt(ms + EPS) * ng_ref[...]
    xn_sc[...] = xn.astype(BF16)
    z = _dot_nt(xn_sc[...], wfl_ref[...]) + bf_ref[...]
    lf_ref[...] = jnp.minimum(z, 0.0) - jnp.log1p(jnp.exp(-jnp.abs(z)))


def _step_in(j, rng):
    return jnp.logical_and(j >= rng[0], j < rng[1])


def _inproj_new_kernel(off_ref, x_ref, ng_ref, w_ref, wfl_ref, bf_ref, qg_ref, kg_ref, bd_ref,
                       h_ref, lf_ref, wbf_ref, xn_sc, *, steps):
    del off_ref
    j = pl.program_id(0)

    @pl.when(j == 0)
    def _():
        _norm_and_logf(x_ref, ng_ref, wfl_ref, bf_ref, xn_sc, lf_ref)

    w = w_ref[...].astype(BF16)
    wbf_ref[...] = w
    acc = _dot_nt(xn_sc[...], w)

    def head_norm(gain):
        sq = acc * acc
        hi = sq.astype(BF16)
        lo = (sq - hi.astype(F32)).astype(BF16)
        ss = _dot(hi, bd_ref[...]) + _dot(lo, bd_ref[...])
        return acc * lax.rsqrt(ss * (1.0 / HEAD_DIM) + EPS) * gain

    is_q = _step_in(j, steps["q"])
    is_k = _step_in(j, steps["k"])

    @pl.when(is_q)
    def _():
        h_ref[...] = head_norm(qg_ref[...])

    @pl.when(is_k)
    def _():
        h_ref[...] = head_norm(kg_ref[...])

    @pl.when(jnp.logical_not(is_q | is_k))
    def _():
        h_ref[...] = acc


def _inproj_new(plan, x2d, ng, w_t, wfl, bfp, qg, kg, bd):
    m, d = x2d.shape
    tn = plan["tn"]
    nj = plan["n_steps"]
    kern = functools.partial(_inproj_new_kernel, steps=plan["steps"])
    grid_spec = pltpu.PrefetchScalarGridSpec(
        num_scalar_prefetch=1,
        grid=(nj,),
        in_specs=[
            pl.BlockSpec((m, d), lambda j, off: (0, 0)),
            pl.BlockSpec((1, d), lambda j, off: (0, 0)),
            pl.BlockSpec((pl.Element(tn), pl.Element(d)), lambda j, off: (off[j] * ROW_ALIGN, 0)),
            pl.BlockSpec((V7X_LANES, d), lambda j, off: (0, 0)),
            pl.BlockSpec((1, V7X_LANES), lambda j, off: (0, 0)),
            pl.BlockSpec((1, tn), lambda j, off: (0, 0)),
            pl.BlockSpec((1, tn), lambda j, off: (0, 0)),
            pl.BlockSpec((tn, tn), lambda j, off: (0, 0)),
        ],
        out_specs=[
            pl.BlockSpec((m, tn), lambda j, off: (0, j)),
            pl.BlockSpec((m, V7X_LANES), lambda j, off: (0, 0)),
            pl.BlockSpec((tn, d), lambda j, off: (j, 0)),
        ],
        scratch_shapes=[pltpu.VMEM((m, d), BF16)],
    )
    return pl.pallas_call(
        kern,
        grid_spec=grid_spec,
        out_shape=[
            jax.ShapeDtypeStruct((m, nj * tn), F32),
            jax.ShapeDtypeStruct((m, V7X_LANES), F32),
            jax.ShapeDtypeStruct((nj * tn, d), BF16),
        ],
        compiler_params=_cparams(1),
        name="inproj_new",
    )(plan["row_offsets"], x2d, ng, w_t, wfl, bfp, qg, kg, bd)


def _prenorm_kernel(x_ref, ng_ref, wfl_ref, bf_ref, xn_ref, lf_ref):
    _norm_and_logf(x_ref, ng_ref, wfl_ref, bf_ref, xn_ref, lf_ref)


def _prenorm(x2d, ng, wfl, bfp, *, tm):
    m, d = x2d.shape
    return pl.pallas_call(
        _prenorm_kernel,
        grid=(m // tm,),
        in_specs=[
            pl.BlockSpec((tm, d), lambda i: (i, 0)),
            pl.BlockSpec((1, d), lambda i: (0, 0)),
            pl.BlockSpec((V7X_LANES, d), lambda i: (0, 0)),
            pl.BlockSpec((1, V7X_LANES), lambda i: (0, 0)),
        ],
        out_specs=[
            pl.BlockSpec((tm, d), lambda i: (i, 0)),
            pl.BlockSpec((tm, V7X_LANES), lambda i: (i, 0)),
        ],
        out_shape=[
            jax.ShapeDtypeStruct((m, d), BF16),
            jax.ShapeDtypeStruct((m, V7X_LANES), F32),
        ],
        compiler_params=_cparams(1),
        name="prenorm",
    )(x2d, ng, wfl, bfp)


def _inproj_fm_kernel(xn_ref, w_ref, qgc_ref, kgc_ref, qt_ref, kt_ref, vt_ref, *, steps):
    j = pl.program_id(1) + steps["q"][0]

    def head_norm(acc, gain_col_ref):
        tn, tm = acc.shape
        a3 = acc.reshape(tn // HEAD_DIM, HEAD_DIM, tm)
        ms = jnp.mean(a3 * a3, axis=1, keepdims=True)
        g3 = gain_col_ref[...].reshape(tn // HEAD_DIM, HEAD_DIM, V7X_LANES)[:, :, 0:1]
        return (a3 * lax.rsqrt(ms + EPS) * g3).reshape(tn, tm)

    @pl.when(_step_in(j, steps["q"]))
    def _():
        qt_ref[...] = head_norm(_dot_nt(w_ref[...], xn_ref[...]), qgc_ref)

    @pl.when(_step_in(j, steps["k"]))
    def _():
        kt_ref[...] = head_norm(_dot_nt(w_ref[...], xn_ref[...]), kgc_ref)

    @pl.when(_step_in(j, steps["v"]))
    def _():
        vt_ref[...] = _dot_nt(w_ref[...], xn_ref[...])


def _inproj_fm(plan, xn, wbf, qgc, kgc, *, tm, seq):
    m, d = xn.shape
    tn = plan["tn"]
    steps = plan["steps"]
    j0 = steps["q"][0]
    assert steps["v"][1] == plan["n_steps"]
    kern = functools.partial(_inproj_fm_kernel, steps=steps)
    nb = seq // tm
    width = (steps["q"][1] - steps["q"][0]) * tn

    def fm_spec(name):
        lo, hi = steps[name]
        return pl.BlockSpec((None, tn, tm), lambda i, j: (i // nb, jnp.clip(j + j0 - lo, 0, hi - lo - 1), i % nb))

    return pl.pallas_call(
        kern,
        grid=(m // tm, plan["n_steps"] - j0),
        in_specs=[
            pl.BlockSpec((tm, d), lambda i, j: (i, 0)),
            pl.BlockSpec((tn, d), lambda i, j: (j + j0, 0)),
            pl.BlockSpec((tn, V7X_LANES), lambda i, j: (0, 0)),
            pl.BlockSpec((tn, V7X_LANES), lambda i, j: (0, 0)),
        ],
        out_specs=[fm_spec("q"), fm_spec("k"), fm_spec("v")],
        out_shape=[jax.ShapeDtypeStruct((m // seq, width, seq), F32)] * 3,
        compiler_params=_cparams(2),
        name="inproj_fm",
    )(xn, wbf, qgc, kgc)


def _inproj_plan(pw, aw, dm, n_heads, tn):
    src, off = {}, 0
    for name, size in (("u", pw), ("gp", pw), ("q", aw), ("k", aw), ("v", aw), ("ga", aw), ("fl", n_heads),
                       ("ma", dm), ("mb", dm)):
        src[name] = (off, size)
        off += size
    order = ("ma", "mb", "u", "gp", "ga", "q", "k", "v")
    row_offsets, steps, cols = [], {}, {}
    for name in order:
        base, size = src[name]
        steps[name] = (len(row_offsets), len(row_offsets) + size // tn)
        cols[name] = len(row_offsets) * tn
        row_offsets += [base + o for o in range(0, size, tn)]
    assert all(r % ROW_ALIGN == 0 for r in row_offsets)
    return {"tn": tn, "steps": steps, "n_steps": len(row_offsets), "cols": cols, "fl_rows": src["fl"],
            "row_offsets": jnp.asarray([r // ROW_ALIGN for r in row_offsets], jnp.int32)}


def _fcum_kernel(lf_ref, fcol_ref, ft_ref, lft_ref, *, seq, n_heads):
    c = V7X_LANES
    row = lax.broadcasted_iota(jnp.int32, (c, c), 0)
    col = lax.broadcasted_iota(jnp.int32, (c, c), 1)
    tri = jnp.where(col <= row, 1.0, 0.0).astype(BF16)
    carry = jnp.zeros((1, c), F32)
    for ci in range(seq // c):
        x = lf_ref[ci * c:(ci + 1) * c, :]
        hi, mid, lo = _split3(x)
        fc = (_dot(tri, hi) + _dot(tri, mid)) + _dot(tri, lo) + carry
        fcol_ref[ci * c:(ci + 1) * c, :] = fc
        ft_ref[:, ci * c:(ci + 1) * c] = fc.T[:n_heads, :]
        lft_ref[:, ci * c:(ci + 1) * c] = x.T[:n_heads, :]
        carry = fc[c - 1:c, :]


def _fcum(lf2d, *, batch, seq, n_heads):
    kern = functools.partial(_fcum_kernel, seq=seq, n_heads=n_heads)
    return pl.pallas_call(
        kern,
        grid=(batch,),
        in_specs=[pl.BlockSpec((seq, V7X_LANES), lambda b: (b, 0))],
        out_specs=[
            pl.BlockSpec((seq, V7X_LANES), lambda b: (b, 0)),
            pl.BlockSpec((None, n_heads, seq), lambda b: (b, 0, 0)),
            pl.BlockSpec((None, n_heads, seq), lambda b: (b, 0, 0)),
        ],
        out_shape=[
            jax.ShapeDtypeStruct((batch * seq, V7X_LANES), F32),
            jax.ShapeDtypeStruct((batch, n_heads, seq), F32),
            jax.ShapeDtypeStruct((batch, n_heads, seq), F32),
        ],
        compiler_params=_cparams(1),
        name="forget_cumsum",
    )(lf2d)


def _attn_kernel(qt_ref, kt_ref, vt_ref, fcol_ref, ft_ref, o_ref, ka_sc, *, tq, seq, n_pair):
    g = pl.program_id(1)
    i = pl.program_id(2)
    d = HEAD_DIM
    lanes = V7X_LANES
    lane1 = lax.broadcasted_iota(jnp.int32, (1, lanes), 1)

    @pl.when(i == 0)
    def _():
        hrow = lax.broadcasted_iota(jnp.int32, (lanes, lanes), 0)
        lane = lax.broadcasted_iota(jnp.int32, (lanes, lanes), 1)

        def sel(head, lane0, piece):
            return jnp.where((hrow == head) & (lane == lane0 + N_AUG + piece), -1.0, 0.0).astype(BF16)

        ones0 = jnp.where((lane1 >= d) & (lane1 < d + N_AUG), 1.0, 0.0)
        ones1 = jnp.where(lane1 < N_AUG, 1.0, 0.0)
        ck = 256
        for c in range(seq // ck):
            pieces = _split3(fcol_ref[c * ck:(c + 1) * ck, :] * LOG2E)
            for pr in range(n_pair):
                head0 = 2 * (g * n_pair + pr)
                k_rows = kt_ref[pr * lanes:(pr + 1) * lanes, c * ck:(c + 1) * ck].T
                aug0 = ones0
                aug1 = ones1
                for pi, piece in enumerate(pieces):
                    aug0 = aug0 + _dot(piece, sel(head0, d, pi))
                    aug1 = aug1 + _dot(piece, sel(head0 + 1, 0, pi))
                ka_sc[2 * pr, c * ck:(c + 1) * ck, :] = jnp.where(lane1 < d, k_rows, aug0).astype(BF16)
                ka_sc[2 * pr + 1, c * ck:(c + 1) * ck, :] = jnp.where(lane1 >= d, k_rows, aug1).astype(BF16)

    qs = pl.multiple_of(i * tq, tq)
    rowi = lax.broadcasted_iota(jnp.int32, (d, tq), 0)

    def aug_rows(f_row):
        hi, mid, lo = (x.astype(F32) for x in _split3(f_row))
        return jnp.where(rowi == 0, hi, jnp.where(rowi == 1, mid, jnp.where(rowi == 2, lo,
                         jnp.where(rowi < 2 * N_AUG, 1.0, 0.0))))

    qa = []
    for pr in range(n_pair):
        qt = qt_ref[pr * lanes:(pr + 1) * lanes, :] * (ATTN_SCALE * LOG2E)
        fq = ft_ref[pr, :, pl.ds(qs, tq)] * LOG2E
        qa.append(jnp.concatenate([qt[:d, :], aug_rows(fq[0:1, :])], axis=0).astype(BF16))
        qa.append(jnp.concatenate([aug_rows(fq[1:2, :]), qt[d:, :]], axis=0).astype(BF16))
    krow = lax.broadcasted_iota(jnp.int32, (tq, tq), 0)
    qcol = lax.broadcasted_iota(jnp.int32, (tq, tq), 1)
    causal = krow <= qcol

    def step(j, carry, masked):
        ks = pl.multiple_of(j * tq, tq)
        heads = range(2 * n_pair)
        scores = [_dot(ka_sc[hh, pl.ds(ks, tq), :], qa[hh]) for hh in heads]
        stats, probs = [], []
        for hh in heads:
            m, l, _ = carry[hh]
            s = jnp.where(causal, scores[hh], NEG_INF) if masked else scores[hh]
            m_new = jnp.maximum(m, jnp.max(s, axis=0, keepdims=True))
            alpha = jnp.exp2(m - m_new)
            pt = jnp.exp2(s - m_new)
            stats.append((m_new, alpha * l + jnp.sum(pt, axis=0, keepdims=True), alpha))
            probs.append(pt.astype(BF16))
        pvs = [_dot(vt_ref[hh * d:(hh + 1) * d, pl.ds(ks, tq)].astype(BF16), probs[hh]) for hh in heads]
        return tuple((stats[hh][0], stats[hh][1], stats[hh][2] * carry[hh][2] + pvs[hh]) for hh in heads)

    init = tuple((jnp.full((1, tq), NEG_INF, F32), jnp.zeros((1, tq), F32), jnp.zeros((d, tq), F32))
                 for _ in range(2 * n_pair))
    carry = lax.fori_loop(0, i, lambda j, c: step(j, c, False), init)
    final = step(i, carry, True)
    o_ref[...] = jnp.concatenate([acc / l for (_, l, acc) in final], axis=0).T


def _prompt_attention(qt, kt, vt, fcol, ft_pairs, *, batch, seq, n_heads, tq, n_pair):
    n_groups = n_heads // (2 * n_pair)
    qn = seq // tq
    rows = n_pair * V7X_LANES
    kern = functools.partial(_attn_kernel, tq=tq, seq=seq, n_pair=n_pair)
    return pl.pallas_call(
        kern,
        grid=(batch, n_groups, qn),
        in_specs=[
            pl.BlockSpec((None, rows, tq), lambda b, g, i: (b, g, i)),
            pl.BlockSpec((None, rows, seq), lambda b, g, i: (b, g, 0)),
            pl.BlockSpec((None, rows, seq), lambda b, g, i: (b, g, 0)),
            pl.BlockSpec((seq, V7X_LANES), lambda b, g, i: (b, 0)),
            pl.BlockSpec((None, n_pair, 2, seq), lambda b, g, i: (b, g, 0, 0)),
        ],
        out_specs=pl.BlockSpec((tq, rows), lambda b, g, i: (b * qn + i, g)),
        out_shape=jax.ShapeDtypeStruct((batch * seq, n_heads * HEAD_DIM), F32),
        scratch_shapes=[pltpu.VMEM((2 * n_pair, seq, V7X_LANES), BF16)],
        compiler_params=_cparams(3),
        name="prompt_attention",
    )(qt, kt, vt, fcol, ft_pairs)


def _decode_step(cj, n_chunks, q_ref, kn_ref, vn_ref, lfn_ref, kc_refs, vc_refs, lf_refs, o_ref, scratch,
                 *, n_heads, n_new, between=None):
    m_sc, l_sc, acc_sc, carry_sc, qbd_sc, qb_sc = scratch
    rows = n_heads * n_new
    feat = n_heads * HEAD_DIM

    def tile_rows(x):
        return jnp.concatenate([x] * n_heads, axis=0)

    def rep_rows(x):
        return jnp.concatenate([jnp.broadcast_to(x[h:h + 1, :], (n_new, x.shape[1])) for h in range(n_heads)], axis=0)

    def to_pages(x):
        xp = jnp.concatenate([x, jnp.zeros((PAGE_SIZE - n_new, feat), F32)], axis=0)
        return jnp.concatenate([xp[:, c * PAGE_SIZE:(c + 1) * PAGE_SIZE].T for c in range(feat // PAGE_SIZE)], axis=0)

    def score_phase(k_pages, bias):
        s = _dot(qbd_sc[...], k_pages) + bias
        m_prev = m_sc[...]
        m_new = jnp.maximum(m_prev, jnp.max(s, axis=1, keepdims=True))
        alpha = jnp.exp(m_prev - m_new)
        p = jnp.exp(s - m_new)
        l_sc[...] = alpha * l_sc[...] + jnp.sum(p, axis=1, keepdims=True)
        m_sc[...] = m_new
        return p, alpha

    def value_phase(v_pages, p, alpha):
        alpha_row = jnp.broadcast_to(alpha, (rows, rows)).T[0:1, :]
        pv = _dot(v_pages, p.T.astype(BF16))
        acc_sc[...] = alpha_row * acc_sc[...] + pv

    @pl.when(cj == 0)
    def _():
        m_sc[...] = jnp.full(m_sc.shape, NEG_INF, F32)
        l_sc[...] = jnp.zeros(l_sc.shape, F32)
        acc_sc[...] = jnp.zeros(acc_sc.shape, F32)
        carry_sc[...] = jnp.zeros(carry_sc.shape, F32)
        row_head = lax.broadcasted_iota(jnp.int32, (rows, feat), 0) // n_new
        col_head = lax.broadcasted_iota(jnp.int32, (rows, feat), 1) // HEAD_DIM
        qbd_sc[...] = jnp.where(row_head == col_head, tile_rows(q_ref[...] * ATTN_SCALE), 0.0).astype(BF16)
        f_new = _scan_rows(lfn_ref[...])
        rh = lax.broadcasted_iota(jnp.int32, (rows, V7X_LANES), 0) // n_new
        ln = lax.broadcasted_iota(jnp.int32, (rows, V7X_LANES), 1)
        qb = jnp.sum(jnp.where(rh == ln, tile_rows(f_new), 0.0), axis=1, keepdims=True)
        qb_sc[...] = qb
        f_pad = jnp.concatenate([f_new, jnp.zeros((PAGE_SIZE - n_new, V7X_LANES), F32)], axis=0)
        f_new_t = f_pad.T[:n_heads, :]
        tok = lax.broadcasted_iota(jnp.int32, (rows, PAGE_SIZE), 0) % n_new
        pos = lax.broadcasted_iota(jnp.int32, (rows, PAGE_SIZE), 1)
        bias = (qb - rep_rows(f_new_t)) + jnp.where(pos <= tok, 0.0, NEG_INF)
        p, alpha = score_phase(to_pages(kn_ref[...]).astype(BF16), bias)
        value_phase(to_pages(vn_ref[...]).astype(BF16), p, alpha)

    carry = carry_sc[...]
    biases = []
    for lf_ref in lf_refs:
        lf = lf_ref[...]
        incl = _scan_lanes(lf, reverse=True)
        biases.append(rep_rows((incl - lf) + carry))
        carry = carry + incl[:, 0:1]
    carry_sc[...] = carry
    k_pages = jnp.concatenate([r[...].reshape(feat, PAGE_SIZE).astype(BF16) for r in kc_refs], axis=1)
    p, alpha = score_phase(k_pages, qb_sc[...] + jnp.concatenate(biases, axis=1))
    if between is not None:
        between()
    v_pages = jnp.concatenate([r[...].reshape(feat, PAGE_SIZE).astype(BF16) for r in vc_refs], axis=1)
    value_phase(v_pages, p, alpha)

    @pl.when(cj == n_chunks - 1)
    def _():
        outs = []
        for h in range(n_heads):
            blk = acc_sc[h * HEAD_DIM:(h + 1) * HEAD_DIM, :].T
            outs.append(blk[h * n_new:(h + 1) * n_new, :])
        o_ref[...] = jnp.concatenate(outs, axis=0) / l_sc[...]


def _decode_scratch(n_heads, n_new):
    rows = n_heads * n_new
    feat = n_heads * HEAD_DIM
    return [
        pltpu.VMEM((rows, 1), F32),
        pltpu.VMEM((rows, 1), F32),
        pltpu.VMEM((feat, rows), F32),
        pltpu.VMEM((n_heads, PAGE_SIZE), F32),
        pltpu.VMEM((rows, feat), BF16),
        pltpu.VMEM((rows, 1), F32),
    ]


def _decode_in_specs(seq_of, chunk_of, *, n_heads, n_new, n_pg, n_pages, cols, n_lead):
    feat = n_heads * HEAD_DIM
    last = n_pages - 1

    def tok_spec(width, col):
        return pl.BlockSpec((n_new, width), lambda *a: (seq_of(*a[:n_lead]), col))

    def page_spec(g, tail):
        def idx(*a):
            pt = a[n_lead]
            page = pt[seq_of(*a[:n_lead]), last - (chunk_of(*a[:n_lead]) * n_pg + g)]
            return (0, page) + (0,) * (1 + len(tail))
        return pl.BlockSpec((None, None, n_heads) + tail, idx)

    kv_tail = (HEAD_DIM, PAGE_SIZE)
    specs = [tok_spec(feat, cols["q"] // feat), tok_spec(feat, cols["k"] // feat), tok_spec(feat, cols["v"] // feat),
             tok_spec(V7X_LANES, 0)]
    specs += [page_spec(g, kv_tail) for g in range(n_pg)]
    specs += [page_spec(g, kv_tail) for g in range(n_pg)]
    specs += [page_spec(g, (PAGE_SIZE,)) for g in range(n_pg)]
    return specs


def _decode_kernel(pt_ref, q_ref, kn_ref, vn_ref, lfn_ref, *rest, n_heads, n_new, n_pg):
    del pt_ref
    kc_refs = rest[0:n_pg]
    vc_refs = rest[n_pg:2 * n_pg]
    lf_refs = rest[2 * n_pg:3 * n_pg]
    o_ref = rest[3 * n_pg]
    _decode_step(pl.program_id(1), pl.num_programs(1), q_ref, kn_ref, vn_ref, lfn_ref, kc_refs, vc_refs, lf_refs,
                 o_ref, rest[3 * n_pg + 1:], n_heads=n_heads, n_new=n_new)


def _decode_attention(page_table, hs, lf_new, cache_kt, cache_vt, cache_lft, *, n_heads, n_new, n_pg, cols,
                      seq0, n_seq):
    n_pages = page_table.shape[1]
    rows = n_heads * n_new
    kern = functools.partial(_decode_kernel, n_heads=n_heads, n_new=n_new, n_pg=n_pg)
    grid_spec = pltpu.PrefetchScalarGridSpec(
        num_scalar_prefetch=1,
        grid=(n_seq, n_pages // n_pg),
        in_specs=_decode_in_specs(lambda b, j: b + seq0, lambda b, j: j, n_heads=n_heads, n_new=n_new, n_pg=n_pg,
                                  n_pages=n_pages, cols=cols, n_lead=2),
        out_specs=pl.BlockSpec((None, rows, HEAD_DIM), lambda b, j, pt: (b, 0, 0)),
        scratch_shapes=_decode_scratch(n_heads, n_new),
    )
    args = [page_table, hs, hs, hs, lf_new] + [cache_kt] * n_pg + [cache_vt] * n_pg + [cache_lft] * n_pg
    return pl.pallas_call(
        kern,
        grid_spec=grid_spec,
        out_shape=jax.ShapeDtypeStruct((n_seq, rows, HEAD_DIM), F32),
        compiler_params=_cparams(2),
        name="decode_attention",
    )(*args)


def _h_decode_kernel(pt_ref, xn_ref, w_ref, q_ref, kn_ref, vn_ref, lfn_ref, *rest, n_heads, n_new, n_pg, n_j,
                     n_chunks):
    del pt_ref
    kc_refs = rest[0:n_pg]
    vc_refs = rest[n_pg:2 * n_pg]
    lf_refs = rest[2 * n_pg:3 * n_pg]
    h_ref, o_ref = rest[3 * n_pg:3 * n_pg + 2]
    cj = (pl.program_id(0) * n_j + pl.program_id(1)) % n_chunks

    def project():
        h_ref[...] = _dot_nt(xn_ref[...], w_ref[...])

    _decode_step(cj, n_chunks, q_ref, kn_ref, vn_ref, lfn_ref, kc_refs, vc_refs, lf_refs, o_ref,
                 rest[3 * n_pg + 2:], n_heads=n_heads, n_new=n_new, between=project)


def _h_decode(plan, page_table, xn, wbf, hs, lf_new, cache_kt, cache_vt, cache_lft, *, tm, n_heads, n_new, n_pg):
    m, d = xn.shape
    tn = plan["tn"]
    cols = plan["cols"]
    n_j = plan["steps"]["q"][0]
    n_pages = page_table.shape[1]
    n_chunks = n_pages // n_pg
    n_steps = (m // tm) * n_j
    assert n_steps % n_chunks == 0 and n_steps // n_chunks <= page_table.shape[0]
    n_seq = n_steps // n_chunks
    rows = n_heads * n_new
    kern = functools.partial(_h_decode_kernel, n_heads=n_heads, n_new=n_new, n_pg=n_pg, n_j=n_j, n_chunks=n_chunks)
    in_specs = [
        pl.BlockSpec((tm, d), lambda i, j, pt: (i, 0)),
        pl.BlockSpec((tn, d), lambda i, j, pt: (j, 0)),
    ] + _decode_in_specs(lambda i, j: (i * n_j + j) // n_chunks, lambda i, j: (i * n_j + j) % n_chunks,
                         n_heads=n_heads, n_new=n_new, n_pg=n_pg, n_pages=n_pages, cols=cols, n_lead=2)
    grid_spec = pltpu.PrefetchScalarGridSpec(
        num_scalar_prefetch=1,
        grid=(m // tm, n_j),
        in_specs=in_specs,
        out_specs=[
            pl.BlockSpec((tm, tn), lambda i, j, pt: (i, j)),
            pl.BlockSpec((None, rows, HEAD_DIM), lambda i, j, pt: ((i * n_j + j) // n_chunks, 0, 0)),
        ],
        scratch_shapes=_decode_scratch(n_heads, n_new),
    )
    args = [page_table, xn, wbf, hs, hs, hs, lf_new] + [cache_kt] * n_pg + [cache_vt] * n_pg + [cache_lft] * n_pg
    h, att_rows = pl.pallas_call(
        kern,
        grid_spec=grid_spec,
        out_shape=[
            jax.ShapeDtypeStruct((m, n_j * tn), F32),
            jax.ShapeDtypeStruct((n_seq, rows, HEAD_DIM), F32),
        ],
        compiler_params=_cparams(2),
        name="h_decode",
    )(*args)
    return h, att_rows, n_seq


def _pool_diff(z, u, pos, group_w):
    s = z
    k = 1
    while k < group_w:
        s = s + pltpu.roll(s, k, 0)
        k *= 2
    cnt = jnp.minimum(pos + 1, group_w).astype(F32)
    return s[HALO:, :] / cnt - u


def _mix_tail(d_groups, gp, ga, att, ma, mb, x, wpool_ref, ps_ref, wup_ref, wua_ref, wout_ref):
    mixed = [_dot(d.astype(BF16), wpool_ref[gi]) for gi, d in enumerate(d_groups)]
    mixed = jnp.concatenate(mixed, axis=1)
    branch_a = (mixed * ps_ref[...]) * _silu(gp)
    branch_b = att * _silu(ga)
    up_a = _dot(branch_a.astype(BF16), wup_ref[...])
    up_b = _dot(branch_b.astype(BF16), wua_ref[...])
    merged = _sigmoid(ma) * up_a + _sigmoid(mb) * up_b
    return x + _dot(merged.astype(BF16), wout_ref[...])


def _out_prompt_kernel(u_ref, halo_ref, gp_ref, ga_ref, ma_ref, mb_ref, att_ref, x_ref,
                       wpool_ref, ps_ref, wup_ref, wua_ref, wout_ref, y_ref, *, tm, seq):
    i = pl.program_id(0)
    pos0 = (i * tm) % seq
    u = u_ref[...]
    halo = jnp.where(pos0 == 0, 0.0, halo_ref[...])
    z = jnp.concatenate([halo, u], axis=0)
    pos = pos0 + lax.broadcasted_iota(jnp.int32, (tm, 1), 0)
    gw = u.shape[1] // len(POOL_WINDOWS)
    d_groups = [_pool_diff(z[:, gi * gw:(gi + 1) * gw], u[:, gi * gw:(gi + 1) * gw], pos, w)
                for gi, w in enumerate(POOL_WINDOWS)]
    y_ref[...] = _mix_tail(d_groups, gp_ref[...], ga_ref[...], att_ref[...], ma_ref[...], mb_ref[...], x_ref[...],
                           wpool_ref, ps_ref, wup_ref, wua_ref, wout_ref)


def _out_sample_kernel(z_ref, gp_ref, ga_ref, ma_ref, mb_ref, att_ref, x_ref,
                       wpool_ref, ps_ref, wup_ref, wua_ref, wout_ref, y_ref, *, dec_b, n_new, pos0):
    gw = z_ref.shape[2] // len(POOL_WINDOWS)
    pos = pos0 + lax.broadcasted_iota(jnp.int32, (n_new, 1), 0)
    per_seq = []
    for b in range(dec_b):
        z = z_ref[b]
        u = z[HALO:, :]
        per_seq.append([_pool_diff(z[:, gi * gw:(gi + 1) * gw], u[:, gi * gw:(gi + 1) * gw], pos, w)
                        for gi, w in enumerate(POOL_WINDOWS)])
    d_groups = [jnp.concatenate([per_seq[b][gi] for b in range(dec_b)], axis=0) for gi in range(len(POOL_WINDOWS))]
    y_ref[...] = _mix_tail(d_groups, gp_ref[...], ga_ref[...], att_ref[...], ma_ref[...], mb_ref[...], x_ref[...],
                           wpool_ref, ps_ref, wup_ref, wua_ref, wout_ref)


def _weight_specs(pw, gw, dm):
    one = pl.Buffered(1)
    return [
        pl.BlockSpec((len(POOL_WINDOWS), gw, gw), lambda i: (0, 0, 0), pipeline_mode=one),
        pl.BlockSpec((1, pw), lambda i: (0, 0), pipeline_mode=one),
        pl.BlockSpec((pw, dm), lambda i: (0, 0), pipeline_mode=one),
        pl.BlockSpec((pw, dm), lambda i: (0, 0), pipeline_mode=one),
        pl.BlockSpec((dm, dm), lambda i: (0, 0), pipeline_mode=one),
    ]


def _out_prompt(h, att, x2d, wpool, ps, wup, wua, wout, *, seq, tm, cols):
    m, dm = x2d.shape
    pw = att.shape[1]
    gw = pw // len(POOL_WINDOWS)
    hb = tm // HALO
    kern = functools.partial(_out_prompt_kernel, tm=tm, seq=seq)
    return pl.pallas_call(
        kern,
        grid=(m // tm,),
        in_specs=[
            pl.BlockSpec((tm, pw), lambda i: (i, cols["u"] // pw)),
            pl.BlockSpec((HALO, pw), lambda i: (jnp.maximum(i * hb - 1, 0), cols["u"] // pw)),
            pl.BlockSpec((tm, pw), lambda i: (i, cols["gp"] // pw)),
            pl.BlockSpec((tm, pw), lambda i: (i, cols["ga"] // pw)),
            pl.BlockSpec((tm, dm), lambda i: (i, cols["ma"] // dm)),
            pl.BlockSpec((tm, dm), lambda i: (i, cols["mb"] // dm)),
            pl.BlockSpec((tm, pw), lambda i: (i, 0)),
            pl.BlockSpec((tm, dm), lambda i: (i, 0)),
        ] + _weight_specs(pw, gw, dm),
        out_specs=pl.BlockSpec((tm, dm), lambda i: (i, 0)),
        out_shape=jax.ShapeDtypeStruct((m, dm), F32),
        compiler_params=_cparams(1),
        name="out_prompt",
    )(h, h, h, h, h, h, att, x2d, wpool, ps, wup, wua, wout)


def _out_sample(z, h, att, x2d, wpool, ps, wup, wua, wout, *, dec_b, n_new, pos0, cols):
    m, dm = x2d.shape
    pw = att.shape[1]
    gw = pw // len(POOL_WINDOWS)
    kern = functools.partial(_out_sample_kernel, dec_b=dec_b, n_new=n_new, pos0=pos0)
    return pl.pallas_call(
        kern,
        grid=(1,),
        in_specs=[
            pl.BlockSpec((dec_b, HALO + n_new, pw), lambda i: (0, 0, 0)),
            pl.BlockSpec((m, pw), lambda i: (0, cols["gp"] // pw)),
            pl.BlockSpec((m, pw), lambda i: (0, cols["ga"] // pw)),
            pl.BlockSpec((m, dm), lambda i: (0, cols["ma"] // dm)),
            pl.BlockSpec((m, dm), lambda i: (0, cols["mb"] // dm)),
            pl.BlockSpec((m, pw), lambda i: (0, 0)),
            pl.BlockSpec((m, dm), lambda i: (0, 0)),
        ] + _weight_specs(pw, gw, dm),
        out_specs=pl.BlockSpec((m, dm), lambda i: (0, 0)),
        out_shape=jax.ShapeDtypeStruct((m, dm), F32),
        compiler_params=_cparams(1),
        name="out_sample",
    )(z, h, h, h, h, att, x2d, wpool, ps, wup, wua, wout)


def kernel(x_prompt, x_sample, cache_k, cache_v, cache_logf, state_pool, page_table, norm_gain, w_in, b_f,
           q_norm_gain, k_norm_gain, w_pool_map, pool_scale, w_up_pool, w_up_attn, w_out):
    batch, seq, dm = x_prompt.shape
    dec_b, n_new, _ = x_sample.shape
    assert w_in.shape[0] == 1
    n_heads = b_f.shape[1]
    aw = n_heads * HEAD_DIM
    pw = w_up_pool.shape[1]
    n_pages = page_table.shape[1]
    past_len = n_pages * PAGE_SIZE
    assert pw == aw and dm == 2 * pw and n_pages % PAGES_PER_STEP == 0

    tn = 512
    plan = _inproj_plan(pw, aw, dm, n_heads, tn)
    cols = plan["cols"]
    w_t = w_in[0].T
    fl0, fl_n = plan["fl_rows"]
    wfl = jnp.pad(w_t[fl0:fl0 + fl_n], ((0, V7X_LANES - n_heads), (0, 0))).astype(BF16)
    bfp = jnp.pad(b_f, ((0, 0), (0, V7X_LANES - n_heads)))
    reps = tn // HEAD_DIM
    qg = jnp.tile(q_norm_gain[0], reps)[None, :]
    kg = jnp.tile(k_norm_gain[0], reps)[None, :]
    qgc = jnp.broadcast_to(jnp.tile(q_norm_gain[0], reps)[:, None], (tn, V7X_LANES))
    kgc = jnp.broadcast_to(jnp.tile(k_norm_gain[0], reps)[:, None], (tn, V7X_LANES))
    seg = np.arange(tn) // HEAD_DIM
    bd = jnp.asarray((seg[:, None] == seg[None, :]).astype(np.float32)).astype(BF16)
    wpool = w_pool_map[0].astype(BF16)
    wup = w_up_pool[0].astype(BF16)
    wua = w_up_attn[0].astype(BF16)
    wout = w_out[0].astype(BF16)

    ms = dec_b * n_new
    xs2 = x_sample.reshape(ms, dm)
    hs, lfs, wbf = _inproj_new(plan, xs2, norm_gain, w_t, wfl, bfp, qg, kg, bd)

    cache_kt = cache_k.transpose(0, 1, 3, 4, 2)
    cache_vt = cache_v.transpose(0, 1, 3, 4, 2)
    cache_lft = cache_logf.transpose(0, 1, 3, 2)
    decode_args = dict(n_heads=n_heads, n_new=n_new, n_pg=DECODE_PAGES_FUSED)

    xp2 = x_prompt.reshape(batch * seq, dm)
    xn_p, lfp = _prenorm(xp2, norm_gain, wfl, bfp, tm=512)
    hp, att_rows_a, n_fused = _h_decode(plan, page_table, xn_p, wbf, hs, lfs, cache_kt, cache_vt, cache_lft,
                                        tm=1024, **decode_args)
    qt_p, kt_p, vt_p = _inproj_fm(plan, xn_p, wbf, qgc, kgc, tm=1024, seq=seq)
    fcol, ft, lft = _fcum(lfp, batch=batch, seq=seq, n_heads=n_heads)
    ft_pairs = ft.reshape(batch, n_heads // 2, 2, seq)
    att_p = _prompt_attention(qt_p, kt_p, vt_p, fcol, ft_pairs, batch=batch, seq=seq, n_heads=n_heads, tq=256,
                              n_pair=ATTN_PAIRS_PER_STEP)
    yp = _out_prompt(hp, att_p, xp2, wpool, pool_scale, wup, wua, wout, seq=seq, tm=256, cols=cols)

    k_s = hs[:, cols["k"]:cols["k"] + aw].reshape(dec_b, n_new, n_heads, HEAD_DIM)
    v_s = hs[:, cols["v"]:cols["v"] + aw].reshape(dec_b, n_new, n_heads, HEAD_DIM)
    u_s = hs[:, cols["u"]:cols["u"] + pw].reshape(dec_b, n_new, pw)
    logf_s = lfs[:, :n_heads].reshape(dec_b, n_new, n_heads)
    att_rows = att_rows_a
    if n_fused < dec_b:
        att_rows_b = _decode_attention(page_table, hs, lfs, cache_kt, cache_vt, cache_lft, n_heads=n_heads,
                                       n_new=n_new, n_pg=PAGES_PER_STEP, cols=cols, seq0=n_fused,
                                       n_seq=dec_b - n_fused)
        att_rows = jnp.concatenate([att_rows_a, att_rows_b], axis=0)
    att_s = att_rows.reshape(dec_b, n_heads, n_new, HEAD_DIM).transpose(0, 2, 1, 3).reshape(ms, aw)
    z_s = jnp.concatenate([jnp.zeros((dec_b, HALO - POOL_BUF, pw), F32), state_pool[0], u_s], axis=1)
    ys = _out_sample(z_s, hs, att_s, xs2, wpool, pool_scale, wup, wua, wout,
                     dec_b=dec_b, n_new=n_new, pos0=past_len, cols=cols)

    k_p = kt_p.reshape(1, batch, n_heads, HEAD_DIM, seq).transpose(0, 1, 4, 2, 3)
    v_p = vt_p.reshape(1, batch, n_heads, HEAD_DIM, seq).transpose(0, 1, 4, 2, 3)
    logf_p = lft.transpose(0, 2, 1)[None]
    pool_p = hp.reshape(batch, seq, hp.shape[1])[:, seq - POOL_BUF:, cols["u"]:cols["u"] + pw][None]
    pool_s = z_s[:, HALO + n_new - POOL_BUF:, :][None]
    return (yp.reshape(batch, seq, dm), ys.reshape(dec_b, n_new, dm), k_p, v_p, logf_p, pool_p,
            k_s[None], v_s[None], logf_s[None], pool_s)
```

```python
import functools

import jax
import jax.numpy as jnp
import numpy as np
from jax import lax
from jax.experimental import pallas as pl
from jax.experimental.pallas import tpu as pltpu

F32 = jnp.float32
BF16 = jnp.bfloat16

HEAD_DIM = 64
POOL_WINDOWS = (2, 4, 8, 16)
POOL_BUF = 15
PAGE_SIZE = 128
EPS = 1e-6
NEG_INF = -1e30
ATTN_SCALE = HEAD_DIM ** -0.5
LOG2E = 1.4426950408889634

V7X_LANES = 128
V7X_SUBLANES = 8
V7X_VMEM_LIMIT_BYTES = 56 * 1024 * 1024

ROW_ALIGN = 16
HALO = 16
PAGES_PER_STEP = 16
DECODE_PAGES_FUSED = 8
N_AUG = 3
ATTN_PAIRS_PER_STEP = 8


def _cparams(n_grid_axes):
    return pltpu.CompilerParams(
        dimension_semantics=("arbitrary",) * n_grid_axes,
        vmem_limit_bytes=V7X_VMEM_LIMIT_BYTES,
    )


def _split3(x):
    hi = x.astype(BF16)
    r1 = x - hi.astype(F32)
    mid = r1.astype(BF16)
    lo = (r1 - mid.astype(F32)).astype(BF16)
    return hi, mid, lo


def _dot(a, b):
    return jnp.dot(a, b, preferred_element_type=F32)


def _dot_nt(a, b):
    return lax.dot_general(a, b, (((1,), (1,)), ((), ())), preferred_element_type=F32)


def _sigmoid(x):
    return 1.0 / (1.0 + jnp.exp(-x))


def _silu(x):
    return x * _sigmoid(x)


def _scan_lanes(x, *, reverse):
    n = x.shape[-1]
    ax = x.ndim - 1
    lane = lax.broadcasted_iota(jnp.int32, x.shape, ax)
    k = 1
    while k < n:
        if reverse:
            shifted = pltpu.roll(x, n - k, ax)
            x = x + jnp.where(lane < n - k, shifted, 0.0)
        else:
            shifted = pltpu.roll(x, k, ax)
            x = x + jnp.where(lane >= k, shifted, 0.0)
        k *= 2
    return x


def _scan_rows(x):
    n = x.shape[0]
    row = lax.broadcasted_iota(jnp.int32, x.shape, 0)
    k = 1
    while k < n:
        x = x + jnp.where(row >= k, pltpu.roll(x, k, 0), 0.0)
        k *= 2
    return x


def _norm_and_logf(x_ref, ng_ref, wfl_ref, bf_ref, xn_sc, lf_ref):
    x = x_ref[...]
    ms = jnp.mean(x * x, axis=-1, keepdims=True)
    xn = x * lax.rsqrt(ms + EPS) * ng_ref[...]
    xn_sc[...] = xn.astype(BF16)
    z = _dot_nt(xn_sc[...], wfl_ref[...]) + bf_ref[...]
    lf_ref[...] = jnp.minimum(z, 0.0) - jnp.log1p(jnp.exp(-jnp.abs(z)))


def _step_in(j, rng):
    return jnp.logical_and(j >= rng[0], j < rng[1])


def _inproj_new_kernel(off_ref, x_ref, ng_ref, w_ref, wfl_ref, bf_ref, qg_ref, kg_ref, bd_ref,
                       h_ref, lf_ref, wbf_ref, xn_sc, *, steps):
    del off_ref
    j = pl.program_id(0)

    @pl.when(j == 0)
    def _():
        _norm_and_logf(x_ref, ng_ref, wfl_ref, bf_ref, xn_sc, lf_ref)

    w = w_ref[...].astype(BF16)
    wbf_ref[...] = w
    acc = _dot_nt(xn_sc[...], w)

    def head_norm(gain):
        sq = acc * acc
        hi = sq.astype(BF16)
        lo = (sq - hi.astype(F32)).astype(BF16)
        ss = _dot(hi, bd_ref[...]) + _dot(lo, bd_ref[...])
        return acc * lax.rsqrt(ss * (1.0 / HEAD_DIM) + EPS) * gain

    is_q = _step_in(j, steps["q"])
    is_k = _step_in(j, steps["k"])

    @pl.when(is_q)
    def _():
        h_ref[...] = head_norm(qg_ref[...])

    @pl.when(is_k)
    def _():
        h_ref[...] = head_norm(kg_ref[...])

    @pl.when(jnp.logical_not(is_q | is_k))
    def _():
        h_ref[...] = acc


def _inproj_new(plan, x2d, ng, w_t, wfl, bfp, qg, kg, bd):
    m, d = x2d.shape
    tn = plan["tn"]
    nj = plan["n_steps"]
    kern = functools.partial(_inproj_new_kernel, steps=plan["steps"])
    grid_spec = pltpu.PrefetchScalarGridSpec(
        num_scalar_prefetch=1,
        grid=(nj,),
        in_specs=[
            pl.BlockSpec((m, d), lambda j, off: (0, 0)),
            pl.BlockSpec((1, d), lambda j, off: (0, 0)),
            pl.BlockSpec((pl.Element(tn), pl.Element(d)), lambda j, off: (off[j] * ROW_ALIGN, 0)),
            pl.BlockSpec((V7X_LANES, d), lambda j, off: (0, 0)),
            pl.BlockSpec((1, V7X_LANES), lambda j, off: (0, 0)),
            pl.BlockSpec((1, tn), lambda j, off: (0, 0)),
            pl.BlockSpec((1, tn), lambda j, off: (0, 0)),
            pl.BlockSpec((tn, tn), lambda j, off: (0, 0)),
        ],
        out_specs=[
            pl.BlockSpec((m, tn), lambda j, off: (0, j)),
            pl.BlockSpec((m, V7X_LANES), lambda j, off: (0, 0)),
            pl.BlockSpec((tn, d), lambda j, off: (j, 0)),
        ],
        scratch_shapes=[pltpu.VMEM((m, d), BF16)],
    )
    return pl.pallas_call(
        kern,
        grid_spec=grid_spec,
        out_shape=[
            jax.ShapeDtypeStruct((m, nj * tn), F32),
            jax.ShapeDtypeStruct((m, V7X_LANES), F32),
            jax.ShapeDtypeStruct((nj * tn, d), BF16),
        ],
        compiler_params=_cparams(1),
        name="inproj_new",
    )(plan["row_offsets"], x2d, ng, w_t, wfl, bfp, qg, kg, bd)


def _prenorm_kernel(x_ref, ng_ref, wfl_ref, bf_ref, xn_ref, lf_ref):
    _norm_and_logf(x_ref, ng_ref, wfl_ref, bf_ref, xn_ref, lf_ref)


def _prenorm(x2d, ng, wfl, bfp, *, tm):
    m, d = x2d.shape
    return pl.pallas_call(
        _prenorm_kernel,
        grid=(m // tm,),
        in_specs=[
            pl.BlockSpec((tm, d), lambda i: (i, 0)),
            pl.BlockSpec((1, d), lambda i: (0, 0)),
            pl.BlockSpec((V7X_LANES, d), lambda i: (0, 0)),
            pl.BlockSpec((1, V7X_LANES), lambda i: (0, 0)),
        ],
        out_specs=[
            pl.BlockSpec((tm, d), lambda i: (i, 0)),
            pl.BlockSpec((tm, V7X_LANES), lambda i: (i, 0)),
        ],
        out_shape=[
            jax.ShapeDtypeStruct((m, d), BF16),
            jax.ShapeDtypeStruct((m, V7X_LANES), F32),
        ],
        compiler_params=_cparams(1),
        name="prenorm",
    )(x2d, ng, wfl, bfp)


def _inproj_fm_kernel(xn_ref, w_ref, qgc_ref, kgc_ref, qt_ref, kt_ref, vt_ref):
    j = pl.program_id(1)

    def head_norm(acc, gain_col_ref):
        tn, tm = acc.shape
        a3 = acc.reshape(tn // HEAD_DIM, HEAD_DIM, tm)
        ms = jnp.mean(a3 * a3, axis=1, keepdims=True)
        g3 = gain_col_ref[...].reshape(tn // HEAD_DIM, HEAD_DIM, V7X_LANES)[:, :, 0:1]
        return (a3 * lax.rsqrt(ms + EPS) * g3).reshape(tn, tm)

    @pl.when(j == 0)
    def _():
        qt_ref[...] = head_norm(_dot_nt(w_ref[...], xn_ref[...]), qgc_ref)

    @pl.when(j == 1)
    def _():
        kt_ref[...] = head_norm(_dot_nt(w_ref[...], xn_ref[...]), kgc_ref)

    @pl.when(j == 2)
    def _():
        vt_ref[...] = _dot_nt(w_ref[...], xn_ref[...])


def _inproj_fm(plan, xn, wbf, qgc, kgc, *, tm, seq, width):
    m, d = xn.shape
    cols = plan["cols"]
    blk0 = cols["q"] // width
    assert cols["q"] % width == 0 and cols["k"] == cols["q"] + width and cols["v"] == cols["k"] + width
    nb = seq // tm
    out_spec = pl.BlockSpec((None, width, tm), lambda i, j: (i // nb, 0, i % nb))
    return pl.pallas_call(
        _inproj_fm_kernel,
        grid=(m // tm, 3),
        in_specs=[
            pl.BlockSpec((tm, d), lambda i, j: (i, 0)),
            pl.BlockSpec((width, d), lambda i, j: (blk0 + j, 0)),
            pl.BlockSpec((width, V7X_LANES), lambda i, j: (0, 0)),
            pl.BlockSpec((width, V7X_LANES), lambda i, j: (0, 0)),
        ],
        out_specs=[out_spec] * 3,
        out_shape=[jax.ShapeDtypeStruct((m // seq, width, seq), F32)] * 3,
        compiler_params=_cparams(2),
        name="inproj_fm",
    )(xn, wbf, qgc, kgc)


def _inproj_plan(pw, aw, dm, n_heads, tn):
    src, off = {}, 0
    for name, size in (("u", pw), ("gp", pw), ("q", aw), ("k", aw), ("v", aw), ("ga", aw), ("fl", n_heads),
                       ("ma", dm), ("mb", dm)):
        src[name] = (off, size)
        off += size
    order = ("ma", "mb", "u", "gp", "ga", "q", "k", "v")
    row_offsets, steps, cols = [], {}, {}
    for name in order:
        base, size = src[name]
        steps[name] = (len(row_offsets), len(row_offsets) + size // tn)
        cols[name] = len(row_offsets) * tn
        row_offsets += [base + o for o in range(0, size, tn)]
    assert all(r % ROW_ALIGN == 0 for r in row_offsets)
    return {"tn": tn, "steps": steps, "n_steps": len(row_offsets), "cols": cols, "fl_rows": src["fl"],
            "row_offsets": jnp.asarray([r // ROW_ALIGN for r in row_offsets], jnp.int32)}


def _fcum_kernel(lf_ref, fcol_ref, ft_ref, lft_ref, *, seq, n_heads):
    c = V7X_LANES
    row = lax.broadcasted_iota(jnp.int32, (c, c), 0)
    col = lax.broadcasted_iota(jnp.int32, (c, c), 1)
    tri = jnp.where(col <= row, 1.0, 0.0).astype(BF16)
    carry = jnp.zeros((1, c), F32)
    for ci in range(seq // c):
        x = lf_ref[ci * c:(ci + 1) * c, :]
        hi, mid, lo = _split3(x)
        fc = (_dot(tri, hi) + _dot(tri, mid)) + _dot(tri, lo) + carry
        fcol_ref[ci * c:(ci + 1) * c, :] = fc
        ft_ref[:, ci * c:(ci + 1) * c] = fc.T[:n_heads, :]
        lft_ref[:, ci * c:(ci + 1) * c] = x.T[:n_heads, :]
        carry = fc[c - 1:c, :]


def _fcum(lf2d, *, batch, seq, n_heads):
    kern = functools.partial(_fcum_kernel, seq=seq, n_heads=n_heads)
    return pl.pallas_call(
        kern,
        grid=(batch,),
        in_specs=[pl.BlockSpec((seq, V7X_LANES), lambda b: (b, 0))],
        out_specs=[
            pl.BlockSpec((seq, V7X_LANES), lambda b: (b, 0)),
            pl.BlockSpec((None, n_heads, seq), lambda b: (b, 0, 0)),
            pl.BlockSpec((None, n_heads, seq), lambda b: (b, 0, 0)),
        ],
        out_shape=[
            jax.ShapeDtypeStruct((batch * seq, V7X_LANES), F32),
            jax.ShapeDtypeStruct((batch, n_heads, seq), F32),
            jax.ShapeDtypeStruct((batch, n_heads, seq), F32),
        ],
        compiler_params=_cparams(1),
        name="forget_cumsum",
    )(lf2d)


def _attn_kernel(qt_ref, kt_ref, vt_ref, fcol_ref, ft_ref, o_ref, ka_sc, *, tq, seq, n_pair):
    g = pl.program_id(1)
    i = pl.program_id(2)
    d = HEAD_DIM
    lanes = V7X_LANES
    lane1 = lax.broadcasted_iota(jnp.int32, (1, lanes), 1)

    @pl.when(i == 0)
    def _():
        prow = lax.broadcasted_iota(jnp.int32, (N_AUG * lanes, lanes), 0)
        lane = lax.broadcasted_iota(jnp.int32, (N_AUG * lanes, lanes), 1)
        piece, hrow = prow // lanes, prow % lanes
        ones = jnp.where(((lane1 >= d) & (lane1 < d + N_AUG)) | (lane1 < N_AUG), 1.0, 0.0)
        sels = []
        for pr in range(n_pair):
            head0 = 2 * (g * n_pair + pr)
            hit = ((hrow == head0) & (lane == d + N_AUG + piece)) | ((hrow == head0 + 1) & (lane == N_AUG + piece))
            sels.append(jnp.where(hit, -1.0, 0.0).astype(BF16))
        ck = 256
        for c in range(seq // ck):
            pieces = jnp.concatenate(_split3(fcol_ref[c * ck:(c + 1) * ck, :] * LOG2E), axis=1)
            for pr in range(n_pair):
                k_rows = kt_ref[pr * lanes:(pr + 1) * lanes, c * ck:(c + 1) * ck].T
                aug = ones + _dot(pieces, sels[pr])
                ka_sc[2 * pr, c * ck:(c + 1) * ck, :] = jnp.where(lane1 < d, k_rows, aug).astype(BF16)
                ka_sc[2 * pr + 1, c * ck:(c + 1) * ck, :] = jnp.where(lane1 >= d, k_rows, aug).astype(BF16)

    qs = pl.multiple_of(i * tq, tq)
    rowi = lax.broadcasted_iota(jnp.int32, (d, tq), 0)

    def aug_rows(f_row):
        hi, mid, lo = (x.astype(F32) for x in _split3(f_row))
        return jnp.where(rowi == 0, hi, jnp.where(rowi == 1, mid, jnp.where(rowi == 2, lo,
                         jnp.where(rowi < 2 * N_AUG, 1.0, 0.0))))

    qa = []
    for pr in range(n_pair):
        qt = qt_ref[pr * lanes:(pr + 1) * lanes, :] * (ATTN_SCALE * LOG2E)
        fq = ft_ref[pr, :, pl.ds(qs, tq)] * LOG2E
        qa.append(jnp.concatenate([qt[:d, :], aug_rows(fq[0:1, :])], axis=0).astype(BF16))
        qa.append(jnp.concatenate([aug_rows(fq[1:2, :]), qt[d:, :]], axis=0).astype(BF16))
    krow = lax.broadcasted_iota(jnp.int32, (tq, tq), 0)
    qcol = lax.broadcasted_iota(jnp.int32, (tq, tq), 1)
    causal = krow <= qcol

    def step(j, carry, masked):
        ks = pl.multiple_of(j * tq, tq)
        heads = range(2 * n_pair)
        scores = [_dot(ka_sc[hh, pl.ds(ks, tq), :], qa[hh]) for hh in heads]
        stats, probs = [], []
        for hh in heads:
            m, l, _ = carry[hh]
            s = jnp.where(causal, scores[hh], NEG_INF) if masked else scores[hh]
            m_new = jnp.maximum(m, jnp.max(s, axis=0, keepdims=True))
            alpha = jnp.exp2(m - m_new)
            pt = jnp.exp2(s - m_new)
            stats.append((m_new, alpha * l + jnp.sum(pt, axis=0, keepdims=True), alpha))
            probs.append(pt.astype(BF16))
        pvs = [_dot(vt_ref[hh * d:(hh + 1) * d, pl.ds(ks, tq)].astype(BF16), probs[hh]) for hh in heads]
        return tuple((stats[hh][0], stats[hh][1], stats[hh][2] * carry[hh][2] + pvs[hh]) for hh in heads)

    init = tuple((jnp.full((1, tq), NEG_INF, F32), jnp.zeros((1, tq), F32), jnp.zeros((d, tq), F32))
                 for _ in range(2 * n_pair))
    carry = lax.fori_loop(0, i, lambda j, c: step(j, c, False), init)
    final = step(i, carry, True)
    o_ref[...] = jnp.concatenate([acc / l for (_, l, acc) in final], axis=0).T


def _prompt_attention(qt, kt, vt, fcol, ft_pairs, *, batch, seq, n_heads, tq, n_pair):
    n_groups = n_heads // (2 * n_pair)
    qn = seq // tq
    rows = n_pair * V7X_LANES
    kern = functools.partial(_attn_kernel, tq=tq, seq=seq, n_pair=n_pair)
    return pl.pallas_call(
        kern,
        grid=(batch, n_groups, qn),
        in_specs=[
            pl.BlockSpec((None, rows, tq), lambda b, g, i: (b, g, i)),
            pl.BlockSpec((None, rows, seq), lambda b, g, i: (b, g, 0)),
            pl.BlockSpec((None, rows, seq), lambda b, g, i: (b, g, 0)),
            pl.BlockSpec((seq, V7X_LANES), lambda b, g, i: (b, 0)),
            pl.BlockSpec((None, n_pair, 2, seq), lambda b, g, i: (b, g, 0, 0)),
        ],
        out_specs=pl.BlockSpec((tq, rows), lambda b, g, i: (b * qn + i, g)),
        out_shape=jax.ShapeDtypeStruct((batch * seq, n_heads * HEAD_DIM), F32),
        scratch_shapes=[pltpu.VMEM((2 * n_pair, seq, V7X_LANES), BF16)],
        compiler_params=_cparams(3),
        name="prompt_attention",
    )(qt, kt, vt, fcol, ft_pairs)


def _decode_step(cj, n_chunks, q_ref, kn_ref, vn_ref, lfn_ref, kc_refs, vc_refs, lf_refs, o_ref, scratch,
                 *, n_heads, n_new, between=None):
    m_sc, l_sc, acc_sc, carry_sc, qbd_sc, qb_sc = scratch
    rows = n_heads * n_new
    feat = n_heads * HEAD_DIM

    def tile_rows(x):
        return jnp.concatenate([x] * n_heads, axis=0)

    def rep_rows(x):
        return jnp.concatenate([jnp.broadcast_to(x[h:h + 1, :], (n_new, x.shape[1])) for h in range(n_heads)], axis=0)

    def to_pages(x):
        xp = jnp.concatenate([x, jnp.zeros((PAGE_SIZE - n_new, feat), F32)], axis=0)
        return jnp.concatenate([xp[:, c * PAGE_SIZE:(c + 1) * PAGE_SIZE].T for c in range(feat // PAGE_SIZE)], axis=0)

    def score_phase(k_pages, bias):
        s = _dot(qbd_sc[...], k_pages) + bias
        m_prev = m_sc[...]
        m_new = jnp.maximum(m_prev, jnp.max(s, axis=1, keepdims=True))
        alpha = jnp.exp(m_prev - m_new)
        p = jnp.exp(s - m_new)
        l_sc[...] = alpha * l_sc[...] + jnp.sum(p, axis=1, keepdims=True)
        m_sc[...] = m_new
        return p, alpha

    def value_phase(v_pages, p, alpha):
        acc_sc[...] = alpha * acc_sc[...] + _dot_nt(p.astype(BF16), v_pages)

    @pl.when(cj == 0)
    def _():
        m_sc[...] = jnp.full(m_sc.shape, NEG_INF, F32)
        l_sc[...] = jnp.zeros(l_sc.shape, F32)
        acc_sc[...] = jnp.zeros(acc_sc.shape, F32)
        carry_sc[...] = jnp.zeros(carry_sc.shape, F32)
        row_head = lax.broadcasted_iota(jnp.int32, (rows, feat), 0) // n_new
        col_head = lax.broadcasted_iota(jnp.int32, (rows, feat), 1) // HEAD_DIM
        qbd_sc[...] = jnp.where(row_head == col_head, tile_rows(q_ref[...] * ATTN_SCALE), 0.0).astype(BF16)
        f_new = _scan_rows(lfn_ref[...])
        rh = lax.broadcasted_iota(jnp.int32, (rows, V7X_LANES), 0) // n_new
        ln = lax.broadcasted_iota(jnp.int32, (rows, V7X_LANES), 1)
        qb = jnp.sum(jnp.where(rh == ln, tile_rows(f_new), 0.0), axis=1, keepdims=True)
        qb_sc[...] = qb
        f_pad = jnp.concatenate([f_new, jnp.zeros((PAGE_SIZE - n_new, V7X_LANES), F32)], axis=0)
        f_new_t = f_pad.T[:n_heads, :]
        tok = lax.broadcasted_iota(jnp.int32, (rows, PAGE_SIZE), 0) % n_new
        pos = lax.broadcasted_iota(jnp.int32, (rows, PAGE_SIZE), 1)
        bias = (qb - rep_rows(f_new_t)) + jnp.where(pos <= tok, 0.0, NEG_INF)
        p, alpha = score_phase(to_pages(kn_ref[...]).astype(BF16), bias)
        value_phase(to_pages(vn_ref[...]).astype(BF16), p, alpha)

    carry = carry_sc[...]
    biases = []
    for lf_ref in lf_refs:
        lf = lf_ref[...]
        incl = _scan_lanes(lf, reverse=True)
        biases.append(rep_rows((incl - lf) + carry))
        carry = carry + incl[:, 0:1]
    carry_sc[...] = carry
    k_pages = jnp.concatenate([r[...].reshape(feat, PAGE_SIZE).astype(BF16) for r in kc_refs], axis=1)
    p, alpha = score_phase(k_pages, qb_sc[...] + jnp.concatenate(biases, axis=1))
    if between is not None:
        between()
    v_pages = jnp.concatenate([r[...].reshape(feat, PAGE_SIZE).astype(BF16) for r in vc_refs], axis=1)
    value_phase(v_pages, p, alpha)

    @pl.when(cj == n_chunks - 1)
    def _():
        outs = [acc_sc[h * n_new:(h + 1) * n_new, h * HEAD_DIM:(h + 1) * HEAD_DIM] for h in range(n_heads)]
        o_ref[...] = jnp.concatenate(outs, axis=0) / l_sc[...]


def _decode_scratch(n_heads, n_new):
    rows = n_heads * n_new
    feat = n_heads * HEAD_DIM
    return [
        pltpu.VMEM((rows, 1), F32),
        pltpu.VMEM((rows, 1), F32),
        pltpu.VMEM((rows, feat), F32),
        pltpu.VMEM((n_heads, PAGE_SIZE), F32),
        pltpu.VMEM((rows, feat), BF16),
        pltpu.VMEM((rows, 1), F32),
    ]


def _decode_in_specs(seq_of, chunk_of, *, n_heads, n_new, n_pg, n_pages, cols, n_lead):
    feat = n_heads * HEAD_DIM
    last = n_pages - 1

    def tok_spec(width, col):
        return pl.BlockSpec((n_new, width), lambda *a: (seq_of(*a[:n_lead]), col))

    def page_spec(g, tail):
        def idx(*a):
            pt = a[n_lead]
            page = pt[seq_of(*a[:n_lead]), last - (chunk_of(*a[:n_lead]) * n_pg + g)]
            return (0, page) + (0,) * (1 + len(tail))
        return pl.BlockSpec((None, None, n_heads) + tail, idx)

    kv_tail = (HEAD_DIM, PAGE_SIZE)
    specs = [tok_spec(feat, cols["q"] // feat), tok_spec(feat, cols["k"] // feat), tok_spec(feat, cols["v"] // feat),
             tok_spec(V7X_LANES, 0)]
    specs += [page_spec(g, kv_tail) for g in range(n_pg)]
    specs += [page_spec(g, kv_tail) for g in range(n_pg)]
    specs += [page_spec(g, (PAGE_SIZE,)) for g in range(n_pg)]
    return specs


def _decode_kernel(pt_ref, q_ref, kn_ref, vn_ref, lfn_ref, *rest, n_heads, n_new, n_pg):
    del pt_ref
    kc_refs = rest[0:n_pg]
    vc_refs = rest[n_pg:2 * n_pg]
    lf_refs = rest[2 * n_pg:3 * n_pg]
    o_ref = rest[3 * n_pg]
    _decode_step(pl.program_id(1), pl.num_programs(1), q_ref, kn_ref, vn_ref, lfn_ref, kc_refs, vc_refs, lf_refs,
                 o_ref, rest[3 * n_pg + 1:], n_heads=n_heads, n_new=n_new)


def _decode_attention(page_table, hs, lf_new, cache_kt, cache_vt, cache_lft, *, n_heads, n_new, n_pg, cols,
                      seq0, n_seq):
    n_pages = page_table.shape[1]
    rows = n_heads * n_new
    kern = functools.partial(_decode_kernel, n_heads=n_heads, n_new=n_new, n_pg=n_pg)
    grid_spec = pltpu.PrefetchScalarGridSpec(
        num_scalar_prefetch=1,
        grid=(n_seq, n_pages // n_pg),
        in_specs=_decode_in_specs(lambda b, j: b + seq0, lambda b, j: j, n_heads=n_heads, n_new=n_new, n_pg=n_pg,
                                  n_pages=n_pages, cols=cols, n_lead=2),
        out_specs=pl.BlockSpec((None, rows, HEAD_DIM), lambda b, j, pt: (b, 0, 0)),
        scratch_shapes=_decode_scratch(n_heads, n_new),
    )
    args = [page_table, hs, hs, hs, lf_new] + [cache_kt] * n_pg + [cache_vt] * n_pg + [cache_lft] * n_pg
    return pl.pallas_call(
        kern,
        grid_spec=grid_spec,
        out_shape=jax.ShapeDtypeStruct((n_seq, rows, HEAD_DIM), F32),
        compiler_params=_cparams(2),
        name="decode_attention",
    )(*args)


def _h_decode_kernel(pt_ref, xn_ref, w_ref, q_ref, kn_ref, vn_ref, lfn_ref, *rest, n_heads, n_new, n_pg, n_j,
                     n_chunks):
    del pt_ref
    kc_refs = rest[0:n_pg]
    vc_refs = rest[n_pg:2 * n_pg]
    lf_refs = rest[2 * n_pg:3 * n_pg]
    h_ref, o_ref = rest[3 * n_pg:3 * n_pg + 2]
    cj = (pl.program_id(0) * n_j + pl.program_id(1)) % n_chunks

    def project():
        h_ref[...] = _dot_nt(xn_ref[...], w_ref[...])

    _decode_step(cj, n_chunks, q_ref, kn_ref, vn_ref, lfn_ref, kc_refs, vc_refs, lf_refs, o_ref,
                 rest[3 * n_pg + 2:], n_heads=n_heads, n_new=n_new, between=project)


def _h_decode(plan, page_table, xn, wbf, hs, lf_new, cache_kt, cache_vt, cache_lft, *, tm, n_heads, n_new, n_pg):
    m, d = xn.shape
    tn = plan["tn"]
    cols = plan["cols"]
    n_j = plan["steps"]["q"][0]
    n_pages = page_table.shape[1]
    n_chunks = n_pages // n_pg
    n_steps = (m // tm) * n_j
    assert n_steps % n_chunks == 0 and n_steps // n_chunks <= page_table.shape[0]
    n_seq = n_steps // n_chunks
    rows = n_heads * n_new
    kern = functools.partial(_h_decode_kernel, n_heads=n_heads, n_new=n_new, n_pg=n_pg, n_j=n_j, n_chunks=n_chunks)
    in_specs = [
        pl.BlockSpec((tm, d), lambda i, j, pt: (i, 0)),
        pl.BlockSpec((tn, d), lambda i, j, pt: (j, 0)),
    ] + _decode_in_specs(lambda i, j: (i * n_j + j) // n_chunks, lambda i, j: (i * n_j + j) % n_chunks,
                         n_heads=n_heads, n_new=n_new, n_pg=n_pg, n_pages=n_pages, cols=cols, n_lead=2)
    grid_spec = pltpu.PrefetchScalarGridSpec(
        num_scalar_prefetch=1,
        grid=(m // tm, n_j),
        in_specs=in_specs,
        out_specs=[
            pl.BlockSpec((tm, tn), lambda i, j, pt: (i, j)),
            pl.BlockSpec((None, rows, HEAD_DIM), lambda i, j, pt: ((i * n_j + j) // n_chunks, 0, 0)),
        ],
        scratch_shapes=_decode_scratch(n_heads, n_new),
    )
    args = [page_table, xn, wbf, hs, hs, hs, lf_new] + [cache_kt] * n_pg + [cache_vt] * n_pg + [cache_lft] * n_pg
    h, att_rows = pl.pallas_call(
        kern,
        grid_spec=grid_spec,
        out_shape=[
            jax.ShapeDtypeStruct((m, n_j * tn), F32),
            jax.ShapeDtypeStruct((n_seq, rows, HEAD_DIM), F32),
        ],
        compiler_params=_cparams(2),
        name="h_decode",
    )(*args)
    return h, att_rows, n_seq


def _pool_diff(z, u, pos, group_w):
    s = z
    k = 1
    while k < group_w:
        s = s + pltpu.roll(s, k, 0)
        k *= 2
    cnt = jnp.minimum(pos + 1, group_w).astype(F32)
    return s[HALO:, :] / cnt - u


def _mix_tail(d_groups, gp, ga, att, ma, mb, x, wpool_ref, ps_ref, wup_ref, wua_ref, wout_ref):
    mixed = [_dot(d.astype(BF16), wpool_ref[gi]) for gi, d in enumerate(d_groups)]
    mixed = jnp.concatenate(mixed, axis=1)
    branch_a = (mixed * ps_ref[...]) * _silu(gp)
    branch_b = att * _silu(ga)
    up_a = _dot(branch_a.astype(BF16), wup_ref[...])
    up_b = _dot(branch_b.astype(BF16), wua_ref[...])
    merged = _sigmoid(ma) * up_a + _sigmoid(mb) * up_b
    return x + _dot(merged.astype(BF16), wout_ref[...])


def _out_prompt_kernel(u_ref, halo_ref, gp_ref, ga_ref, ma_ref, mb_ref, att_ref, x_ref,
                       wpool_ref, ps_ref, wup_ref, wua_ref, wout_ref, y_ref, *, tm, seq):
    i = pl.program_id(0)
    pos0 = (i * tm) % seq
    u = u_ref[...]
    halo = jnp.where(pos0 == 0, 0.0, halo_ref[...])
    z = jnp.concatenate([halo, u], axis=0)
    pos = pos0 + lax.broadcasted_iota(jnp.int32, (tm, 1), 0)
    gw = u.shape[1] // len(POOL_WINDOWS)
    d_groups = [_pool_diff(z[:, gi * gw:(gi + 1) * gw], u[:, gi * gw:(gi + 1) * gw], pos, w)
                for gi, w in enumerate(POOL_WINDOWS)]
    y_ref[...] = _mix_tail(d_groups, gp_ref[...], ga_ref[...], att_ref[...], ma_ref[...], mb_ref[...], x_ref[...],
                           wpool_ref, ps_ref, wup_ref, wua_ref, wout_ref)


def _out_sample_kernel(z_ref, gp_ref, ga_ref, ma_ref, mb_ref, att_ref, x_ref,
                       wpool_ref, ps_ref, wup_ref, wua_ref, wout_ref, y_ref, *, dec_b, n_new, pos0):
    gw = z_ref.shape[2] // len(POOL_WINDOWS)
    pos = pos0 + lax.broadcasted_iota(jnp.int32, (n_new, 1), 0)
    per_seq = []
    for b in range(dec_b):
        z = z_ref[b]
        u = z[HALO:, :]
        per_seq.append([_pool_diff(z[:, gi * gw:(gi + 1) * gw], u[:, gi * gw:(gi + 1) * gw], pos, w)
                        for gi, w in enumerate(POOL_WINDOWS)])
    d_groups = [jnp.concatenate([per_seq[b][gi] for b in range(dec_b)], axis=0) for gi in range(len(POOL_WINDOWS))]
    y_ref[...] = _mix_tail(d_groups, gp_ref[...], ga_ref[...], att_ref[...], ma_ref[...], mb_ref[...], x_ref[...],
                           wpool_ref, ps_ref, wup_ref, wua_ref, wout_ref)


def _weight_specs(pw, gw, dm):
    one = pl.Buffered(1)
    return [
        pl.BlockSpec((len(POOL_WINDOWS), gw, gw), lambda i: (0, 0, 0), pipeline_mode=one),
        pl.BlockSpec((1, pw), lambda i: (0, 0), pipeline_mode=one),
        pl.BlockSpec((pw, dm), lambda i: (0, 0), pipeline_mode=one),
        pl.BlockSpec((pw, dm), lambda i: (0, 0), pipeline_mode=one),
        pl.BlockSpec((dm, dm), lambda i: (0, 0), pipeline_mode=one),
    ]


def _out_prompt(h, att, x2d, wpool, ps, wup, wua, wout, *, seq, tm, cols):
    m, dm = x2d.shape
    pw = att.shape[1]
    gw = pw // len(POOL_WINDOWS)
    hb = tm // HALO
    kern = functools.partial(_out_prompt_kernel, tm=tm, seq=seq)
    return pl.pallas_call(
        kern,
        grid=(m // tm,),
        in_specs=[
            pl.BlockSpec((tm, pw), lambda i: (i, cols["u"] // pw)),
            pl.BlockSpec((HALO, pw), lambda i: (jnp.maximum(i * hb - 1, 0), cols["u"] // pw)),
            pl.BlockSpec((tm, pw), lambda i: (i, cols["gp"] // pw)),
            pl.BlockSpec((tm, pw), lambda i: (i, cols["ga"] // pw)),
            pl.BlockSpec((tm, dm), lambda i: (i, cols["ma"] // dm)),
            pl.BlockSpec((tm, dm), lambda i: (i, cols["mb"] // dm)),
            pl.BlockSpec((tm, pw), lambda i: (i, 0)),
            pl.BlockSpec((tm, dm), lambda i: (i, 0)),
        ] + _weight_specs(pw, gw, dm),
        out_specs=pl.BlockSpec((tm, dm), lambda i: (i, 0)),
        out_shape=jax.ShapeDtypeStruct((m, dm), F32),
        compiler_params=_cparams(1),
        name="out_prompt",
    )(h, h, h, h, h, h, att, x2d, wpool, ps, wup, wua, wout)


def _out_sample(z, h, att, x2d, wpool, ps, wup, wua, wout, *, dec_b, n_new, pos0, cols):
    m, dm = x2d.shape
    pw = att.shape[1]
    gw = pw // len(POOL_WINDOWS)
    kern = functools.partial(_out_sample_kernel, dec_b=dec_b, n_new=n_new, pos0=pos0)
    return pl.pallas_call(
        kern,
        grid=(1,),
        in_specs=[
            pl.BlockSpec((dec_b, HALO + n_new, pw), lambda i: (0, 0, 0)),
            pl.BlockSpec((m, pw), lambda i: (0, cols["gp"] // pw)),
            pl.BlockSpec((m, pw), lambda i: (0, cols["ga"] // pw)),
            pl.BlockSpec((m, dm), lambda i: (0, cols["ma"] // dm)),
            pl.BlockSpec((m, dm), lambda i: (0, cols["mb"] // dm)),
            pl.BlockSpec((m, pw), lambda i: (0, 0)),
            pl.BlockSpec((m, dm), lambda i: (0, 0)),
        ] + _weight_specs(pw, gw, dm),
        out_specs=pl.BlockSpec((m, dm), lambda i: (0, 0)),
        out_shape=jax.ShapeDtypeStruct((m, dm), F32),
        compiler_params=_cparams(1),
        name="out_sample",
    )(z, h, h, h, h, att, x2d, wpool, ps, wup, wua, wout)


def kernel(x_prompt, x_sample, cache_k, cache_v, cache_logf, state_pool, page_table, norm_gain, w_in, b_f,
           q_norm_gain, k_norm_gain, w_pool_map, pool_scale, w_up_pool, w_up_attn, w_out):
    batch, seq, dm = x_prompt.shape
    dec_b, n_new, _ = x_sample.shape
    assert w_in.shape[0] == 1
    n_heads = b_f.shape[1]
    aw = n_heads * HEAD_DIM
    pw = w_up_pool.shape[1]
    n_pages = page_table.shape[1]
    past_len = n_pages * PAGE_SIZE
    assert pw == aw and dm == 2 * pw and n_pages % PAGES_PER_STEP == 0

    tn = 512
    plan = _inproj_plan(pw, aw, dm, n_heads, tn)
    cols = plan["cols"]
    w_t = w_in[0].T
    fl0, fl_n = plan["fl_rows"]
    wfl = jnp.pad(w_t[fl0:fl0 + fl_n], ((0, V7X_LANES - n_heads), (0, 0))).astype(BF16)
    bfp = jnp.pad(b_f, ((0, 0), (0, V7X_LANES - n_heads)))
    reps = tn // HEAD_DIM
    qg = jnp.tile(q_norm_gain[0], reps)[None, :]
    kg = jnp.tile(k_norm_gain[0], reps)[None, :]
    qgc = jnp.broadcast_to(jnp.tile(q_norm_gain[0], n_heads)[:, None], (aw, V7X_LANES))
    kgc = jnp.broadcast_to(jnp.tile(k_norm_gain[0], n_heads)[:, None], (aw, V7X_LANES))
    seg = np.arange(tn) // HEAD_DIM
    bd = jnp.asarray((seg[:, None] == seg[None, :]).astype(np.float32)).astype(BF16)
    wpool = w_pool_map[0].astype(BF16)
    wup = w_up_pool[0].astype(BF16)
    wua = w_up_attn[0].astype(BF16)
    wout = w_out[0].astype(BF16)

    ms = dec_b * n_new
    xs2 = x_sample.reshape(ms, dm)
    hs, lfs, wbf = _inproj_new(plan, xs2, norm_gain, w_t, wfl, bfp, qg, kg, bd)

    cache_kt = cache_k.transpose(0, 1, 3, 4, 2)
    cache_vt = cache_v.transpose(0, 1, 3, 4, 2)
    cache_lft = cache_logf.transpose(0, 1, 3, 2)
    decode_args = dict(n_heads=n_heads, n_new=n_new, n_pg=DECODE_PAGES_FUSED)

    xp2 = x_prompt.reshape(batch * seq, dm)
    xn_p, lfp = _prenorm(xp2, norm_gain, wfl, bfp, tm=512)
    hp, att_rows_a, n_fused = _h_decode(plan, page_table, xn_p, wbf, hs, lfs, cache_kt, cache_vt, cache_lft,
                                        tm=1024, **decode_args)
    qt_p, kt_p, vt_p = _inproj_fm(plan, xn_p, wbf, qgc, kgc, tm=1024, seq=seq, width=aw)
    fcol, ft, lft = _fcum(lfp, batch=batch, seq=seq, n_heads=n_heads)
    ft_pairs = ft.reshape(batch, n_heads // 2, 2, seq)
    att_p = _prompt_attention(qt_p, kt_p, vt_p, fcol, ft_pairs, batch=batch, seq=seq, n_heads=n_heads, tq=256,
                              n_pair=ATTN_PAIRS_PER_STEP)
    yp = _out_prompt(hp, att_p, xp2, wpool, pool_scale, wup, wua, wout, seq=seq, tm=256, cols=cols)

    k_s = hs[:, cols["k"]:cols["k"] + aw].reshape(dec_b, n_new, n_heads, HEAD_DIM)
    v_s = hs[:, cols["v"]:cols["v"] + aw].reshape(dec_b, n_new, n_heads, HEAD_DIM)
    u_s = hs[:, cols["u"]:cols["u"] + pw].reshape(dec_b, n_new, pw)
    logf_s = lfs[:, :n_heads].reshape(dec_b, n_new, n_heads)
    att_rows = att_rows_a
    if n_fused < dec_b:
        att_rows_b = _decode_attention(page_table, hs, lfs, cache_kt, cache_vt, cache_lft, n_heads=n_heads,
                                       n_new=n_new, n_pg=PAGES_PER_STEP, cols=cols, seq0=n_fused,
                                       n_seq=dec_b - n_fused)
        att_rows = jnp.concatenate([att_rows_a, att_rows_b], axis=0)
    att_s = att_rows.reshape(dec_b, n_heads, n_new, HEAD_DIM).transpose(0, 2, 1, 3).reshape(ms, aw)
    z_s = jnp.concatenate([jnp.zeros((dec_b, HALO - POOL_BUF, pw), F32), state_pool[0], u_s], axis=1)
    ys = _out_sample(z_s, hs, att_s, xs2, wpool, pool_scale, wup, wua, wout,
                     dec_b=dec_b, n_new=n_new, pos0=past_len, cols=cols)

    k_p = kt_p.reshape(1, batch, n_heads, HEAD_DIM, seq).transpose(0, 1, 4, 2, 3)
    v_p = vt_p.reshape(1, batch, n_heads, HEAD_DIM, seq).transpose(0, 1, 4, 2, 3)
    logf_p = lft.transpose(0, 2, 1)[None]
    pool_p = hp.reshape(batch, seq, hp.shape[1])[:, seq - POOL_BUF:, cols["u"]:cols["u"] + pw][None]
    pool_s = z_s[:, HALO + n_new - POOL_BUF:, :][None]
    return (yp.reshape(batch, seq, dm), ys.reshape(dec_b, n_new, dm), k_p, v_p, logf_p, pool_p,
            k_s[None], v_s[None], logf_s[None], pool_s)
```

```python
import functools

import jax
import jax.numpy as jnp
import numpy as np
from jax import lax
from jax.experimental import pallas as pl
from jax.experimental.pallas import tpu as pltpu

F32 = jnp.float32
BF16 = jnp.bfloat16

HEAD_DIM = 64
POOL_WINDOWS = (2, 4, 8, 16)
POOL_BUF = 15
PAGE_SIZE = 128
EPS = 1e-6
NEG_INF = -1e30
ATTN_SCALE = HEAD_DIM ** -0.5
LOG2E = 1.4426950408889634

V7X_LANES = 128
V7X_SUBLANES = 8
V7X_VMEM_LIMIT_BYTES = 56 * 1024 * 1024

ROW_ALIGN = 16
HALO = 16
PAGES_PER_STEP = 16
DECODE_PAGES_FUSED = 8
N_AUG = 3
ATTN_PAIRS_PER_STEP = 8


def _cparams(n_grid_axes):
    return pltpu.CompilerParams(
        dimension_semantics=("arbitrary",) * n_grid_axes,
        vmem_limit_bytes=V7X_VMEM_LIMIT_BYTES,
    )


def _split3(x):
    hi = x.astype(BF16)
    r1 = x - hi.astype(F32)
    mid = r1.astype(BF16)
    lo = (r1 - mid.astype(F32)).astype(BF16)
    return hi, mid, lo


def _dot(a, b):
    return jnp.dot(a, b, preferred_element_type=F32)


def _dot_nt(a, b):
    return lax.dot_general(a, b, (((1,), (1,)), ((), ())), preferred_element_type=F32)


def _sigmoid(x):
    return 1.0 / (1.0 + jnp.exp(-x))


def _silu(x):
    return x * _sigmoid(x)


def _scan_lanes(x, *, reverse):
    n = x.shape[-1]
    ax = x.ndim - 1
    lane = lax.broadcasted_iota(jnp.int32, x.shape, ax)
    k = 1
    while k < n:
        if reverse:
            shifted = pltpu.roll(x, n - k, ax)
            x = x + jnp.where(lane < n - k, shifted, 0.0)
        else:
            shifted = pltpu.roll(x, k, ax)
            x = x + jnp.where(lane >= k, shifted, 0.0)
        k *= 2
    return x


def _scan_rows(x):
    n = x.shape[0]
    row = lax.broadcasted_iota(jnp.int32, x.shape, 0)
    k = 1
    while k < n:
        x = x + jnp.where(row >= k, pltpu.roll(x, k, 0), 0.0)
        k *= 2
    return x


def _norm_and_logf(x_ref, ng_ref, wfl_ref, bf_ref, xn_sc, lf_ref):
    x = x_ref[...]
    ms = jnp.mean(x * x, axis=-1, keepdims=True)
    xn = x * lax.rsqrt(ms + EPS) * ng_ref[...]
    xn_sc[...] = xn.astype(BF16)
    z = _dot_nt(xn_sc[...], wfl_ref[...]) + bf_ref[...]
    lf_ref[...] = jnp.minimum(z, 0.0) - jnp.log1p(jnp.exp(-jnp.abs(z)))


def _step_in(j, rng):
    return jnp.logical_and(j >= rng[0], j < rng[1])


def _inproj_new_kernel(off_ref, x_ref, ng_ref, w_ref, wfl_ref, bf_ref, qg_ref, kg_ref, bd_ref,
                       h_ref, lf_ref, wbf_ref, xn_sc, *, steps):
    del off_ref
    j = pl.program_id(0)

    @pl.when(j == 0)
    def _():
        _norm_and_logf(x_ref, ng_ref, wfl_ref, bf_ref, xn_sc, lf_ref)

    w = w_ref[...].astype(BF16)
    wbf_ref[...] = w
    acc = _dot_nt(xn_sc[...], w)

    def head_norm(gain):
        sq = acc * acc
        hi = sq.astype(BF16)
        lo = (sq - hi.astype(F32)).astype(BF16)
        ss = _dot(hi, bd_ref[...]) + _dot(lo, bd_ref[...])
        return acc * lax.rsqrt(ss * (1.0 / HEAD_DIM) + EPS) * gain

    is_q = _step_in(j, steps["q"])
    is_k = _step_in(j, steps["k"])

    @pl.when(is_q)
    def _():
        h_ref[...] = head_norm(qg_ref[...])

    @pl.when(is_k)
    def _():
        h_ref[...] = head_norm(kg_ref[...])

    @pl.when(jnp.logical_not(is_q | is_k))
    def _():
        h_ref[...] = acc


def _inproj_new(plan, x2d, ng, w_t, wfl, bfp, qg, kg, bd):
    m, d = x2d.shape
    tn = plan["tn"]
    nj = plan["n_steps"]
    kern = functools.partial(_inproj_new_kernel, steps=plan["steps"])
    grid_spec = pltpu.PrefetchScalarGridSpec(
        num_scalar_prefetch=1,
        grid=(nj,),
        in_specs=[
            pl.BlockSpec((m, d), lambda j, off: (0, 0)),
            pl.BlockSpec((1, d), lambda j, off: (0, 0)),
            pl.BlockSpec((pl.Element(tn), pl.Element(d)), lambda j, off: (off[j] * ROW_ALIGN, 0)),
            pl.BlockSpec((V7X_LANES, d), lambda j, off: (0, 0)),
            pl.BlockSpec((1, V7X_LANES), lambda j, off: (0, 0)),
            pl.BlockSpec((1, tn), lambda j, off: (0, 0)),
            pl.BlockSpec((1, tn), lambda j, off: (0, 0)),
            pl.BlockSpec((tn, tn), lambda j, off: (0, 0)),
        ],
        out_specs=[
            pl.BlockSpec((m, tn), lambda j, off: (0, j)),
            pl.BlockSpec((m, V7X_LANES), lambda j, off: (0, 0)),
            pl.BlockSpec((tn, d), lambda j, off: (j, 0)),
        ],
        scratch_shapes=[pltpu.VMEM((m, d), BF16)],
    )
    return pl.pallas_call(
        kern,
        grid_spec=grid_spec,
        out_shape=[
            jax.ShapeDtypeStruct((m, nj * tn), F32),
            jax.ShapeDtypeStruct((m, V7X_LANES), F32),
            jax.ShapeDtypeStruct((nj * tn, d), BF16),
        ],
        compiler_params=_cparams(1),
        name="inproj_new",
    )(plan["row_offsets"], x2d, ng, w_t, wfl, bfp, qg, kg, bd)


def _prenorm_kernel(x_ref, ng_ref, wfl_ref, bf_ref, xn_ref, lf_ref):
    _norm_and_logf(x_ref, ng_ref, wfl_ref, bf_ref, xn_ref, lf_ref)


def _prenorm(x2d, ng, wfl, bfp, *, tm):
    m, d = x2d.shape
    return pl.pallas_call(
        _prenorm_kernel,
        grid=(m // tm,),
        in_specs=[
            pl.BlockSpec((tm, d), lambda i: (i, 0)),
            pl.BlockSpec((1, d), lambda i: (0, 0)),
            pl.BlockSpec((V7X_LANES, d), lambda i: (0, 0)),
            pl.BlockSpec((1, V7X_LANES), lambda i: (0, 0)),
        ],
        out_specs=[
            pl.BlockSpec((tm, d), lambda i: (i, 0)),
            pl.BlockSpec((tm, V7X_LANES), lambda i: (i, 0)),
        ],
        out_shape=[
            jax.ShapeDtypeStruct((m, d), BF16),
            jax.ShapeDtypeStruct((m, V7X_LANES), F32),
        ],
        compiler_params=_cparams(1),
        name="prenorm",
    )(x2d, ng, wfl, bfp)


def _inproj_fm_kernel(xn_ref, w_ref, qgc_ref, kgc_ref, qt_ref, kt_ref, vt_ref):
    j = pl.program_id(1)

    def head_norm(acc, gain_col_ref):
        tn, tm = acc.shape
        a3 = acc.reshape(tn // HEAD_DIM, HEAD_DIM, tm)
        ms = jnp.mean(a3 * a3, axis=1, keepdims=True)
        g3 = gain_col_ref[...].reshape(tn // HEAD_DIM, HEAD_DIM, V7X_LANES)[:, :, 0:1]
        return (a3 * lax.rsqrt(ms + EPS) * g3).reshape(tn, tm)

    @pl.when(j == 0)
    def _():
        qt_ref[...] = head_norm(_dot_nt(w_ref[...], xn_ref[...]), qgc_ref)

    @pl.when(j == 1)
    def _():
        kt_ref[...] = head_norm(_dot_nt(w_ref[...], xn_ref[...]), kgc_ref)

    @pl.when(j == 2)
    def _():
        vt_ref[...] = _dot_nt(w_ref[...], xn_ref[...])


def _inproj_fm(plan, xn, wbf, qgc, kgc, *, tm, seq, width):
    m, d = xn.shape
    cols = plan["cols"]
    blk0 = cols["q"] // width
    assert cols["q"] % width == 0 and cols["k"] == cols["q"] + width and cols["v"] == cols["k"] + width
    nb = seq // tm
    out_spec = pl.BlockSpec((None, width, tm), lambda i, j: (i // nb, 0, i % nb))
    return pl.pallas_call(
        _inproj_fm_kernel,
        grid=(m // tm, 3),
        in_specs=[
            pl.BlockSpec((tm, d), lambda i, j: (i, 0)),
            pl.BlockSpec((width, d), lambda i, j: (blk0 + j, 0)),
            pl.BlockSpec((width, V7X_LANES), lambda i, j: (0, 0)),
            pl.BlockSpec((width, V7X_LANES), lambda i, j: (0, 0)),
        ],
        out_specs=[out_spec] * 3,
        out_shape=[jax.ShapeDtypeStruct((m // seq, width, seq), F32)] * 3,
        compiler_params=_cparams(2),
        name="inproj_fm",
    )(xn, wbf, qgc, kgc)


def _inproj_plan(pw, aw, dm, n_heads, tn):
    src, off = {}, 0
    for name, size in (("u", pw), ("gp", pw), ("q", aw), ("k", aw), ("v", aw), ("ga", aw), ("fl", n_heads),
                       ("ma", dm), ("mb", dm)):
        src[name] = (off, size)
        off += size
    order = ("ma", "mb", "gp", "ga", "u", "q", "k", "v")
    row_offsets, steps, cols = [], {}, {}
    for name in order:
        base, size = src[name]
        steps[name] = (len(row_offsets), len(row_offsets) + size // tn)
        cols[name] = len(row_offsets) * tn
        row_offsets += [base + o for o in range(0, size, tn)]
    assert all(r % ROW_ALIGN == 0 for r in row_offsets)
    return {"tn": tn, "steps": steps, "n_steps": len(row_offsets), "cols": cols, "fl_rows": src["fl"],
            "row_offsets": jnp.asarray([r // ROW_ALIGN for r in row_offsets], jnp.int32)}


def _fcum_kernel(lf_ref, fcol_ref, ft_ref, lft_ref, *, seq, n_heads):
    c = V7X_LANES
    row = lax.broadcasted_iota(jnp.int32, (c, c), 0)
    col = lax.broadcasted_iota(jnp.int32, (c, c), 1)
    tri = jnp.where(col <= row, 1.0, 0.0).astype(BF16)
    carry = jnp.zeros((1, c), F32)
    for ci in range(seq // c):
        x = lf_ref[ci * c:(ci + 1) * c, :]
        hi, mid, lo = _split3(x)
        fc = (_dot(tri, hi) + _dot(tri, mid)) + _dot(tri, lo) + carry
        fcol_ref[ci * c:(ci + 1) * c, :] = fc
        ft_ref[:, ci * c:(ci + 1) * c] = fc.T[:n_heads, :]
        lft_ref[:, ci * c:(ci + 1) * c] = x.T[:n_heads, :]
        carry = fc[c - 1:c, :]


def _fcum(lf2d, *, batch, seq, n_heads):
    kern = functools.partial(_fcum_kernel, seq=seq, n_heads=n_heads)
    return pl.pallas_call(
        kern,
        grid=(batch,),
        in_specs=[pl.BlockSpec((seq, V7X_LANES), lambda b: (b, 0))],
        out_specs=[
            pl.BlockSpec((seq, V7X_LANES), lambda b: (b, 0)),
            pl.BlockSpec((None, n_heads, seq), lambda b: (b, 0, 0)),
            pl.BlockSpec((None, n_heads, seq), lambda b: (b, 0, 0)),
        ],
        out_shape=[
            jax.ShapeDtypeStruct((batch * seq, V7X_LANES), F32),
            jax.ShapeDtypeStruct((batch, n_heads, seq), F32),
            jax.ShapeDtypeStruct((batch, n_heads, seq), F32),
        ],
        compiler_params=_cparams(1),
        name="forget_cumsum",
    )(lf2d)


def _attn_kernel(qt_ref, kt_ref, vt_ref, fcol_ref, ft_ref, o_ref, ka_sc, *, tq, seq, n_pair):
    g = pl.program_id(1)
    i = pl.program_id(2)
    d = HEAD_DIM
    lanes = V7X_LANES
    lane1 = lax.broadcasted_iota(jnp.int32, (1, lanes), 1)

    @pl.when(i == 0)
    def _():
        prow = lax.broadcasted_iota(jnp.int32, (N_AUG * lanes, lanes), 0)
        lane = lax.broadcasted_iota(jnp.int32, (N_AUG * lanes, lanes), 1)
        piece, hrow = prow // lanes, prow % lanes
        ones = jnp.where(((lane1 >= d) & (lane1 < d + N_AUG)) | (lane1 < N_AUG), 1.0, 0.0)
        sels = []
        for pr in range(n_pair):
            head0 = 2 * (g * n_pair + pr)
            hit = ((hrow == head0) & (lane == d + N_AUG + piece)) | ((hrow == head0 + 1) & (lane == N_AUG + piece))
            sels.append(jnp.where(hit, -1.0, 0.0).astype(BF16))
        ck = 256
        for c in range(seq // ck):
            pieces = jnp.concatenate(_split3(fcol_ref[c * ck:(c + 1) * ck, :] * LOG2E), axis=1)
            for pr in range(n_pair):
                k_rows = kt_ref[pr * lanes:(pr + 1) * lanes, c * ck:(c + 1) * ck].T
                aug = ones + _dot(pieces, sels[pr])
                ka_sc[2 * pr, c * ck:(c + 1) * ck, :] = jnp.where(lane1 < d, k_rows, aug).astype(BF16)
                ka_sc[2 * pr + 1, c * ck:(c + 1) * ck, :] = jnp.where(lane1 >= d, k_rows, aug).astype(BF16)

    qs = pl.multiple_of(i * tq, tq)
    rowi = lax.broadcasted_iota(jnp.int32, (d, tq), 0)

    def aug_rows(f_row):
        hi, mid, lo = (x.astype(F32) for x in _split3(f_row))
        return jnp.where(rowi == 0, hi, jnp.where(rowi == 1, mid, jnp.where(rowi == 2, lo,
                         jnp.where(rowi < 2 * N_AUG, 1.0, 0.0))))

    qa = []
    for pr in range(n_pair):
        qt = qt_ref[pr * lanes:(pr + 1) * lanes, :] * (ATTN_SCALE * LOG2E)
        fq = ft_ref[pr, :, pl.ds(qs, tq)] * LOG2E
        qa.append(jnp.concatenate([qt[:d, :], aug_rows(fq[0:1, :])], axis=0).astype(BF16))
        qa.append(jnp.concatenate([aug_rows(fq[1:2, :]), qt[d:, :]], axis=0).astype(BF16))
    krow = lax.broadcasted_iota(jnp.int32, (tq, tq), 0)
    qcol = lax.broadcasted_iota(jnp.int32, (tq, tq), 1)
    causal = krow <= qcol

    def step(j, carry, masked):
        ks = pl.multiple_of(j * tq, tq)
        heads = range(2 * n_pair)
        scores = [_dot(ka_sc[hh, pl.ds(ks, tq), :], qa[hh]) for hh in heads]
        stats, probs = [], []
        for hh in heads:
            m, l, _ = carry[hh]
            s = jnp.where(causal, scores[hh], NEG_INF) if masked else scores[hh]
            m_new = jnp.maximum(m, jnp.max(s, axis=0, keepdims=True))
            alpha = jnp.exp2(m - m_new)
            pt = jnp.exp2(s - m_new)
            stats.append((m_new, alpha * l + jnp.sum(pt, axis=0, keepdims=True), alpha))
            probs.append(pt.astype(BF16))
        pvs = [_dot(vt_ref[hh * d:(hh + 1) * d, pl.ds(ks, tq)].astype(BF16), probs[hh]) for hh in heads]
        return tuple((stats[hh][0], stats[hh][1], stats[hh][2] * carry[hh][2] + pvs[hh]) for hh in heads)

    init = tuple((jnp.full((1, tq), NEG_INF, F32), jnp.zeros((1, tq), F32), jnp.zeros((d, tq), F32))
                 for _ in range(2 * n_pair))
    carry = lax.fori_loop(0, i, lambda j, c: step(j, c, False), init)
    final = step(i, carry, True)
    o_ref[...] = jnp.concatenate([acc / l for (_, l, acc) in final], axis=0).T


def _prompt_attention(qt, kt, vt, fcol, ft_pairs, *, batch, seq, n_heads, tq, n_pair):
    n_groups = n_heads // (2 * n_pair)
    qn = seq // tq
    rows = n_pair * V7X_LANES
    kern = functools.partial(_attn_kernel, tq=tq, seq=seq, n_pair=n_pair)
    return pl.pallas_call(
        kern,
        grid=(batch, n_groups, qn),
        in_specs=[
            pl.BlockSpec((None, rows, tq), lambda b, g, i: (b, g, i)),
            pl.BlockSpec((None, rows, seq), lambda b, g, i: (b, g, 0)),
            pl.BlockSpec((None, rows, seq), lambda b, g, i: (b, g, 0)),
            pl.BlockSpec((seq, V7X_LANES), lambda b, g, i: (b, 0)),
            pl.BlockSpec((None, n_pair, 2, seq), lambda b, g, i: (b, g, 0, 0)),
        ],
        out_specs=pl.BlockSpec((tq, rows), lambda b, g, i: (b * qn + i, g)),
        out_shape=jax.ShapeDtypeStruct((batch * seq, n_heads * HEAD_DIM), F32),
        scratch_shapes=[pltpu.VMEM((2 * n_pair, seq, V7X_LANES), BF16)],
        compiler_params=_cparams(3),
        name="prompt_attention",
    )(qt, kt, vt, fcol, ft_pairs)


def _decode_step(cj, n_chunks, q_ref, kn_ref, vn_ref, lfn_ref, kc_refs, vc_refs, lf_refs, o_ref, scratch,
                 *, n_heads, n_new, between=None):
    m_sc, l_sc, acc_sc, carry_sc, qbd_sc, qb_sc = scratch
    rows = n_heads * n_new
    feat = n_heads * HEAD_DIM

    def tile_rows(x):
        return jnp.concatenate([x] * n_heads, axis=0)

    def rep_rows(x):
        return jnp.concatenate([jnp.broadcast_to(x[h:h + 1, :], (n_new, x.shape[1])) for h in range(n_heads)], axis=0)

    def to_pages(x):
        xp = jnp.concatenate([x, jnp.zeros((PAGE_SIZE - n_new, feat), F32)], axis=0)
        return jnp.concatenate([xp[:, c * PAGE_SIZE:(c + 1) * PAGE_SIZE].T for c in range(feat // PAGE_SIZE)], axis=0)

    def score_phase(k_pages, bias):
        s = _dot(qbd_sc[...], k_pages) + bias
        m_prev = m_sc[...]
        m_new = jnp.maximum(m_prev, jnp.max(s, axis=1, keepdims=True))
        alpha = jnp.exp(m_prev - m_new)
        p = jnp.exp(s - m_new)
        l_sc[...] = alpha * l_sc[...] + jnp.sum(p, axis=1, keepdims=True)
        m_sc[...] = m_new
        return p, alpha

    def value_phase(v_pages, p, alpha):
        acc_sc[...] = alpha * acc_sc[...] + _dot_nt(p.astype(BF16), v_pages)

    @pl.when(cj == 0)
    def _():
        m_sc[...] = jnp.full(m_sc.shape, NEG_INF, F32)
        l_sc[...] = jnp.zeros(l_sc.shape, F32)
        acc_sc[...] = jnp.zeros(acc_sc.shape, F32)
        carry_sc[...] = jnp.zeros(carry_sc.shape, F32)
        row_head = lax.broadcasted_iota(jnp.int32, (rows, feat), 0) // n_new
        col_head = lax.broadcasted_iota(jnp.int32, (rows, feat), 1) // HEAD_DIM
        qbd_sc[...] = jnp.where(row_head == col_head, tile_rows(q_ref[...] * ATTN_SCALE), 0.0).astype(BF16)
        f_new = _scan_rows(lfn_ref[...])
        rh = lax.broadcasted_iota(jnp.int32, (rows, V7X_LANES), 0) // n_new
        ln = lax.broadcasted_iota(jnp.int32, (rows, V7X_LANES), 1)
        qb = jnp.sum(jnp.where(rh == ln, tile_rows(f_new), 0.0), axis=1, keepdims=True)
        qb_sc[...] = qb
        f_pad = jnp.concatenate([f_new, jnp.zeros((PAGE_SIZE - n_new, V7X_LANES), F32)], axis=0)
        f_new_t = f_pad.T[:n_heads, :]
        tok = lax.broadcasted_iota(jnp.int32, (rows, PAGE_SIZE), 0) % n_new
        pos = lax.broadcasted_iota(jnp.int32, (rows, PAGE_SIZE), 1)
        bias = (qb - rep_rows(f_new_t)) + jnp.where(pos <= tok, 0.0, NEG_INF)
        p, alpha = score_phase(to_pages(kn_ref[...]).astype(BF16), bias)
        value_phase(to_pages(vn_ref[...]).astype(BF16), p, alpha)

    carry = carry_sc[...]
    biases = []
    for lf_ref in lf_refs:
        lf = lf_ref[...]
        incl = _scan_lanes(lf, reverse=True)
        biases.append(rep_rows((incl - lf) + carry))
        carry = carry + incl[:, 0:1]
    carry_sc[...] = carry
    k_pages = jnp.concatenate([r[...].reshape(feat, PAGE_SIZE).astype(BF16) for r in kc_refs], axis=1)
    p, alpha = score_phase(k_pages, qb_sc[...] + jnp.concatenate(biases, axis=1))
    if between is not None:
        between()
    v_pages = jnp.concatenate([r[...].reshape(feat, PAGE_SIZE).astype(BF16) for r in vc_refs], axis=1)
    value_phase(v_pages, p, alpha)

    @pl.when(cj == n_chunks - 1)
    def _():
        outs = [acc_sc[h * n_new:(h + 1) * n_new, h * HEAD_DIM:(h + 1) * HEAD_DIM] for h in range(n_heads)]
        o_ref[...] = jnp.concatenate(outs, axis=0) / l_sc[...]


def _decode_scratch(n_heads, n_new):
    rows = n_heads * n_new
    feat = n_heads * HEAD_DIM
    return [
        pltpu.VMEM((rows, 1), F32),
        pltpu.VMEM((rows, 1), F32),
        pltpu.VMEM((rows, feat), F32),
        pltpu.VMEM((n_heads, PAGE_SIZE), F32),
        pltpu.VMEM((rows, feat), BF16),
        pltpu.VMEM((rows, 1), F32),
    ]


def _decode_in_specs(seq_of, chunk_of, *, n_heads, n_new, n_pg, n_pages, cols, n_lead):
    feat = n_heads * HEAD_DIM
    last = n_pages - 1

    def tok_spec(width, col):
        return pl.BlockSpec((n_new, width), lambda *a: (seq_of(*a[:n_lead]), col))

    def page_spec(g, tail):
        def idx(*a):
            pt = a[n_lead]
            page = pt[seq_of(*a[:n_lead]), last - (chunk_of(*a[:n_lead]) * n_pg + g)]
            return (0, page) + (0,) * (1 + len(tail))
        return pl.BlockSpec((None, None, n_heads) + tail, idx)

    kv_tail = (HEAD_DIM, PAGE_SIZE)
    specs = [tok_spec(feat, cols["q"] // feat), tok_spec(feat, cols["k"] // feat), tok_spec(feat, cols["v"] // feat),
             tok_spec(V7X_LANES, 0)]
    specs += [page_spec(g, kv_tail) for g in range(n_pg)]
    specs += [page_spec(g, kv_tail) for g in range(n_pg)]
    specs += [page_spec(g, (PAGE_SIZE,)) for g in range(n_pg)]
    return specs


def _decode_kernel(pt_ref, q_ref, kn_ref, vn_ref, lfn_ref, *rest, n_heads, n_new, n_pg):
    del pt_ref
    kc_refs = rest[0:n_pg]
    vc_refs = rest[n_pg:2 * n_pg]
    lf_refs = rest[2 * n_pg:3 * n_pg]
    o_ref = rest[3 * n_pg]
    _decode_step(pl.program_id(1), pl.num_programs(1), q_ref, kn_ref, vn_ref, lfn_ref, kc_refs, vc_refs, lf_refs,
                 o_ref, rest[3 * n_pg + 1:], n_heads=n_heads, n_new=n_new)


def _decode_attention(page_table, hs, lf_new, cache_kt, cache_vt, cache_lft, *, n_heads, n_new, n_pg, cols,
                      seq0, n_seq):
    n_pages = page_table.shape[1]
    rows = n_heads * n_new
    kern = functools.partial(_decode_kernel, n_heads=n_heads, n_new=n_new, n_pg=n_pg)
    grid_spec = pltpu.PrefetchScalarGridSpec(
        num_scalar_prefetch=1,
        grid=(n_seq, n_pages // n_pg),
        in_specs=_decode_in_specs(lambda b, j: b + seq0, lambda b, j: j, n_heads=n_heads, n_new=n_new, n_pg=n_pg,
                                  n_pages=n_pages, cols=cols, n_lead=2),
        out_specs=pl.BlockSpec((None, rows, HEAD_DIM), lambda b, j, pt: (b, 0, 0)),
        scratch_shapes=_decode_scratch(n_heads, n_new),
    )
    args = [page_table, hs, hs, hs, lf_new] + [cache_kt] * n_pg + [cache_vt] * n_pg + [cache_lft] * n_pg
    return pl.pallas_call(
        kern,
        grid_spec=grid_spec,
        out_shape=jax.ShapeDtypeStruct((n_seq, rows, HEAD_DIM), F32),
        compiler_params=_cparams(2),
        name="decode_attention",
    )(*args)


def _h_decode_kernel(pt_ref, xn_ref, w_ref, q_ref, kn_ref, vn_ref, lfn_ref, *rest, n_heads, n_new, n_pg, n_j,
                     n_gate, n_chunks):
    del pt_ref
    kc_refs = rest[0:n_pg]
    vc_refs = rest[n_pg:2 * n_pg]
    lf_refs = rest[2 * n_pg:3 * n_pg]
    hg_ref, u_ref, o_ref = rest[3 * n_pg:3 * n_pg + 3]
    j = pl.program_id(1)
    cj = (pl.program_id(0) * n_j + j) % n_chunks

    def step(store):
        _decode_step(cj, n_chunks, q_ref, kn_ref, vn_ref, lfn_ref, kc_refs, vc_refs, lf_refs, o_ref,
                     rest[3 * n_pg + 3:], n_heads=n_heads, n_new=n_new,
                     between=lambda: store(_dot_nt(xn_ref[...], w_ref[...])))

    def store_gate(acc):
        hg_ref[...] = acc.astype(BF16)

    def store_u(acc):
        u_ref[...] = acc

    pl.when(j < n_gate)(lambda: step(store_gate))
    pl.when(j >= n_gate)(lambda: step(store_u))


def _h_decode(plan, page_table, xn, wbf, hs, lf_new, cache_kt, cache_vt, cache_lft, *, tm, n_heads, n_new, n_pg):
    m, d = xn.shape
    tn = plan["tn"]
    cols = plan["cols"]
    n_j = plan["steps"]["q"][0]
    n_pages = page_table.shape[1]
    n_chunks = n_pages // n_pg
    n_steps = (m // tm) * n_j
    assert n_steps % n_chunks == 0 and n_steps // n_chunks <= page_table.shape[0]
    n_seq = n_steps // n_chunks
    rows = n_heads * n_new
    n_gate = plan["steps"]["u"][0]
    assert plan["steps"]["u"][1] == n_j
    kern = functools.partial(_h_decode_kernel, n_heads=n_heads, n_new=n_new, n_pg=n_pg, n_j=n_j, n_gate=n_gate,
                             n_chunks=n_chunks)
    in_specs = [
        pl.BlockSpec((tm, d), lambda i, j, pt: (i, 0)),
        pl.BlockSpec((tn, d), lambda i, j, pt: (j, 0)),
    ] + _decode_in_specs(lambda i, j: (i * n_j + j) // n_chunks, lambda i, j: (i * n_j + j) % n_chunks,
                         n_heads=n_heads, n_new=n_new, n_pg=n_pg, n_pages=n_pages, cols=cols, n_lead=2)
    grid_spec = pltpu.PrefetchScalarGridSpec(
        num_scalar_prefetch=1,
        grid=(m // tm, n_j),
        in_specs=in_specs,
        out_specs=[
            pl.BlockSpec((tm, tn), lambda i, j, pt: (i, jnp.minimum(j, n_gate - 1))),
            pl.BlockSpec((tm, tn), lambda i, j, pt: (i, jnp.maximum(j - n_gate, 0))),
            pl.BlockSpec((None, rows, HEAD_DIM), lambda i, j, pt: ((i * n_j + j) // n_chunks, 0, 0)),
        ],
        scratch_shapes=_decode_scratch(n_heads, n_new),
    )
    args = [page_table, xn, wbf, hs, hs, hs, lf_new] + [cache_kt] * n_pg + [cache_vt] * n_pg + [cache_lft] * n_pg
    hg, u, att_rows = pl.pallas_call(
        kern,
        grid_spec=grid_spec,
        out_shape=[
            jax.ShapeDtypeStruct((m, n_gate * tn), BF16),
            jax.ShapeDtypeStruct((m, (n_j - n_gate) * tn), F32),
            jax.ShapeDtypeStruct((n_seq, rows, HEAD_DIM), F32),
        ],
        compiler_params=_cparams(2),
        name="h_decode",
    )(*args)
    return hg, u, att_rows, n_seq


def _pool_diff(z, u, pos, group_w):
    s = z
    k = 1
    while k < group_w:
        s = s + pltpu.roll(s, k, 0)
        k *= 2
    cnt = jnp.minimum(pos + 1, group_w).astype(F32)
    return s[HALO:, :] / cnt - u


def _mix_tail(d_groups, gp, ga, att, ma, mb, x, wpool_ref, ps_ref, wup_ref, wua_ref, wout_ref):
    mixed = [_dot(d.astype(BF16), wpool_ref[gi]) for gi, d in enumerate(d_groups)]
    mixed = jnp.concatenate(mixed, axis=1)
    branch_a = (mixed * ps_ref[...]) * _silu(gp.astype(F32))
    branch_b = att * _silu(ga.astype(F32))
    up_a = _dot(branch_a.astype(BF16), wup_ref[...])
    up_b = _dot(branch_b.astype(BF16), wua_ref[...])
    merged = _sigmoid(ma.astype(F32)) * up_a + _sigmoid(mb.astype(F32)) * up_b
    return x + _dot(merged.astype(BF16), wout_ref[...])


def _out_prompt_kernel(u_ref, halo_ref, gp_ref, ga_ref, ma_ref, mb_ref, att_ref, x_ref,
                       wpool_ref, ps_ref, wup_ref, wua_ref, wout_ref, y_ref, *, tm, seq):
    i = pl.program_id(0)
    pos0 = (i * tm) % seq
    u = u_ref[...]
    halo = jnp.where(pos0 == 0, 0.0, halo_ref[...])
    z = jnp.concatenate([halo, u], axis=0)
    pos = pos0 + lax.broadcasted_iota(jnp.int32, (tm, 1), 0)
    gw = u.shape[1] // len(POOL_WINDOWS)
    d_groups = [_pool_diff(z[:, gi * gw:(gi + 1) * gw], u[:, gi * gw:(gi + 1) * gw], pos, w)
                for gi, w in enumerate(POOL_WINDOWS)]
    y_ref[...] = _mix_tail(d_groups, gp_ref[...], ga_ref[...], att_ref[...], ma_ref[...], mb_ref[...], x_ref[...],
                           wpool_ref, ps_ref, wup_ref, wua_ref, wout_ref)


def _out_sample_kernel(z_ref, gp_ref, ga_ref, ma_ref, mb_ref, att_ref, x_ref,
                       wpool_ref, ps_ref, wup_ref, wua_ref, wout_ref, y_ref, *, dec_b, n_new, pos0):
    gw = z_ref.shape[2] // len(POOL_WINDOWS)
    pos = pos0 + lax.broadcasted_iota(jnp.int32, (n_new, 1), 0)
    per_seq = []
    for b in range(dec_b):
        z = z_ref[b]
        u = z[HALO:, :]
        per_seq.append([_pool_diff(z[:, gi * gw:(gi + 1) * gw], u[:, gi * gw:(gi + 1) * gw], pos, w)
                        for gi, w in enumerate(POOL_WINDOWS)])
    d_groups = [jnp.concatenate([per_seq[b][gi] for b in range(dec_b)], axis=0) for gi in range(len(POOL_WINDOWS))]
    y_ref[...] = _mix_tail(d_groups, gp_ref[...], ga_ref[...], att_ref[...], ma_ref[...], mb_ref[...], x_ref[...],
                           wpool_ref, ps_ref, wup_ref, wua_ref, wout_ref)


def _weight_specs(pw, gw, dm):
    one = pl.Buffered(1)
    return [
        pl.BlockSpec((len(POOL_WINDOWS), gw, gw), lambda i: (0, 0, 0), pipeline_mode=one),
        pl.BlockSpec((1, pw), lambda i: (0, 0), pipeline_mode=one),
        pl.BlockSpec((pw, dm), lambda i: (0, 0), pipeline_mode=one),
        pl.BlockSpec((pw, dm), lambda i: (0, 0), pipeline_mode=one),
        pl.BlockSpec((dm, dm), lambda i: (0, 0), pipeline_mode=one),
    ]


def _out_prompt(hg, u, att, x2d, wpool, ps, wup, wua, wout, *, seq, tm, cols):
    m, dm = x2d.shape
    pw = att.shape[1]
    gw = pw // len(POOL_WINDOWS)
    hb = tm // HALO
    kern = functools.partial(_out_prompt_kernel, tm=tm, seq=seq)
    return pl.pallas_call(
        kern,
        grid=(m // tm,),
        in_specs=[
            pl.BlockSpec((tm, pw), lambda i: (i, 0)),
            pl.BlockSpec((HALO, pw), lambda i: (jnp.maximum(i * hb - 1, 0), 0)),
            pl.BlockSpec((tm, pw), lambda i: (i, cols["gp"] // pw)),
            pl.BlockSpec((tm, pw), lambda i: (i, cols["ga"] // pw)),
            pl.BlockSpec((tm, dm), lambda i: (i, cols["ma"] // dm)),
            pl.BlockSpec((tm, dm), lambda i: (i, cols["mb"] // dm)),
            pl.BlockSpec((tm, pw), lambda i: (i, 0)),
            pl.BlockSpec((tm, dm), lambda i: (i, 0)),
        ] + _weight_specs(pw, gw, dm),
        out_specs=pl.BlockSpec((tm, dm), lambda i: (i, 0)),
        out_shape=jax.ShapeDtypeStruct((m, dm), F32),
        compiler_params=_cparams(1),
        name="out_prompt",
    )(u, u, hg, hg, hg, hg, att, x2d, wpool, ps, wup, wua, wout)


def _out_sample(z, h, att, x2d, wpool, ps, wup, wua, wout, *, dec_b, n_new, pos0, cols):
    m, dm = x2d.shape
    pw = att.shape[1]
    gw = pw // len(POOL_WINDOWS)
    kern = functools.partial(_out_sample_kernel, dec_b=dec_b, n_new=n_new, pos0=pos0)
    return pl.pallas_call(
        kern,
        grid=(1,),
        in_specs=[
            pl.BlockSpec((dec_b, HALO + n_new, pw), lambda i: (0, 0, 0)),
            pl.BlockSpec((m, pw), lambda i: (0, cols["gp"] // pw)),
            pl.BlockSpec((m, pw), lambda i: (0, cols["ga"] // pw)),
            pl.BlockSpec((m, dm), lambda i: (0, cols["ma"] // dm)),
            pl.BlockSpec((m, dm), lambda i: (0, cols["mb"] // dm)),
            pl.BlockSpec((m, pw), lambda i: (0, 0)),
            pl.BlockSpec((m, dm), lambda i: (0, 0)),
        ] + _weight_specs(pw, gw, dm),
        out_specs=pl.BlockSpec((m, dm), lambda i: (0, 0)),
        out_shape=jax.ShapeDtypeStruct((m, dm), F32),
        compiler_params=_cparams(1),
        name="out_sample",
    )(z, h, h, h, h, att, x2d, wpool, ps, wup, wua, wout)


def kernel(x_prompt, x_sample, cache_k, cache_v, cache_logf, state_pool, page_table, norm_gain, w_in, b_f,
           q_norm_gain, k_norm_gain, w_pool_map, pool_scale, w_up_pool, w_up_attn, w_out):
    batch, seq, dm = x_prompt.shape
    dec_b, n_new, _ = x_sample.shape
    assert w_in.shape[0] == 1
    n_heads = b_f.shape[1]
    aw = n_heads * HEAD_DIM
    pw = w_up_pool.shape[1]
    n_pages = page_table.shape[1]
    past_len = n_pages * PAGE_SIZE
    assert pw == aw and dm == 2 * pw and n_pages % PAGES_PER_STEP == 0

    tn = 512
    plan = _inproj_plan(pw, aw, dm, n_heads, tn)
    cols = plan["cols"]
    w_t = w_in[0].T
    fl0, fl_n = plan["fl_rows"]
    wfl = jnp.pad(w_t[fl0:fl0 + fl_n], ((0, V7X_LANES - n_heads), (0, 0))).astype(BF16)
    bfp = jnp.pad(b_f, ((0, 0), (0, V7X_LANES - n_heads)))
    reps = tn // HEAD_DIM
    qg = jnp.tile(q_norm_gain[0], reps)[None, :]
    kg = jnp.tile(k_norm_gain[0], reps)[None, :]
    qgc = jnp.broadcast_to(jnp.tile(q_norm_gain[0], n_heads)[:, None], (aw, V7X_LANES))
    kgc = jnp.broadcast_to(jnp.tile(k_norm_gain[0], n_heads)[:, None], (aw, V7X_LANES))
    seg = np.arange(tn) // HEAD_DIM
    bd = jnp.asarray((seg[:, None] == seg[None, :]).astype(np.float32)).astype(BF16)
    wpool = w_pool_map[0].astype(BF16)
    wup = w_up_pool[0].astype(BF16)
    wua = w_up_attn[0].astype(BF16)
    wout = w_out[0].astype(BF16)

    ms = dec_b * n_new
    xs2 = x_sample.reshape(ms, dm)
    hs, lfs, wbf = _inproj_new(plan, xs2, norm_gain, w_t, wfl, bfp, qg, kg, bd)

    cache_kt = cache_k.transpose(0, 1, 3, 4, 2)
    cache_vt = cache_v.transpose(0, 1, 3, 4, 2)
    cache_lft = cache_logf.transpose(0, 1, 3, 2)
    decode_args = dict(n_heads=n_heads, n_new=n_new, n_pg=DECODE_PAGES_FUSED)

    xp2 = x_prompt.reshape(batch * seq, dm)
    xn_p, lfp = _prenorm(xp2, norm_gain, wfl, bfp, tm=512)
    hg_p, u_p, att_rows_a, n_fused = _h_decode(plan, page_table, xn_p, wbf, hs, lfs, cache_kt, cache_vt, cache_lft,
                                        tm=1024, **decode_args)
    qt_p, kt_p, vt_p = _inproj_fm(plan, xn_p, wbf, qgc, kgc, tm=1024, seq=seq, width=aw)
    fcol, ft, lft = _fcum(lfp, batch=batch, seq=seq, n_heads=n_heads)
    ft_pairs = ft.reshape(batch, n_heads // 2, 2, seq)
    att_p = _prompt_attention(qt_p, kt_p, vt_p, fcol, ft_pairs, batch=batch, seq=seq, n_heads=n_heads, tq=256,
                              n_pair=ATTN_PAIRS_PER_STEP)
    yp = _out_prompt(hg_p, u_p, att_p, xp2, wpool, pool_scale, wup, wua, wout, seq=seq, tm=256, cols=cols)

    k_s = hs[:, cols["k"]:cols["k"] + aw].reshape(dec_b, n_new, n_heads, HEAD_DIM)
    v_s = hs[:, cols["v"]:cols["v"] + aw].reshape(dec_b, n_new, n_heads, HEAD_DIM)
    u_s = hs[:, cols["u"]:cols["u"] + pw].reshape(dec_b, n_new, pw)
    logf_s = lfs[:, :n_heads].reshape(dec_b, n_new, n_heads)
    att_rows = att_rows_a
    if n_fused < dec_b:
        att_rows_b = _decode_attention(page_table, hs, lfs, cache_kt, cache_vt, cache_lft, n_heads=n_heads,
                                       n_new=n_new, n_pg=PAGES_PER_STEP, cols=cols, seq0=n_fused,
                                       n_seq=dec_b - n_fused)
        att_rows = jnp.concatenate([att_rows_a, att_rows_b], axis=0)
    att_s = att_rows.reshape(dec_b, n_heads, n_new, HEAD_DIM).transpose(0, 2, 1, 3).reshape(ms, aw)
    z_s = jnp.concatenate([jnp.zeros((dec_b, HALO - POOL_BUF, pw), F32), state_pool[0], u_s], axis=1)
    ys = _out_sample(z_s, hs, att_s, xs2, wpool, pool_scale, wup, wua, wout,
                     dec_b=dec_b, n_new=n_new, pos0=past_len, cols=cols)

    k_p = kt_p.reshape(1, batch, n_heads, HEAD_DIM, seq).transpose(0, 1, 4, 2, 3)
    v_p = vt_p.reshape(1, batch, n_heads, HEAD_DIM, seq).transpose(0, 1, 4, 2, 3)
    logf_p = lft.transpose(0, 2, 1)[None]
    pool_p = u_p.reshape(batch, seq, pw)[:, seq - POOL_BUF:, :][None]
    pool_s = z_s[:, HALO + n_new - POOL_BUF:, :][None]
    return (yp.reshape(batch, seq, dm), ys.reshape(dec_b, n_new, dm), k_p, v_p, logf_p, pool_p,
            k_s[None], v_s[None], logf_s[None], pool_s)
```

```python
import functools

import jax
import jax.numpy as jnp
import numpy as np
from jax import lax
from jax.experimental import pallas as pl
from jax.experimental.pallas import tpu as pltpu

F32 = jnp.float32
BF16 = jnp.bfloat16

HEAD_DIM = 64
POOL_WINDOWS = (2, 4, 8, 16)
POOL_BUF = 15
PAGE_SIZE = 128
EPS = 1e-6
NEG_INF = -1e30
ATTN_SCALE = HEAD_DIM ** -0.5
LOG2E = 1.4426950408889634

V7X_LANES = 128
V7X_SUBLANES = 8
V7X_VMEM_LIMIT_BYTES = 56 * 1024 * 1024

ROW_ALIGN = 16
HALO = 16
PAGES_PER_STEP = 16
DECODE_PAGES_FUSED = 8
N_AUG = 3
ATTN_PAIRS_PER_STEP = 8


def _cparams(n_grid_axes):
    return pltpu.CompilerParams(
        dimension_semantics=("arbitrary",) * n_grid_axes,
        vmem_limit_bytes=V7X_VMEM_LIMIT_BYTES,
    )


def _split3(x):
    hi = x.astype(BF16)
    r1 = x - hi.astype(F32)
    mid = r1.astype(BF16)
    lo = (r1 - mid.astype(F32)).astype(BF16)
    return hi, mid, lo


def _dot(a, b):
    return jnp.dot(a, b, preferred_element_type=F32)


def _dot_nt(a, b):
    return lax.dot_general(a, b, (((1,), (1,)), ((), ())), preferred_element_type=F32)


def _sigmoid(x):
    return 1.0 / (1.0 + jnp.exp(-x))


def _silu(x):
    return x * _sigmoid(x)


def _scan_lanes(x, *, reverse):
    n = x.shape[-1]
    ax = x.ndim - 1
    lane = lax.broadcasted_iota(jnp.int32, x.shape, ax)
    k = 1
    while k < n:
        if reverse:
            shifted = pltpu.roll(x, n - k, ax)
            x = x + jnp.where(lane < n - k, shifted, 0.0)
        else:
            shifted = pltpu.roll(x, k, ax)
            x = x + jnp.where(lane >= k, shifted, 0.0)
        k *= 2
    return x


def _divmod_nonneg(x, n):
    assert n > 0 and n & (n - 1) == 0
    return lax.shift_right_logical(x, n.bit_length() - 1), x & (n - 1)


def _scan_rows(x):
    n = x.shape[0]
    row = lax.broadcasted_iota(jnp.int32, x.shape, 0)
    k = 1
    while k < n:
        x = x + jnp.where(row >= k, pltpu.roll(x, k, 0), 0.0)
        k *= 2
    return x


def _norm_and_logf(x_ref, ng_ref, wfl_ref, bf_ref, xn_sc, lf_ref):
    x = x_ref[...]
    ms = jnp.mean(x * x, axis=-1, keepdims=True)
    xn = x * lax.rsqrt(ms + EPS) * ng_ref[...]
    xn_sc[...] = xn.astype(BF16)
    z = _dot_nt(xn_sc[...], wfl_ref[...]) + bf_ref[...]
    lf_ref[...] = jnp.minimum(z, 0.0) - jnp.log1p(jnp.exp(-jnp.abs(z)))


def _step_in(j, rng):
    return jnp.logical_and(j >= rng[0], j < rng[1])


def _inproj_new_kernel(off_ref, x_ref, ng_ref, w_ref, wfl_ref, bf_ref, qg_ref, kg_ref, bd_ref,
                       h_ref, lf_ref, wtok_ref, wfm_ref, xn_sc, *, steps):
    del off_ref
    j = pl.program_id(0)
    n_tok = steps["q"][0]

    @pl.when(j == 0)
    def _():
        _norm_and_logf(x_ref, ng_ref, wfl_ref, bf_ref, xn_sc, lf_ref)

    w = w_ref[...].astype(BF16)
    acc = _dot_nt(xn_sc[...], w)

    @pl.when(j < n_tok)
    def _():
        wtok_ref[...] = w_ref[...].T.astype(BF16)

    @pl.when(j >= n_tok)
    def _():
        wfm_ref[...] = w

    def head_norm(gain):
        sq = acc * acc
        hi = sq.astype(BF16)
        lo = (sq - hi.astype(F32)).astype(BF16)
        ss = _dot(hi, bd_ref[...]) + _dot(lo, bd_ref[...])
        return acc * lax.rsqrt(ss * (1.0 / HEAD_DIM) + EPS) * gain

    is_q = _step_in(j, steps["q"])
    is_k = _step_in(j, steps["k"])

    @pl.when(is_q)
    def _():
        h_ref[...] = head_norm(qg_ref[...])

    @pl.when(is_k)
    def _():
        h_ref[...] = head_norm(kg_ref[...])

    @pl.when(jnp.logical_not(is_q | is_k))
    def _():
        h_ref[...] = acc


def _inproj_new(plan, x2d, ng, w_t, wfl, bfp, qg, kg, bd):
    m, d = x2d.shape
    tn = plan["tn"]
    nj = plan["n_steps"]
    kern = functools.partial(_inproj_new_kernel, steps=plan["steps"])
    n_tok = plan["steps"]["q"][0]
    grid_spec = pltpu.PrefetchScalarGridSpec(
        num_scalar_prefetch=1,
        grid=(nj,),
        in_specs=[
            pl.BlockSpec((m, d), lambda j, off: (0, 0)),
            pl.BlockSpec((1, d), lambda j, off: (0, 0)),
            pl.BlockSpec((pl.Element(tn), pl.Element(d)), lambda j, off: (off[j] * ROW_ALIGN, 0)),
            pl.BlockSpec((V7X_LANES, d), lambda j, off: (0, 0)),
            pl.BlockSpec((1, V7X_LANES), lambda j, off: (0, 0)),
            pl.BlockSpec((1, tn), lambda j, off: (0, 0)),
            pl.BlockSpec((1, tn), lambda j, off: (0, 0)),
            pl.BlockSpec((tn, tn), lambda j, off: (0, 0)),
        ],
        out_specs=[
            pl.BlockSpec((m, tn), lambda j, off: (0, j)),
            pl.BlockSpec((m, V7X_LANES), lambda j, off: (0, 0)),
            pl.BlockSpec((None, d, tn), lambda j, off: (jnp.minimum(j, n_tok - 1), 0, 0)),
            pl.BlockSpec((tn, d), lambda j, off: (jnp.maximum(j - n_tok, 0), 0)),
        ],
        scratch_shapes=[pltpu.VMEM((m, d), BF16)],
    )
    return pl.pallas_call(
        kern,
        grid_spec=grid_spec,
        out_shape=[
            jax.ShapeDtypeStruct((m, nj * tn), F32),
            jax.ShapeDtypeStruct((m, V7X_LANES), F32),
            jax.ShapeDtypeStruct((n_tok, d, tn), BF16),
            jax.ShapeDtypeStruct(((nj - n_tok) * tn, d), BF16),
        ],
        compiler_params=_cparams(1),
        name="inproj_new",
    )(plan["row_offsets"], x2d, ng, w_t, wfl, bfp, qg, kg, bd)


def _prenorm_kernel(x_ref, ng_ref, wfl_ref, bf_ref, xn_ref, lf_ref):
    _norm_and_logf(x_ref, ng_ref, wfl_ref, bf_ref, xn_ref, lf_ref)


def _prenorm(x2d, ng, wfl, bfp, *, tm):
    m, d = x2d.shape
    return pl.pallas_call(
        _prenorm_kernel,
        grid=(m // tm,),
        in_specs=[
            pl.BlockSpec((tm, d), lambda i: (i, 0)),
            pl.BlockSpec((1, d), lambda i: (0, 0)),
            pl.BlockSpec((V7X_LANES, d), lambda i: (0, 0)),
            pl.BlockSpec((1, V7X_LANES), lambda i: (0, 0)),
        ],
        out_specs=[
            pl.BlockSpec((tm, d), lambda i: (i, 0)),
            pl.BlockSpec((tm, V7X_LANES), lambda i: (i, 0)),
        ],
        out_shape=[
            jax.ShapeDtypeStruct((m, d), BF16),
            jax.ShapeDtypeStruct((m, V7X_LANES), F32),
        ],
        compiler_params=_cparams(1),
        name="prenorm",
    )(x2d, ng, wfl, bfp)


def _inproj_fm_kernel(xn_ref, w_ref, qgc_ref, kgc_ref, qt_ref, kt_ref, vt_ref):
    j = pl.program_id(1)

    def head_norm(acc, gain_col_ref):
        tn, tm = acc.shape
        a3 = acc.reshape(tn // HEAD_DIM, HEAD_DIM, tm)
        ms = jnp.mean(a3 * a3, axis=1, keepdims=True)
        g3 = gain_col_ref[...].reshape(tn // HEAD_DIM, HEAD_DIM, V7X_LANES)[:, :, 0:1]
        return (a3 * lax.rsqrt(ms + EPS) * g3).reshape(tn, tm)

    @pl.when(j == 0)
    def _():
        qt_ref[...] = head_norm(_dot_nt(w_ref[...], xn_ref[...]), qgc_ref)

    @pl.when(j == 1)
    def _():
        kt_ref[...] = head_norm(_dot_nt(w_ref[...], xn_ref[...]), kgc_ref)

    @pl.when(j == 2)
    def _():
        vt_ref[...] = _dot_nt(w_ref[...], xn_ref[...])


def _inproj_fm(plan, xn, wbf, qgc, kgc, *, tm, seq, width):
    m, d = xn.shape
    assert wbf.shape[0] == 3 * width
    nb = seq // tm
    out_spec = pl.BlockSpec((None, width, tm), lambda i, j: (_divmod_nonneg(i, nb)[0], 0, _divmod_nonneg(i, nb)[1]))
    return pl.pallas_call(
        _inproj_fm_kernel,
        grid=(m // tm, 3),
        in_specs=[
            pl.BlockSpec((tm, d), lambda i, j: (i, 0)),
            pl.BlockSpec((width, d), lambda i, j: (j, 0)),
            pl.BlockSpec((width, V7X_LANES), lambda i, j: (0, 0)),
            pl.BlockSpec((width, V7X_LANES), lambda i, j: (0, 0)),
        ],
        out_specs=[out_spec] * 3,
        out_shape=[jax.ShapeDtypeStruct((m // seq, width, seq), F32)] * 3,
        compiler_params=_cparams(2),
        name="inproj_fm",
    )(xn, wbf, qgc, kgc)


def _inproj_plan(pw, aw, dm, n_heads, tn):
    src, off = {}, 0
    for name, size in (("u", pw), ("gp", pw), ("q", aw), ("k", aw), ("v", aw), ("ga", aw), ("fl", n_heads),
                       ("ma", dm), ("mb", dm)):
        src[name] = (off, size)
        off += size
    order = ("ma", "mb", "gp", "ga", "u", "q", "k", "v")
    row_offsets, steps, cols = [], {}, {}
    for name in order:
        base, size = src[name]
        steps[name] = (len(row_offsets), len(row_offsets) + size // tn)
        cols[name] = len(row_offsets) * tn
        row_offsets += [base + o for o in range(0, size, tn)]
    assert all(r % ROW_ALIGN == 0 for r in row_offsets)
    return {"tn": tn, "steps": steps, "n_steps": len(row_offsets), "cols": cols, "fl_rows": src["fl"],
            "row_offsets": jnp.asarray([r // ROW_ALIGN for r in row_offsets], jnp.int32)}


def _fcum_kernel(lf_ref, fcol_ref, ft_ref, lft_ref, *, seq, n_heads):
    c = V7X_LANES
    row = lax.broadcasted_iota(jnp.int32, (c, c), 0)
    col = lax.broadcasted_iota(jnp.int32, (c, c), 1)
    tri = jnp.where(col <= row, 1.0, 0.0).astype(BF16)
    carry = jnp.zeros((1, c), F32)
    for ci in range(seq // c):
        x = lf_ref[ci * c:(ci + 1) * c, :]
        hi, mid, lo = _split3(x)
        fc = (_dot(tri, hi) + _dot(tri, mid)) + _dot(tri, lo) + carry
        fcol_ref[ci * c:(ci + 1) * c, :] = fc
        ft_ref[:, ci * c:(ci + 1) * c] = fc.T[:n_heads, :]
        lft_ref[:, ci * c:(ci + 1) * c] = x.T[:n_heads, :]
        carry = fc[c - 1:c, :]


def _fcum(lf2d, *, batch, seq, n_heads):
    kern = functools.partial(_fcum_kernel, seq=seq, n_heads=n_heads)
    return pl.pallas_call(
        kern,
        grid=(batch,),
        in_specs=[pl.BlockSpec((seq, V7X_LANES), lambda b: (b, 0))],
        out_specs=[
            pl.BlockSpec((seq, V7X_LANES), lambda b: (b, 0)),
            pl.BlockSpec((None, n_heads, seq), lambda b: (b, 0, 0)),
            pl.BlockSpec((None, n_heads, seq), lambda b: (b, 0, 0)),
        ],
        out_shape=[
            jax.ShapeDtypeStruct((batch * seq, V7X_LANES), F32),
            jax.ShapeDtypeStruct((batch, n_heads, seq), F32),
            jax.ShapeDtypeStruct((batch, n_heads, seq), F32),
        ],
        compiler_params=_cparams(1),
        name="forget_cumsum",
    )(lf2d)


def _attn_kernel(qt_ref, kt_ref, vt_ref, fcol_ref, ft_ref, o_ref, ka_sc, *, tq, seq, n_pair):
    g = pl.program_id(1)
    i = pl.program_id(2)
    d = HEAD_DIM
    lanes = V7X_LANES
    lane1 = lax.broadcasted_iota(jnp.int32, (1, lanes), 1)

    @pl.when(i == 0)
    def _():
        prow = lax.broadcasted_iota(jnp.int32, (N_AUG * lanes, lanes), 0)
        lane = lax.broadcasted_iota(jnp.int32, (N_AUG * lanes, lanes), 1)
        piece, hrow = prow // lanes, prow % lanes
        ones = jnp.where(((lane1 >= d) & (lane1 < d + N_AUG)) | (lane1 < N_AUG), 1.0, 0.0)
        sels = []
        for pr in range(n_pair):
            head0 = 2 * (g * n_pair + pr)
            hit = ((hrow == head0) & (lane == d + N_AUG + piece)) | ((hrow == head0 + 1) & (lane == N_AUG + piece))
            sels.append(jnp.where(hit, -1.0, 0.0).astype(BF16))
        ck = 256
        for c in range(seq // ck):
            pieces = jnp.concatenate(_split3(fcol_ref[c * ck:(c + 1) * ck, :] * LOG2E), axis=1)
            for pr in range(n_pair):
                k_rows = kt_ref[pr * lanes:(pr + 1) * lanes, c * ck:(c + 1) * ck].T
                aug = ones + _dot(pieces, sels[pr])
                ka_sc[2 * pr, c * ck:(c + 1) * ck, :] = jnp.where(lane1 < d, k_rows, aug).astype(BF16)
                ka_sc[2 * pr + 1, c * ck:(c + 1) * ck, :] = jnp.where(lane1 >= d, k_rows, aug).astype(BF16)

    qs = pl.multiple_of(i * tq, tq)
    rowi = lax.broadcasted_iota(jnp.int32, (d, tq), 0)

    def aug_rows(f_row):
        hi, mid, lo = (x.astype(F32) for x in _split3(f_row))
        return jnp.where(rowi == 0, hi, jnp.where(rowi == 1, mid, jnp.where(rowi == 2, lo,
                         jnp.where(rowi < 2 * N_AUG, 1.0, 0.0))))

    qa = []
    for pr in range(n_pair):
        qt = qt_ref[pr * lanes:(pr + 1) * lanes, :] * (ATTN_SCALE * LOG2E)
        fq = ft_ref[pr, :, pl.ds(qs, tq)] * LOG2E
        qa.append(jnp.concatenate([qt[:d, :], aug_rows(fq[0:1, :])], axis=0).astype(BF16))
        qa.append(jnp.concatenate([aug_rows(fq[1:2, :]), qt[d:, :]], axis=0).astype(BF16))
    krow = lax.broadcasted_iota(jnp.int32, (tq, tq), 0)
    qcol = lax.broadcasted_iota(jnp.int32, (tq, tq), 1)
    causal = krow <= qcol

    def step(j, carry, masked):
        ks = pl.multiple_of(j * tq, tq)
        heads = range(2 * n_pair)
        scores = [_dot(ka_sc[hh, pl.ds(ks, tq), :], qa[hh]) for hh in heads]
        stats, probs = [], []
        for hh in heads:
            m, l, _ = carry[hh]
            s = jnp.where(causal, scores[hh], NEG_INF) if masked else scores[hh]
            m_new = jnp.maximum(m, jnp.max(s, axis=0, keepdims=True))
            alpha = jnp.exp2(m - m_new)
            pt = jnp.exp2(s - m_new)
            stats.append((m_new, alpha * l + jnp.sum(pt, axis=0, keepdims=True), alpha))
            probs.append(pt.astype(BF16))
        pvs = [_dot(vt_ref[hh * d:(hh + 1) * d, pl.ds(ks, tq)].astype(BF16), probs[hh]) for hh in heads]
        return tuple((stats[hh][0], stats[hh][1], stats[hh][2] * carry[hh][2] + pvs[hh]) for hh in heads)

    init = tuple((jnp.full((1, tq), NEG_INF, F32), jnp.zeros((1, tq), F32), jnp.zeros((d, tq), F32))
                 for _ in range(2 * n_pair))
    carry = lax.fori_loop(0, i, lambda j, c: step(j, c, False), init)
    final = step(i, carry, True)
    o_ref[...] = jnp.concatenate([acc / l for (_, l, acc) in final], axis=0).T


def _prompt_attention(qt, kt, vt, fcol, ft_pairs, *, batch, seq, n_heads, tq, n_pair):
    n_groups = n_heads // (2 * n_pair)
    qn = seq // tq
    rows = n_pair * V7X_LANES
    kern = functools.partial(_attn_kernel, tq=tq, seq=seq, n_pair=n_pair)
    return pl.pallas_call(
        kern,
        grid=(batch, n_groups, qn),
        in_specs=[
            pl.BlockSpec((None, rows, tq), lambda b, g, i: (b, g, i)),
            pl.BlockSpec((None, rows, seq), lambda b, g, i: (b, g, 0)),
            pl.BlockSpec((None, rows, seq), lambda b, g, i: (b, g, 0)),
            pl.BlockSpec((seq, V7X_LANES), lambda b, g, i: (b, 0)),
            pl.BlockSpec((None, n_pair, 2, seq), lambda b, g, i: (b, g, 0, 0)),
        ],
        out_specs=pl.BlockSpec((tq, rows), lambda b, g, i: (b * qn + i, g)),
        out_shape=jax.ShapeDtypeStruct((batch * seq, n_heads * HEAD_DIM), F32),
        scratch_shapes=[pltpu.VMEM((2 * n_pair, seq, V7X_LANES), BF16)],
        compiler_params=_cparams(3),
        name="prompt_attention",
    )(qt, kt, vt, fcol, ft_pairs)


def _decode_step(cj, n_chunks, q_ref, kn_ref, vn_ref, lfn_ref, kc_refs, vc_refs, lf_refs, o_ref, scratch,
                 *, n_heads, n_new, between=None):
    m_sc, l_sc, acc_sc, carry_sc, qbd_sc, qb_sc = scratch
    rows = n_heads * n_new
    feat = n_heads * HEAD_DIM

    def tile_rows(x):
        return jnp.concatenate([x] * n_heads, axis=0)

    def rep_rows(x):
        return jnp.concatenate([jnp.broadcast_to(x[h:h + 1, :], (n_new, x.shape[1])) for h in range(n_heads)], axis=0)

    def to_pages(x):
        xp = jnp.concatenate([x, jnp.zeros((PAGE_SIZE - n_new, feat), F32)], axis=0)
        return jnp.concatenate([xp[:, c * PAGE_SIZE:(c + 1) * PAGE_SIZE].T for c in range(feat // PAGE_SIZE)], axis=0)

    def score_phase(k_pages, bias):
        s = _dot(qbd_sc[...], k_pages) + bias
        m_prev = m_sc[...]
        m_new = jnp.maximum(m_prev, jnp.max(s, axis=1, keepdims=True))
        alpha = jnp.exp(m_prev - m_new)
        p = jnp.exp(s - m_new)
        l_sc[...] = alpha * l_sc[...] + jnp.sum(p, axis=1, keepdims=True)
        m_sc[...] = m_new
        return p, alpha

    def value_phase(v_pages, p, alpha):
        acc_sc[...] = alpha * acc_sc[...] + _dot_nt(p.astype(BF16), v_pages)

    @pl.when(cj == 0)
    def _():
        m_sc[...] = jnp.full(m_sc.shape, NEG_INF, F32)
        l_sc[...] = jnp.zeros(l_sc.shape, F32)
        acc_sc[...] = jnp.zeros(acc_sc.shape, F32)
        carry_sc[...] = jnp.zeros(carry_sc.shape, F32)
        row_head = lax.broadcasted_iota(jnp.int32, (rows, feat), 0) // n_new
        col_head = lax.broadcasted_iota(jnp.int32, (rows, feat), 1) // HEAD_DIM
        qbd_sc[...] = jnp.where(row_head == col_head, tile_rows(q_ref[...] * ATTN_SCALE), 0.0).astype(BF16)
        f_new = _scan_rows(lfn_ref[...])
        rh = lax.broadcasted_iota(jnp.int32, (rows, V7X_LANES), 0) // n_new
        ln = lax.broadcasted_iota(jnp.int32, (rows, V7X_LANES), 1)
        qb = jnp.sum(jnp.where(rh == ln, tile_rows(f_new), 0.0), axis=1, keepdims=True)
        qb_sc[...] = qb
        f_pad = jnp.concatenate([f_new, jnp.zeros((PAGE_SIZE - n_new, V7X_LANES), F32)], axis=0)
        f_new_t = f_pad.T[:n_heads, :]
        tok = lax.broadcasted_iota(jnp.int32, (rows, PAGE_SIZE), 0) % n_new
        pos = lax.broadcasted_iota(jnp.int32, (rows, PAGE_SIZE), 1)
        bias = (qb - rep_rows(f_new_t)) + jnp.where(pos <= tok, 0.0, NEG_INF)
        p, alpha = score_phase(to_pages(kn_ref[...]).astype(BF16), bias)
        value_phase(to_pages(vn_ref[...]).astype(BF16), p, alpha)

    carry = carry_sc[...]
    biases = []
    for lf_ref in lf_refs:
        lf = lf_ref[...]
        incl = _scan_lanes(lf, reverse=True)
        biases.append(rep_rows((incl - lf) + carry))
        carry = carry + incl[:, 0:1]
    carry_sc[...] = carry
    k_pages = jnp.concatenate([r[...].reshape(feat, PAGE_SIZE).astype(BF16) for r in kc_refs], axis=1)
    p, alpha = score_phase(k_pages, qb_sc[...] + jnp.concatenate(biases, axis=1))
    if between is not None:
        between()
    v_pages = jnp.concatenate([r[...].reshape(feat, PAGE_SIZE).astype(BF16) for r in vc_refs], axis=1)
    value_phase(v_pages, p, alpha)

    @pl.when(cj == n_chunks - 1)
    def _():
        outs = [acc_sc[h * n_new:(h + 1) * n_new, h * HEAD_DIM:(h + 1) * HEAD_DIM] for h in range(n_heads)]
        o_ref[...] = jnp.concatenate(outs, axis=0) / l_sc[...]


def _decode_scratch(n_heads, n_new):
    rows = n_heads * n_new
    feat = n_heads * HEAD_DIM
    return [
        pltpu.VMEM((rows, 1), F32),
        pltpu.VMEM((rows, 1), F32),
        pltpu.VMEM((rows, feat), F32),
        pltpu.VMEM((n_heads, PAGE_SIZE), F32),
        pltpu.VMEM((rows, feat), BF16),
        pltpu.VMEM((rows, 1), F32),
    ]


def _decode_in_specs(seq_of, chunk_of, *, n_heads, n_new, n_pg, n_pages, cols, n_lead):
    feat = n_heads * HEAD_DIM
    last = n_pages - 1

    def tok_spec(width, col):
        return pl.BlockSpec((n_new, width), lambda *a: (seq_of(*a[:n_lead]), col))

    def page_spec(g, tail):
        def idx(*a):
            pt = a[n_lead]
            page = pt[seq_of(*a[:n_lead]), last - (chunk_of(*a[:n_lead]) * n_pg + g)]
            return (0, page) + (0,) * (1 + len(tail))
        return pl.BlockSpec((None, None, n_heads) + tail, idx)

    kv_tail = (HEAD_DIM, PAGE_SIZE)
    specs = [tok_spec(feat, cols["q"] // feat), tok_spec(feat, cols["k"] // feat), tok_spec(feat, cols["v"] // feat),
             tok_spec(V7X_LANES, 0)]
    specs += [page_spec(g, kv_tail) for g in range(n_pg)]
    specs += [page_spec(g, kv_tail) for g in range(n_pg)]
    specs += [page_spec(g, (PAGE_SIZE,)) for g in range(n_pg)]
    return specs


def _decode_kernel(pt_ref, q_ref, kn_ref, vn_ref, lfn_ref, *rest, n_heads, n_new, n_pg):
    del pt_ref
    kc_refs = rest[0:n_pg]
    vc_refs = rest[n_pg:2 * n_pg]
    lf_refs = rest[2 * n_pg:3 * n_pg]
    o_ref = rest[3 * n_pg]
    _decode_step(pl.program_id(1), pl.num_programs(1), q_ref, kn_ref, vn_ref, lfn_ref, kc_refs, vc_refs, lf_refs,
                 o_ref, rest[3 * n_pg + 1:], n_heads=n_heads, n_new=n_new)


def _decode_attention(page_table, hs, lf_new, cache_kt, cache_vt, cache_lft, *, n_heads, n_new, n_pg, cols,
                      seq0, n_seq):
    n_pages = page_table.shape[1]
    rows = n_heads * n_new
    kern = functools.partial(_decode_kernel, n_heads=n_heads, n_new=n_new, n_pg=n_pg)
    grid_spec = pltpu.PrefetchScalarGridSpec(
        num_scalar_prefetch=1,
        grid=(n_seq, n_pages // n_pg),
        in_specs=_decode_in_specs(lambda b, j: b + seq0, lambda b, j: j, n_heads=n_heads, n_new=n_new, n_pg=n_pg,
                                  n_pages=n_pages, cols=cols, n_lead=2),
        out_specs=pl.BlockSpec((None, rows, HEAD_DIM), lambda b, j, pt: (b, 0, 0)),
        scratch_shapes=_decode_scratch(n_heads, n_new),
    )
    args = [page_table, hs, hs, hs, lf_new] + [cache_kt] * n_pg + [cache_vt] * n_pg + [cache_lft] * n_pg
    return pl.pallas_call(
        kern,
        grid_spec=grid_spec,
        out_shape=jax.ShapeDtypeStruct((n_seq, rows, HEAD_DIM), F32),
        compiler_params=_cparams(2),
        name="decode_attention",
    )(*args)


def _h_decode_kernel(pt_ref, xn_ref, w_ref, q_ref, kn_ref, vn_ref, lfn_ref, *rest, n_heads, n_new, n_pg, n_j,
                     n_gate, n_chunks):
    del pt_ref
    kc_refs = rest[0:n_pg]
    vc_refs = rest[n_pg:2 * n_pg]
    lf_refs = rest[2 * n_pg:3 * n_pg]
    hg_ref, u_ref, o_ref = rest[3 * n_pg:3 * n_pg + 3]
    j = pl.program_id(1)
    cj = _divmod_nonneg(pl.program_id(0) * n_j + j, n_chunks)[1]

    def step(store):
        _decode_step(cj, n_chunks, q_ref, kn_ref, vn_ref, lfn_ref, kc_refs, vc_refs, lf_refs, o_ref,
                     rest[3 * n_pg + 3:], n_heads=n_heads, n_new=n_new,
                     between=lambda: store(_dot(xn_ref[...], w_ref[...])))

    def store_gate(acc):
        hg_ref[...] = acc.astype(BF16)

    def store_u(acc):
        u_ref[...] = acc

    pl.when(j < n_gate)(lambda: step(store_gate))
    pl.when(j >= n_gate)(lambda: step(store_u))


def _h_decode(plan, page_table, xn, wbf, hs, lf_new, cache_kt, cache_vt, cache_lft, *, tm, n_heads, n_new, n_pg):
    m, d = xn.shape
    tn = plan["tn"]
    cols = plan["cols"]
    n_j = plan["steps"]["q"][0]
    n_pages = page_table.shape[1]
    n_chunks = n_pages // n_pg
    n_steps = (m // tm) * n_j
    assert n_steps % n_chunks == 0 and n_steps // n_chunks <= page_table.shape[0]
    n_seq = n_steps // n_chunks
    rows = n_heads * n_new
    n_gate = plan["steps"]["u"][0]
    assert plan["steps"]["u"][1] == n_j
    kern = functools.partial(_h_decode_kernel, n_heads=n_heads, n_new=n_new, n_pg=n_pg, n_j=n_j, n_gate=n_gate,
                             n_chunks=n_chunks)
    in_specs = [
        pl.BlockSpec((tm, d), lambda i, j, pt: (i, 0)),
        pl.BlockSpec((None, d, tn), lambda i, j, pt: (j, 0, 0)),
    ] + _decode_in_specs(lambda i, j: _divmod_nonneg(i * n_j + j, n_chunks)[0],
                         lambda i, j: _divmod_nonneg(i * n_j + j, n_chunks)[1],
                         n_heads=n_heads, n_new=n_new, n_pg=n_pg, n_pages=n_pages, cols=cols, n_lead=2)
    grid_spec = pltpu.PrefetchScalarGridSpec(
        num_scalar_prefetch=1,
        grid=(m // tm, n_j),
        in_specs=in_specs,
        out_specs=[
            pl.BlockSpec((tm, tn), lambda i, j, pt: (i, jnp.minimum(j, n_gate - 1))),
            pl.BlockSpec((tm, tn), lambda i, j, pt: (i, jnp.maximum(j - n_gate, 0))),
            pl.BlockSpec((None, rows, HEAD_DIM), lambda i, j, pt: (_divmod_nonneg(i * n_j + j, n_chunks)[0], 0, 0)),
        ],
        scratch_shapes=_decode_scratch(n_heads, n_new),
    )
    args = [page_table, xn, wbf, hs, hs, hs, lf_new] + [cache_kt] * n_pg + [cache_vt] * n_pg + [cache_lft] * n_pg
    hg, u, att_rows = pl.pallas_call(
        kern,
        grid_spec=grid_spec,
        out_shape=[
            jax.ShapeDtypeStruct((m, n_gate * tn), BF16),
            jax.ShapeDtypeStruct((m, (n_j - n_gate) * tn), F32),
            jax.ShapeDtypeStruct((n_seq, rows, HEAD_DIM), F32),
        ],
        compiler_params=_cparams(2),
        name="h_decode",
    )(*args)
    return hg, u, att_rows, n_seq


def _pool_diff(z, u, pos, group_w):
    s = z
    k = 1
    while k < group_w:
        s = s + pltpu.roll(s, k, 0)
        k *= 2
    cnt = jnp.minimum(pos + 1, group_w).astype(F32)
    return s[HALO:, :] / cnt - u


def _mix_tail(d_groups, gp, ga, att, ma, mb, x, wpool_ref, ps_ref, wup_ref, wua_ref, wout_ref):
    mixed = [_dot(d.astype(BF16), wpool_ref[gi]) for gi, d in enumerate(d_groups)]
    mixed = jnp.concatenate(mixed, axis=1)
    branch_a = (mixed * ps_ref[...]) * _silu(gp.astype(F32))
    branch_b = att * _silu(ga.astype(F32))
    up_a = _dot(branch_a.astype(BF16), wup_ref[...])
    up_b = _dot(branch_b.astype(BF16), wua_ref[...])
    merged = _sigmoid(ma.astype(F32)) * up_a + _sigmoid(mb.astype(F32)) * up_b
    return x + _dot(merged.astype(BF16), wout_ref[...])


def _out_prompt_kernel(u_ref, halo_ref, gp_ref, ga_ref, ma_ref, mb_ref, att_ref, x_ref,
                       wpool_ref, ps_ref, wup_ref, wua_ref, wout_ref, y_ref, *, tm, seq):
    i = pl.program_id(0)
    pos0 = (i * tm) % seq
    u = u_ref[...]
    halo = jnp.where(pos0 == 0, 0.0, halo_ref[...])
    z = jnp.concatenate([halo, u], axis=0)
    pos = pos0 + lax.broadcasted_iota(jnp.int32, (tm, 1), 0)
    gw = u.shape[1] // len(POOL_WINDOWS)
    d_groups = [_pool_diff(z[:, gi * gw:(gi + 1) * gw], u[:, gi * gw:(gi + 1) * gw], pos, w)
                for gi, w in enumerate(POOL_WINDOWS)]
    y_ref[...] = _mix_tail(d_groups, gp_ref[...], ga_ref[...], att_ref[...], ma_ref[...], mb_ref[...], x_ref[...],
                           wpool_ref, ps_ref, wup_ref, wua_ref, wout_ref)


def _out_sample_kernel(z_ref, gp_ref, ga_ref, ma_ref, mb_ref, att_ref, x_ref,
                       wpool_ref, ps_ref, wup_ref, wua_ref, wout_ref, y_ref, *, dec_b, n_new, pos0):
    gw = z_ref.shape[2] // len(POOL_WINDOWS)
    pos = pos0 + lax.broadcasted_iota(jnp.int32, (n_new, 1), 0)
    per_seq = []
    for b in range(dec_b):
        z = z_ref[b]
        u = z[HALO:, :]
        per_seq.append([_pool_diff(z[:, gi * gw:(gi + 1) * gw], u[:, gi * gw:(gi + 1) * gw], pos, w)
                        for gi, w in enumerate(POOL_WINDOWS)])
    d_groups = [jnp.concatenate([per_seq[b][gi] for b in range(dec_b)], axis=0) for gi in range(len(POOL_WINDOWS))]
    y_ref[...] = _mix_tail(d_groups, gp_ref[...], ga_ref[...], att_ref[...], ma_ref[...], mb_ref[...], x_ref[...],
                           wpool_ref, ps_ref, wup_ref, wua_ref, wout_ref)


def _weight_specs(pw, gw, dm):
    one = pl.Buffered(1)
    return [
        pl.BlockSpec((len(POOL_WINDOWS), gw, gw), lambda i: (0, 0, 0), pipeline_mode=one),
        pl.BlockSpec((1, pw), lambda i: (0, 0), pipeline_mode=one),
        pl.BlockSpec((pw, dm), lambda i: (0, 0), pipeline_mode=one),
        pl.BlockSpec((pw, dm), lambda i: (0, 0), pipeline_mode=one),
        pl.BlockSpec((dm, dm), lambda i: (0, 0), pipeline_mode=one),
    ]


def _out_prompt(hg, u, att, x2d, wpool, ps, wup, wua, wout, *, seq, tm, cols):
    m, dm = x2d.shape
    pw = att.shape[1]
    gw = pw // len(POOL_WINDOWS)
    hb = tm // HALO
    kern = functools.partial(_out_prompt_kernel, tm=tm, seq=seq)
    return pl.pallas_call(
        kern,
        grid=(m // tm,),
        in_specs=[
            pl.BlockSpec((tm, pw), lambda i: (i, 0)),
            pl.BlockSpec((HALO, pw), lambda i: (jnp.maximum(i * hb - 1, 0), 0)),
            pl.BlockSpec((tm, pw), lambda i: (i, cols["gp"] // pw)),
            pl.BlockSpec((tm, pw), lambda i: (i, cols["ga"] // pw)),
            pl.BlockSpec((tm, dm), lambda i: (i, cols["ma"] // dm)),
            pl.BlockSpec((tm, dm), lambda i: (i, cols["mb"] // dm)),
            pl.BlockSpec((tm, pw), lambda i: (i, 0)),
            pl.BlockSpec((tm, dm), lambda i: (i, 0)),
        ] + _weight_specs(pw, gw, dm),
        out_specs=pl.BlockSpec((tm, dm), lambda i: (i, 0)),
        out_shape=jax.ShapeDtypeStruct((m, dm), F32),
        compiler_params=_cparams(1),
        name="out_prompt",
    )(u, u, hg, hg, hg, hg, att, x2d, wpool, ps, wup, wua, wout)


def _out_sample(z, h, att, x2d, wpool, ps, wup, wua, wout, *, dec_b, n_new, pos0, cols):
    m, dm = x2d.shape
    pw = att.shape[1]
    gw = pw // len(POOL_WINDOWS)
    kern = functools.partial(_out_sample_kernel, dec_b=dec_b, n_new=n_new, pos0=pos0)
    return pl.pallas_call(
        kern,
        grid=(1,),
        in_specs=[
            pl.BlockSpec((dec_b, HALO + n_new, pw), lambda i: (0, 0, 0)),
            pl.BlockSpec((m, pw), lambda i: (0, cols["gp"] // pw)),
            pl.BlockSpec((m, pw), lambda i: (0, cols["ga"] // pw)),
            pl.BlockSpec((m, dm), lambda i: (0, cols["ma"] // dm)),
            pl.BlockSpec((m, dm), lambda i: (0, cols["mb"] // dm)),
            pl.BlockSpec((m, pw), lambda i: (0, 0)),
            pl.BlockSpec((m, dm), lambda i: (0, 0)),
        ] + _weight_specs(pw, gw, dm),
        out_specs=pl.BlockSpec((m, dm), lambda i: (0, 0)),
        out_shape=jax.ShapeDtypeStruct((m, dm), F32),
        compiler_params=_cparams(1),
        name="out_sample",
    )(z, h, h, h, h, att, x2d, wpool, ps, wup, wua, wout)


def kernel(x_prompt, x_sample, cache_k, cache_v, cache_logf, state_pool, page_table, norm_gain, w_in, b_f,
           q_norm_gain, k_norm_gain, w_pool_map, pool_scale, w_up_pool, w_up_attn, w_out):
    batch, seq, dm = x_prompt.shape
    dec_b, n_new, _ = x_sample.shape
    assert w_in.shape[0] == 1
    n_heads = b_f.shape[1]
    aw = n_heads * HEAD_DIM
    pw = w_up_pool.shape[1]
    n_pages = page_table.shape[1]
    past_len = n_pages * PAGE_SIZE
    assert pw == aw and dm == 2 * pw and n_pages % PAGES_PER_STEP == 0

    tn = 512
    plan = _inproj_plan(pw, aw, dm, n_heads, tn)
    cols = plan["cols"]
    w_t = w_in[0].T
    fl0, fl_n = plan["fl_rows"]
    wfl = jnp.pad(w_t[fl0:fl0 + fl_n], ((0, V7X_LANES - n_heads), (0, 0))).astype(BF16)
    bfp = jnp.pad(b_f, ((0, 0), (0, V7X_LANES - n_heads)))
    reps = tn // HEAD_DIM
    qg = jnp.tile(q_norm_gain[0], reps)[None, :]
    kg = jnp.tile(k_norm_gain[0], reps)[None, :]
    qgc = jnp.broadcast_to(jnp.tile(q_norm_gain[0], n_heads)[:, None], (aw, V7X_LANES))
    kgc = jnp.broadcast_to(jnp.tile(k_norm_gain[0], n_heads)[:, None], (aw, V7X_LANES))
    seg = np.arange(tn) // HEAD_DIM
    bd = jnp.asarray((seg[:, None] == seg[None, :]).astype(np.float32)).astype(BF16)
    wpool = w_pool_map[0].astype(BF16)
    wup = w_up_pool[0].astype(BF16)
    wua = w_up_attn[0].astype(BF16)
    wout = w_out[0].astype(BF16)

    ms = dec_b * n_new
    xs2 = x_sample.reshape(ms, dm)
    hs, lfs, w_tok, w_fm = _inproj_new(plan, xs2, norm_gain, w_t, wfl, bfp, qg, kg, bd)

    cache_kt = cache_k.transpose(0, 1, 3, 4, 2)
    cache_vt = cache_v.transpose(0, 1, 3, 4, 2)
    cache_lft = cache_logf.transpose(0, 1, 3, 2)
    decode_args = dict(n_heads=n_heads, n_new=n_new, n_pg=DECODE_PAGES_FUSED)

    xp2 = x_prompt.reshape(batch * seq, dm)
    xn_p, lfp = _prenorm(xp2, norm_gain, wfl, bfp, tm=1024)
    hg_p, u_p, att_rows_a, n_fused = _h_decode(plan, page_table, xn_p, w_tok, hs, lfs, cache_kt, cache_vt,
                                               cache_lft, tm=1024, **decode_args)
    qt_p, kt_p, vt_p = _inproj_fm(plan, xn_p, w_fm, qgc, kgc, tm=1024, seq=seq, width=aw)
    fcol, ft, lft = _fcum(lfp, batch=batch, seq=seq, n_heads=n_heads)
    ft_pairs = ft.reshape(batch, n_heads // 2, 2, seq)
    att_p = _prompt_attention(qt_p, kt_p, vt_p, fcol, ft_pairs, batch=batch, seq=seq, n_heads=n_heads, tq=256,
                              n_pair=ATTN_PAIRS_PER_STEP)
    yp = _out_prompt(hg_p, u_p, att_p, xp2, wpool, pool_scale, wup, wua, wout, seq=seq, tm=256, cols=cols)

    k_s = hs[:, cols["k"]:cols["k"] + aw].reshape(dec_b, n_new, n_heads, HEAD_DIM)
    v_s = hs[:, cols["v"]:cols["v"] + aw].reshape(dec_b, n_new, n_heads, HEAD_DIM)
    u_s = hs[:, cols["u"]:cols["u"] + pw].reshape(dec_b, n_new, pw)
    logf_s = lfs[:, :n_heads].reshape(dec_b, n_new, n_heads)
    att_rows = att_rows_a
    if n_fused < dec_b:
        att_rows_b = _decode_attention(page_table, hs, lfs, cache_kt, cache_vt, cache_lft, n_heads=n_heads,
                                       n_new=n_new, n_pg=PAGES_PER_STEP, cols=cols, seq0=n_fused,
                                       n_seq=dec_b - n_fused)
        att_rows = jnp.concatenate([att_rows_a, att_rows_b], axis=0)
    att_s = att_rows.reshape(dec_b, n_heads, n_new, HEAD_DIM).transpose(0, 2, 1, 3).reshape(ms, aw)
    z_s = jnp.concatenate([jnp.zeros((dec_b, HALO - POOL_BUF, pw), F32), state_pool[0], u_s], axis=1)
    ys = _out_sample(z_s, hs, att_s, xs2, wpool, pool_scale, wup, wua, wout,
                     dec_b=dec_b, n_new=n_new, pos0=past_len, cols=cols)

    k_p = kt_p.reshape(1, batch, n_heads, HEAD_DIM, seq).transpose(0, 1, 4, 2, 3)
    v_p = vt_p.reshape(1, batch, n_heads, HEAD_DIM, seq).transpose(0, 1, 4, 2, 3)
    logf_p = lft.transpose(0, 2, 1)[None]
    pool_p = u_p.reshape(batch, seq, pw)[:, seq - POOL_BUF:, :][None]
    pool_s = z_s[:, HALO + n_new - POOL_BUF:, :][None]
    return (yp.reshape(batch, seq, dm), ys.reshape(dec_b, n_new, dm), k_p, v_p, logf_p, pool_p,
            k_s[None], v_s[None], logf_s[None], pool_s)
```

```python
import functools

import jax
import jax.numpy as jnp
import numpy as np
from jax import lax
from jax.experimental import pallas as pl
from jax.experimental.pallas import tpu as pltpu

F32 = jnp.float32
BF16 = jnp.bfloat16

HEAD_DIM = 64
POOL_WINDOWS = (2, 4, 8, 16)
POOL_BUF = 15
PAGE_SIZE = 128
EPS = 1e-6
NEG_INF = -1e30
ATTN_SCALE = HEAD_DIM ** -0.5
LOG2E = 1.4426950408889634

V7X_LANES = 128
V7X_SUBLANES = 8
V7X_VMEM_LIMIT_BYTES = 56 * 1024 * 1024

ROW_ALIGN = 16
HALO = 16
PAGES_PER_STEP = 16
DECODE_PAGES_FUSED = 8
N_AUG = 3
ATTN_PAIRS_PER_STEP = 8


def _cparams(n_grid_axes):
    return pltpu.CompilerParams(
        dimension_semantics=("arbitrary",) * n_grid_axes,
        vmem_limit_bytes=V7X_VMEM_LIMIT_BYTES,
    )


def _split3(x):
    hi = x.astype(BF16)
    r1 = x - hi.astype(F32)
    mid = r1.astype(BF16)
    lo = (r1 - mid.astype(F32)).astype(BF16)
    return hi, mid, lo


def _dot(a, b):
    return jnp.dot(a, b, preferred_element_type=F32)


def _dot_nt(a, b):
    return lax.dot_general(a, b, (((1,), (1,)), ((), ())), preferred_element_type=F32)


def _sigmoid(x):
    return 1.0 / (1.0 + jnp.exp(-x))


def _silu(x):
    return x * _sigmoid(x)


def _scan_lanes(x, *, reverse):
    n = x.shape[-1]
    ax = x.ndim - 1
    lane = lax.broadcasted_iota(jnp.int32, x.shape, ax)
    k = 1
    while k < n:
        if reverse:
            shifted = pltpu.roll(x, n - k, ax)
            x = x + jnp.where(lane < n - k, shifted, 0.0)
        else:
            shifted = pltpu.roll(x, k, ax)
            x = x + jnp.where(lane >= k, shifted, 0.0)
        k *= 2
    return x


def _divmod_nonneg(x, n):
    assert n > 0 and n & (n - 1) == 0
    return lax.shift_right_logical(x, n.bit_length() - 1), x & (n - 1)


def _scan_rows(x):
    n = x.shape[0]
    row = lax.broadcasted_iota(jnp.int32, x.shape, 0)
    k = 1
    while k < n:
        x = x + jnp.where(row >= k, pltpu.roll(x, k, 0), 0.0)
        k *= 2
    return x


def _norm_and_logf(x_ref, ng_ref, wfl_ref, bf_ref, xn_sc, lf_ref):
    x = x_ref[...]
    ms = jnp.mean(x * x, axis=-1, keepdims=True)
    xn = x * lax.rsqrt(ms + EPS) * ng_ref[...]
    xn_sc[...] = xn.astype(BF16)
    z = _dot_nt(xn_sc[...], wfl_ref[...]) + bf_ref[...]
    lf_ref[...] = jnp.minimum(z, 0.0) - jnp.log1p(jnp.exp(-jnp.abs(z)))


def _step_in(j, rng):
    return jnp.logical_and(j >= rng[0], j < rng[1])


def _inproj_new_kernel(off_ref, x_ref, ng_ref, w_ref, wfl_ref, bf_ref, qg_ref, kg_ref, bd_ref,
                       h_ref, lf_ref, wtok_ref, wfm_ref, xn_sc, *, steps):
    del off_ref
    j = pl.program_id(0)
    n_tok = steps["q"][0]

    @pl.when(j == 0)
    def _():
        _norm_and_logf(x_ref, ng_ref, wfl_ref, bf_ref, xn_sc, lf_ref)

    w = w_ref[...].astype(BF16)
    acc = _dot_nt(xn_sc[...], w)

    @pl.when(j < n_tok)
    def _():
        wtok_ref[...] = w_ref[...].T.astype(BF16)

    @pl.when(j >= n_tok)
    def _():
        wfm_ref[...] = w

    def head_norm(gain):
        sq = acc * acc
        hi = sq.astype(BF16)
        lo = (sq - hi.astype(F32)).astype(BF16)
        ss = _dot(hi, bd_ref[...]) + _dot(lo, bd_ref[...])
        return acc * lax.rsqrt(ss * (1.0 / HEAD_DIM) + EPS) * gain

    is_q = _step_in(j, steps["q"])
    is_k = _step_in(j, steps["k"])

    @pl.when(is_q)
    def _():
        h_ref[...] = head_norm(qg_ref[...])

    @pl.when(is_k)
    def _():
        h_ref[...] = head_norm(kg_ref[...])

    @pl.when(jnp.logical_not(is_q | is_k))
    def _():
        h_ref[...] = acc


def _inproj_new(plan, x2d, ng, w_t, wfl, bfp, qg, kg, bd):
    m, d = x2d.shape
    tn = plan["tn"]
    nj = plan["n_steps"]
    kern = functools.partial(_inproj_new_kernel, steps=plan["steps"])
    n_tok = plan["steps"]["q"][0]
    grid_spec = pltpu.PrefetchScalarGridSpec(
        num_scalar_prefetch=1,
        grid=(nj,),
        in_specs=[
            pl.BlockSpec((m, d), lambda j, off: (0, 0)),
            pl.BlockSpec((1, d), lambda j, off: (0, 0)),
            pl.BlockSpec((pl.Element(tn), pl.Element(d)), lambda j, off: (off[j] * ROW_ALIGN, 0)),
            pl.BlockSpec((V7X_LANES, d), lambda j, off: (0, 0)),
            pl.BlockSpec((1, V7X_LANES), lambda j, off: (0, 0)),
            pl.BlockSpec((1, tn), lambda j, off: (0, 0)),
            pl.BlockSpec((1, tn), lambda j, off: (0, 0)),
            pl.BlockSpec((tn, tn), lambda j, off: (0, 0)),
        ],
        out_specs=[
            pl.BlockSpec((m, tn), lambda j, off: (0, j)),
            pl.BlockSpec((m, V7X_LANES), lambda j, off: (0, 0)),
            pl.BlockSpec((None, d, tn), lambda j, off: (jnp.minimum(j, n_tok - 1), 0, 0)),
            pl.BlockSpec((tn, d), lambda j, off: (jnp.maximum(j - n_tok, 0), 0)),
        ],
        scratch_shapes=[pltpu.VMEM((m, d), BF16)],
    )
    return pl.pallas_call(
        kern,
        grid_spec=grid_spec,
        out_shape=[
            jax.ShapeDtypeStruct((m, nj * tn), F32),
            jax.ShapeDtypeStruct((m, V7X_LANES), F32),
            jax.ShapeDtypeStruct((n_tok, d, tn), BF16),
            jax.ShapeDtypeStruct(((nj - n_tok) * tn, d), BF16),
        ],
        compiler_params=_cparams(1),
        name="inproj_new",
    )(plan["row_offsets"], x2d, ng, w_t, wfl, bfp, qg, kg, bd)


def _prenorm_kernel(x_ref, ng_ref, wfl_ref, bf_ref, xn_ref, lf_ref):
    _norm_and_logf(x_ref, ng_ref, wfl_ref, bf_ref, xn_ref, lf_ref)


def _prenorm(x2d, ng, wfl, bfp, *, tm):
    m, d = x2d.shape
    return pl.pallas_call(
        _prenorm_kernel,
        grid=(m // tm,),
        in_specs=[
            pl.BlockSpec((tm, d), lambda i: (i, 0)),
            pl.BlockSpec((1, d), lambda i: (0, 0)),
            pl.BlockSpec((V7X_LANES, d), lambda i: (0, 0)),
            pl.BlockSpec((1, V7X_LANES), lambda i: (0, 0)),
        ],
        out_specs=[
            pl.BlockSpec((tm, d), lambda i: (i, 0)),
            pl.BlockSpec((tm, V7X_LANES), lambda i: (i, 0)),
        ],
        out_shape=[
            jax.ShapeDtypeStruct((m, d), BF16),
            jax.ShapeDtypeStruct((m, V7X_LANES), F32),
        ],
        compiler_params=_cparams(1),
        name="prenorm",
    )(x2d, ng, wfl, bfp)


def _inproj_fm_kernel(xn_ref, w_ref, qgc_ref, kgc_ref, qt_ref, kt_ref, vt_ref):
    j = pl.program_id(1)

    def head_norm(acc, gain_col_ref):
        tn, tm = acc.shape
        a3 = acc.reshape(tn // HEAD_DIM, HEAD_DIM, tm)
        ms = jnp.mean(a3 * a3, axis=1, keepdims=True)
        g3 = gain_col_ref[...].reshape(tn // HEAD_DIM, HEAD_DIM, V7X_LANES)[:, :, 0:1]
        return (a3 * lax.rsqrt(ms + EPS) * g3).reshape(tn, tm)

    @pl.when(j == 0)
    def _():
        qt_ref[...] = head_norm(_dot_nt(w_ref[...], xn_ref[...]), qgc_ref)

    @pl.when(j == 1)
    def _():
        kt_ref[...] = head_norm(_dot_nt(w_ref[...], xn_ref[...]), kgc_ref)

    @pl.when(j == 2)
    def _():
        vt_ref[...] = _dot_nt(w_ref[...], xn_ref[...])


def _inproj_fm(plan, xn, wbf, qgc, kgc, *, tm, seq, width):
    m, d = xn.shape
    assert wbf.shape[0] == 3 * width
    nb = seq // tm
    out_spec = pl.BlockSpec((None, width, tm), lambda i, j: (_divmod_nonneg(i, nb)[0], 0, _divmod_nonneg(i, nb)[1]))
    return pl.pallas_call(
        _inproj_fm_kernel,
        grid=(m // tm, 3),
        in_specs=[
            pl.BlockSpec((tm, d), lambda i, j: (i, 0)),
            pl.BlockSpec((width, d), lambda i, j: (j, 0)),
            pl.BlockSpec((width, V7X_LANES), lambda i, j: (0, 0)),
            pl.BlockSpec((width, V7X_LANES), lambda i, j: (0, 0)),
        ],
        out_specs=[out_spec] * 3,
        out_shape=[jax.ShapeDtypeStruct((m // seq, width, seq), F32)] * 3,
        compiler_params=_cparams(2),
        name="inproj_fm",
    )(xn, wbf, qgc, kgc)


def _inproj_plan(pw, aw, dm, n_heads, tn):
    src, off = {}, 0
    for name, size in (("u", pw), ("gp", pw), ("q", aw), ("k", aw), ("v", aw), ("ga", aw), ("fl", n_heads),
                       ("ma", dm), ("mb", dm)):
        src[name] = (off, size)
        off += size
    order = ("ma", "mb", "gp", "ga", "u", "q", "k", "v")
    row_offsets, steps, cols = [], {}, {}
    for name in order:
        base, size = src[name]
        steps[name] = (len(row_offsets), len(row_offsets) + size // tn)
        cols[name] = len(row_offsets) * tn
        row_offsets += [base + o for o in range(0, size, tn)]
    assert all(r % ROW_ALIGN == 0 for r in row_offsets)
    return {"tn": tn, "steps": steps, "n_steps": len(row_offsets), "cols": cols, "fl_rows": src["fl"],
            "row_offsets": jnp.asarray([r // ROW_ALIGN for r in row_offsets], jnp.int32)}


def _fcum_kernel(lf_ref, fcol_ref, ft_ref, lft_ref, *, seq, n_heads):
    c = V7X_LANES
    row = lax.broadcasted_iota(jnp.int32, (c, c), 0)
    col = lax.broadcasted_iota(jnp.int32, (c, c), 1)
    tri = jnp.where(col <= row, 1.0, 0.0).astype(BF16)
    carry = jnp.zeros((1, c), F32)
    for ci in range(seq // c):
        x = lf_ref[ci * c:(ci + 1) * c, :]
        hi, mid, lo = _split3(x)
        fc = (_dot(tri, hi) + _dot(tri, mid)) + _dot(tri, lo) + carry
        fcol_ref[ci * c:(ci + 1) * c, :] = fc
        ft_ref[:, ci * c:(ci + 1) * c] = fc.T[:n_heads, :]
        lft_ref[:, ci * c:(ci + 1) * c] = x.T[:n_heads, :]
        carry = fc[c - 1:c, :]


def _fcum(lf2d, *, batch, seq, n_heads):
    kern = functools.partial(_fcum_kernel, seq=seq, n_heads=n_heads)
    return pl.pallas_call(
        kern,
        grid=(batch,),
        in_specs=[pl.BlockSpec((seq, V7X_LANES), lambda b: (b, 0))],
        out_specs=[
            pl.BlockSpec((seq, V7X_LANES), lambda b: (b, 0)),
            pl.BlockSpec((None, n_heads, seq), lambda b: (b, 0, 0)),
            pl.BlockSpec((None, n_heads, seq), lambda b: (b, 0, 0)),
        ],
        out_shape=[
            jax.ShapeDtypeStruct((batch * seq, V7X_LANES), F32),
            jax.ShapeDtypeStruct((batch, n_heads, seq), F32),
            jax.ShapeDtypeStruct((batch, n_heads, seq), F32),
        ],
        compiler_params=_cparams(1),
        name="forget_cumsum",
    )(lf2d)


def _attn_kernel(qt_ref, kt_ref, vt_ref, fcol_ref, ft_ref, o_ref, ka_sc, *, tq, seq, n_pair):
    g = pl.program_id(1)
    i = pl.program_id(2)
    d = HEAD_DIM
    lanes = V7X_LANES
    lane1 = lax.broadcasted_iota(jnp.int32, (1, lanes), 1)

    @pl.when(i == 0)
    def _():
        prow = lax.broadcasted_iota(jnp.int32, (N_AUG * lanes, lanes), 0)
        lane = lax.broadcasted_iota(jnp.int32, (N_AUG * lanes, lanes), 1)
        piece, hrow = prow // lanes, prow % lanes
        ones = jnp.where(((lane1 >= d) & (lane1 < d + N_AUG)) | (lane1 < N_AUG), 1.0, 0.0)
        sels = []
        for pr in range(n_pair):
            head0 = 2 * (g * n_pair + pr)
            hit = ((hrow == head0) & (lane == d + N_AUG + piece)) | ((hrow == head0 + 1) & (lane == N_AUG + piece))
            sels.append(jnp.where(hit, -1.0, 0.0).astype(BF16))
        ck = 256
        for c in range(seq // ck):
            pieces = jnp.concatenate(_split3(fcol_ref[c * ck:(c + 1) * ck, :] * LOG2E), axis=1)
            for pr in range(n_pair):
                k_rows = kt_ref[pr * lanes:(pr + 1) * lanes, c * ck:(c + 1) * ck].T
                aug = ones + _dot(pieces, sels[pr])
                ka_sc[2 * pr, c * ck:(c + 1) * ck, :] = jnp.where(lane1 < d, k_rows, aug).astype(BF16)
                ka_sc[2 * pr + 1, c * ck:(c + 1) * ck, :] = jnp.where(lane1 >= d, k_rows, aug).astype(BF16)

    qs = pl.multiple_of(i * tq, tq)
    rowi = lax.broadcasted_iota(jnp.int32, (d, tq), 0)

    def aug_rows(f_row):
        hi, mid, lo = (x.astype(F32) for x in _split3(f_row))
        return jnp.where(rowi == 0, hi, jnp.where(rowi == 1, mid, jnp.where(rowi == 2, lo,
                         jnp.where(rowi < 2 * N_AUG, 1.0, 0.0))))

    qa = []
    for pr in range(n_pair):
        qt = qt_ref[pr * lanes:(pr + 1) * lanes, :] * (ATTN_SCALE * LOG2E)
        fq = ft_ref[pr, :, pl.ds(qs, tq)] * LOG2E
        qa.append(jnp.concatenate([qt[:d, :], aug_rows(fq[0:1, :])], axis=0).astype(BF16))
        qa.append(jnp.concatenate([aug_rows(fq[1:2, :]), qt[d:, :]], axis=0).astype(BF16))
    krow = lax.broadcasted_iota(jnp.int32, (tq, tq), 0)
    qcol = lax.broadcasted_iota(jnp.int32, (tq, tq), 1)
    causal = krow <= qcol

    def step(j, carry, masked):
        ks = pl.multiple_of(j * tq, tq)
        heads = range(2 * n_pair)
        scores = [_dot(ka_sc[hh, pl.ds(ks, tq), :], qa[hh]) for hh in heads]
        stats, probs = [], []
        for hh in heads:
            m, l, _ = carry[hh]
            s = jnp.where(causal, scores[hh], NEG_INF) if masked else scores[hh]
            m_new = jnp.maximum(m, jnp.max(s, axis=0, keepdims=True))
            alpha = jnp.exp2(m - m_new)
            pt = jnp.exp2(s - m_new)
            stats.append((m_new, alpha * l + jnp.sum(pt, axis=0, keepdims=True), alpha))
            probs.append(pt.astype(BF16))
        pvs = [_dot(vt_ref[hh * d:(hh + 1) * d, pl.ds(ks, tq)].astype(BF16), probs[hh]) for hh in heads]
        return tuple((stats[hh][0], stats[hh][1], stats[hh][2] * carry[hh][2] + pvs[hh]) for hh in heads)

    init = tuple((jnp.full((1, tq), NEG_INF, F32), jnp.zeros((1, tq), F32), jnp.zeros((d, tq), F32))
                 for _ in range(2 * n_pair))
    carry = lax.fori_loop(0, i, lambda j, c: step(j, c, False), init)
    final = step(i, carry, True)
    o_ref[...] = jnp.concatenate([acc / l for (_, l, acc) in final], axis=0).T


def _prompt_attention(qt, kt, vt, fcol, ft_pairs, *, batch, seq, n_heads, tq, n_pair):
    n_groups = n_heads // (2 * n_pair)
    qn = seq // tq
    rows = n_pair * V7X_LANES
    kern = functools.partial(_attn_kernel, tq=tq, seq=seq, n_pair=n_pair)
    return pl.pallas_call(
        kern,
        grid=(batch, n_groups, qn),
        in_specs=[
            pl.BlockSpec((None, rows, tq), lambda b, g, i: (b, g, i)),
            pl.BlockSpec((None, rows, seq), lambda b, g, i: (b, g, 0)),
            pl.BlockSpec((None, rows, seq), lambda b, g, i: (b, g, 0)),
            pl.BlockSpec((seq, V7X_LANES), lambda b, g, i: (b, 0)),
            pl.BlockSpec((None, n_pair, 2, seq), lambda b, g, i: (b, g, 0, 0)),
        ],
        out_specs=pl.BlockSpec((tq, rows), lambda b, g, i: (b * qn + i, g)),
        out_shape=jax.ShapeDtypeStruct((batch * seq, n_heads * HEAD_DIM), F32),
        scratch_shapes=[pltpu.VMEM((2 * n_pair, seq, V7X_LANES), BF16)],
        compiler_params=_cparams(3),
        name="prompt_attention",
    )(qt, kt, vt, fcol, ft_pairs)


def _decode_step(cj, n_chunks, q_ref, kn_ref, vn_ref, lfn_ref, kc_refs, vc_refs, lf_refs, o_ref, scratch,
                 *, n_heads, n_new, between=None):
    m_sc, l_sc, acc_sc, carry_sc, qbd_sc, qb_sc = scratch
    rows = n_heads * n_new
    feat = n_heads * HEAD_DIM

    def tile_rows(x):
        return jnp.concatenate([x] * n_heads, axis=0)

    def rep_rows(x):
        return jnp.concatenate([jnp.broadcast_to(x[h:h + 1, :], (n_new, x.shape[1])) for h in range(n_heads)], axis=0)

    def to_pages(x):
        xp = jnp.concatenate([x, jnp.zeros((PAGE_SIZE - n_new, feat), F32)], axis=0)
        return jnp.concatenate([xp[:, c * PAGE_SIZE:(c + 1) * PAGE_SIZE].T for c in range(feat // PAGE_SIZE)], axis=0)

    def score_phase(k_pages, bias):
        s = _dot(qbd_sc[...], k_pages) + bias
        m_prev = m_sc[...]
        m_new = jnp.maximum(m_prev, jnp.max(s, axis=1, keepdims=True))
        alpha = jnp.exp(m_prev - m_new)
        p = jnp.exp(s - m_new)
        l_sc[...] = alpha * l_sc[...] + jnp.sum(p, axis=1, keepdims=True)
        m_sc[...] = m_new
        return p, alpha

    def value_phase(v_pages, p, alpha):
        acc_sc[...] = alpha * acc_sc[...] + _dot_nt(p.astype(BF16), v_pages)

    @pl.when(cj == 0)
    def _():
        m_sc[...] = jnp.full(m_sc.shape, NEG_INF, F32)
        l_sc[...] = jnp.zeros(l_sc.shape, F32)
        acc_sc[...] = jnp.zeros(acc_sc.shape, F32)
        carry_sc[...] = jnp.zeros(carry_sc.shape, F32)
        row_head = lax.broadcasted_iota(jnp.int32, (rows, feat), 0) // n_new
        col_head = lax.broadcasted_iota(jnp.int32, (rows, feat), 1) // HEAD_DIM
        qbd_sc[...] = jnp.where(row_head == col_head, tile_rows(q_ref[...] * ATTN_SCALE), 0.0).astype(BF16)
        f_new = _scan_rows(lfn_ref[...])
        rh = lax.broadcasted_iota(jnp.int32, (rows, V7X_LANES), 0) // n_new
        ln = lax.broadcasted_iota(jnp.int32, (rows, V7X_LANES), 1)
        qb = jnp.sum(jnp.where(rh == ln, tile_rows(f_new), 0.0), axis=1, keepdims=True)
        qb_sc[...] = qb
        f_pad = jnp.concatenate([f_new, jnp.zeros((PAGE_SIZE - n_new, V7X_LANES), F32)], axis=0)
        f_new_t = f_pad.T[:n_heads, :]
        tok = lax.broadcasted_iota(jnp.int32, (rows, PAGE_SIZE), 0) % n_new
        pos = lax.broadcasted_iota(jnp.int32, (rows, PAGE_SIZE), 1)
        bias = (qb - rep_rows(f_new_t)) + jnp.where(pos <= tok, 0.0, NEG_INF)
        p, alpha = score_phase(to_pages(kn_ref[...]).astype(BF16), bias)
        value_phase(to_pages(vn_ref[...]).astype(BF16), p, alpha)

    carry = carry_sc[...]
    biases = []
    for lf_ref in lf_refs:
        lf = lf_ref[...]
        incl = _scan_lanes(lf, reverse=True)
        biases.append(rep_rows((incl - lf) + carry))
        carry = carry + incl[:, 0:1]
    carry_sc[...] = carry
    k_pages = jnp.concatenate([r[...].reshape(feat, PAGE_SIZE).astype(BF16) for r in kc_refs], axis=1)
    p, alpha = score_phase(k_pages, qb_sc[...] + jnp.concatenate(biases, axis=1))
    if between is not None:
        between()
    v_pages = jnp.concatenate([r[...].reshape(feat, PAGE_SIZE).astype(BF16) for r in vc_refs], axis=1)
    value_phase(v_pages, p, alpha)

    @pl.when(cj == n_chunks - 1)
    def _():
        outs = [acc_sc[h * n_new:(h + 1) * n_new, h * HEAD_DIM:(h + 1) * HEAD_DIM] for h in range(n_heads)]
        o_ref[...] = jnp.concatenate(outs, axis=0) / l_sc[...]


def _decode_scratch(n_heads, n_new):
    rows = n_heads * n_new
    feat = n_heads * HEAD_DIM
    return [
        pltpu.VMEM((rows, 1), F32),
        pltpu.VMEM((rows, 1), F32),
        pltpu.VMEM((rows, feat), F32),
        pltpu.VMEM((n_heads, PAGE_SIZE), F32),
        pltpu.VMEM((rows, feat), BF16),
        pltpu.VMEM((rows, 1), F32),
    ]


def _decode_in_specs(seq_of, chunk_of, *, n_heads, n_new, n_pg, n_pages, cols, n_lead):
    feat = n_heads * HEAD_DIM
    last = n_pages - 1

    def tok_spec(width, col):
        return pl.BlockSpec((n_new, width), lambda *a: (seq_of(*a[:n_lead]), col))

    def page_spec(g, tail):
        def idx(*a):
            pt = a[n_lead]
            page = pt[seq_of(*a[:n_lead]), last - (chunk_of(*a[:n_lead]) * n_pg + g)]
            return (0, page) + (0,) * (1 + len(tail))
        return pl.BlockSpec((None, None, n_heads) + tail, idx)

    kv_tail = (HEAD_DIM, PAGE_SIZE)
    specs = [tok_spec(feat, cols["q"] // feat), tok_spec(feat, cols["k"] // feat), tok_spec(feat, cols["v"] // feat),
             tok_spec(V7X_LANES, 0)]
    specs += [page_spec(g, kv_tail) for g in range(n_pg)]
    specs += [page_spec(g, kv_tail) for g in range(n_pg)]
    specs += [page_spec(g, (PAGE_SIZE,)) for g in range(n_pg)]
    return specs


def _decode_kernel(pt_ref, q_ref, kn_ref, vn_ref, lfn_ref, *rest, n_heads, n_new, n_pg):
    del pt_ref
    kc_refs = rest[0:n_pg]
    vc_refs = rest[n_pg:2 * n_pg]
    lf_refs = rest[2 * n_pg:3 * n_pg]
    o_ref = rest[3 * n_pg]
    _decode_step(pl.program_id(1), pl.num_programs(1), q_ref, kn_ref, vn_ref, lfn_ref, kc_refs, vc_refs, lf_refs,
                 o_ref, rest[3 * n_pg + 1:], n_heads=n_heads, n_new=n_new)


def _decode_attention(page_table, hs, lf_new, cache_kt, cache_vt, cache_lft, *, n_heads, n_new, n_pg, cols,
                      seq0, n_seq):
    n_pages = page_table.shape[1]
    rows = n_heads * n_new
    kern = functools.partial(_decode_kernel, n_heads=n_heads, n_new=n_new, n_pg=n_pg)
    grid_spec = pltpu.PrefetchScalarGridSpec(
        num_scalar_prefetch=1,
        grid=(n_seq, n_pages // n_pg),
        in_specs=_decode_in_specs(lambda b, j: b + seq0, lambda b, j: j, n_heads=n_heads, n_new=n_new, n_pg=n_pg,
                                  n_pages=n_pages, cols=cols, n_lead=2),
        out_specs=pl.BlockSpec((None, rows, HEAD_DIM), lambda b, j, pt: (b, 0, 0)),
        scratch_shapes=_decode_scratch(n_heads, n_new),
    )
    args = [page_table, hs, hs, hs, lf_new] + [cache_kt] * n_pg + [cache_vt] * n_pg + [cache_lft] * n_pg
    return pl.pallas_call(
        kern,
        grid_spec=grid_spec,
        out_shape=jax.ShapeDtypeStruct((n_seq, rows, HEAD_DIM), F32),
        compiler_params=_cparams(2),
        name="decode_attention",
    )(*args)


def _h_decode_kernel(pt_ref, xn_ref, w_ref, q_ref, kn_ref, vn_ref, lfn_ref, *rest, n_heads, n_new, n_pg, n_j,
                     n_gate, n_chunks):
    del pt_ref
    kc_refs = rest[0:n_pg]
    vc_refs = rest[n_pg:2 * n_pg]
    lf_refs = rest[2 * n_pg:3 * n_pg]
    hg_ref, u_ref, o_ref = rest[3 * n_pg:3 * n_pg + 3]
    j = pl.program_id(1)
    cj = _divmod_nonneg(pl.program_id(0) * n_j + j, n_chunks)[1]

    def step(store):
        _decode_step(cj, n_chunks, q_ref, kn_ref, vn_ref, lfn_ref, kc_refs, vc_refs, lf_refs, o_ref,
                     rest[3 * n_pg + 3:], n_heads=n_heads, n_new=n_new,
                     between=lambda: store(_dot(xn_ref[...], w_ref[...])))

    def store_gate(acc):
        hg_ref[...] = acc.astype(BF16)

    def store_u(acc):
        u_ref[...] = acc

    pl.when(j < n_gate)(lambda: step(store_gate))
    pl.when(j >= n_gate)(lambda: step(store_u))


def _h_decode(plan, page_table, xn, wbf, hs, lf_new, cache_kt, cache_vt, cache_lft, *, tm, n_heads, n_new, n_pg):
    m, d = xn.shape
    tn = plan["tn"]
    cols = plan["cols"]
    n_j = plan["steps"]["q"][0]
    n_pages = page_table.shape[1]
    n_chunks = n_pages // n_pg
    n_steps = (m // tm) * n_j
    assert n_steps % n_chunks == 0 and n_steps // n_chunks <= page_table.shape[0]
    n_seq = n_steps // n_chunks
    rows = n_heads * n_new
    n_gate = plan["steps"]["u"][0]
    assert plan["steps"]["u"][1] == n_j
    kern = functools.partial(_h_decode_kernel, n_heads=n_heads, n_new=n_new, n_pg=n_pg, n_j=n_j, n_gate=n_gate,
                             n_chunks=n_chunks)
    in_specs = [
        pl.BlockSpec((tm, d), lambda i, j, pt: (i, 0)),
        pl.BlockSpec((None, d, tn), lambda i, j, pt: (j, 0, 0)),
    ] + _decode_in_specs(lambda i, j: _divmod_nonneg(i * n_j + j, n_chunks)[0],
                         lambda i, j: _divmod_nonneg(i * n_j + j, n_chunks)[1],
                         n_heads=n_heads, n_new=n_new, n_pg=n_pg, n_pages=n_pages, cols=cols, n_lead=2)
    grid_spec = pltpu.PrefetchScalarGridSpec(
        num_scalar_prefetch=1,
        grid=(m // tm, n_j),
        in_specs=in_specs,
        out_specs=[
            pl.BlockSpec((tm, tn), lambda i, j, pt: (i, jnp.minimum(j, n_gate - 1))),
            pl.BlockSpec((tm, tn), lambda i, j, pt: (i, jnp.maximum(j - n_gate, 0))),
            pl.BlockSpec((None, rows, HEAD_DIM), lambda i, j, pt: (_divmod_nonneg(i * n_j + j, n_chunks)[0], 0, 0)),
        ],
        scratch_shapes=_decode_scratch(n_heads, n_new),
    )
    args = [page_table, xn, wbf, hs, hs, hs, lf_new] + [cache_kt] * n_pg + [cache_vt] * n_pg + [cache_lft] * n_pg
    hg, u, att_rows = pl.pallas_call(
        kern,
        grid_spec=grid_spec,
        out_shape=[
            jax.ShapeDtypeStruct((m, n_gate * tn), BF16),
            jax.ShapeDtypeStruct((m, (n_j - n_gate) * tn), F32),
            jax.ShapeDtypeStruct((n_seq, rows, HEAD_DIM), F32),
        ],
        compiler_params=_cparams(2),
        name="h_decode",
    )(*args)
    return hg, u, att_rows, n_seq


def _pool_diff(z, u, pos, group_w):
    s = z
    k = 1
    while k < group_w:
        s = s + pltpu.roll(s, k, 0)
        k *= 2
    cnt = jnp.minimum(pos + 1, group_w).astype(F32)
    return s[HALO:, :] / cnt - u


def _mix_tail(d_groups, gp, ga, att, ma, mb, x, wpool_ref, ps_ref, wup_ref, wua_ref, wout_ref):
    mixed = [_dot(d.astype(BF16), wpool_ref[gi]) for gi, d in enumerate(d_groups)]
    mixed = jnp.concatenate(mixed, axis=1)
    branch_a = (mixed * ps_ref[...]) * _silu(gp.astype(F32))
    branch_b = att * _silu(ga.astype(F32))
    up_a = _dot(branch_a.astype(BF16), wup_ref[...])
    up_b = _dot(branch_b.astype(BF16), wua_ref[...])
    merged = _sigmoid(ma.astype(F32)) * up_a + _sigmoid(mb.astype(F32)) * up_b
    return x + _dot(merged.astype(BF16), wout_ref[...])


def _out_prompt_kernel(pt_ref, u_ref, halo_ref, gp_ref, ga_ref, ma_ref, mb_ref, att_ref, x_ref,
                       wpool_ref, ps_ref, wup_ref, wua_ref, wout_ref, q_ref, kn_ref, vn_ref, lfn_ref, *rest,
                       tm, seq, n_heads, n_new, n_pg):
    del pt_ref
    kc_refs = rest[0:n_pg]
    vc_refs = rest[n_pg:2 * n_pg]
    lf_refs = rest[2 * n_pg:3 * n_pg]
    y_ref, o_ref = rest[3 * n_pg:3 * n_pg + 2]
    i = pl.program_id(0)

    def output_stage():
        pos0 = (i * tm) % seq
        u = u_ref[...]
        halo = jnp.where(pos0 == 0, 0.0, halo_ref[...])
        z = jnp.concatenate([halo, u], axis=0)
        pos = pos0 + lax.broadcasted_iota(jnp.int32, (tm, 1), 0)
        gw = u.shape[1] // len(POOL_WINDOWS)
        d_groups = [_pool_diff(z[:, gi * gw:(gi + 1) * gw], u[:, gi * gw:(gi + 1) * gw], pos, w)
                    for gi, w in enumerate(POOL_WINDOWS)]
        y_ref[...] = _mix_tail(d_groups, gp_ref[...], ga_ref[...], att_ref[...], ma_ref[...], mb_ref[...], x_ref[...],
                               wpool_ref, ps_ref, wup_ref, wua_ref, wout_ref)

    _decode_step(i, pl.num_programs(0), q_ref, kn_ref, vn_ref, lfn_ref, kc_refs, vc_refs, lf_refs, o_ref,
                 rest[3 * n_pg + 2:], n_heads=n_heads, n_new=n_new, between=output_stage)


def _out_sample_kernel(z_ref, gp_ref, ga_ref, ma_ref, mb_ref, att_ref, x_ref,
                       wpool_ref, ps_ref, wup_ref, wua_ref, wout_ref, y_ref, *, dec_b, n_new, pos0):
    gw = z_ref.shape[2] // len(POOL_WINDOWS)
    pos = pos0 + lax.broadcasted_iota(jnp.int32, (n_new, 1), 0)
    per_seq = []
    for b in range(dec_b):
        z = z_ref[b]
        u = z[HALO:, :]
        per_seq.append([_pool_diff(z[:, gi * gw:(gi + 1) * gw], u[:, gi * gw:(gi + 1) * gw], pos, w)
                        for gi, w in enumerate(POOL_WINDOWS)])
    d_groups = [jnp.concatenate([per_seq[b][gi] for b in range(dec_b)], axis=0) for gi in range(len(POOL_WINDOWS))]
    y_ref[...] = _mix_tail(d_groups, gp_ref[...], ga_ref[...], att_ref[...], ma_ref[...], mb_ref[...], x_ref[...],
                           wpool_ref, ps_ref, wup_ref, wua_ref, wout_ref)


def _weight_specs(pw, gw, dm):
    one = pl.Buffered(1)
    return [
        pl.BlockSpec((len(POOL_WINDOWS), gw, gw), lambda *_: (0, 0, 0), pipeline_mode=one),
        pl.BlockSpec((1, pw), lambda *_: (0, 0), pipeline_mode=one),
        pl.BlockSpec((pw, dm), lambda *_: (0, 0), pipeline_mode=one),
        pl.BlockSpec((pw, dm), lambda *_: (0, 0), pipeline_mode=one),
        pl.BlockSpec((dm, dm), lambda *_: (0, 0), pipeline_mode=one),
    ]


def _out_prompt(hg, u, att, x2d, wpool, ps, wup, wua, wout, page_table, hs, lf_new, cache_kt, cache_vt, cache_lft,
                *, seq, tm, cols, n_heads, n_new, decode_seq):
    m, dm = x2d.shape
    pw = att.shape[1]
    gw = pw // len(POOL_WINDOWS)
    hb = tm // HALO
    n_steps = m // tm
    n_pages = page_table.shape[1]
    assert n_pages % n_steps == 0
    n_pg = n_pages // n_steps
    rows = n_heads * n_new
    kern = functools.partial(_out_prompt_kernel, tm=tm, seq=seq, n_heads=n_heads, n_new=n_new, n_pg=n_pg)
    in_specs = [
        pl.BlockSpec((tm, pw), lambda i, pt: (i, 0)),
        pl.BlockSpec((HALO, pw), lambda i, pt: (jnp.maximum(i * hb - 1, 0), 0)),
        pl.BlockSpec((tm, pw), lambda i, pt: (i, cols["gp"] // pw)),
        pl.BlockSpec((tm, pw), lambda i, pt: (i, cols["ga"] // pw)),
        pl.BlockSpec((tm, dm), lambda i, pt: (i, cols["ma"] // dm)),
        pl.BlockSpec((tm, dm), lambda i, pt: (i, cols["mb"] // dm)),
        pl.BlockSpec((tm, pw), lambda i, pt: (i, 0)),
        pl.BlockSpec((tm, dm), lambda i, pt: (i, 0)),
    ] + _weight_specs(pw, gw, dm) + _decode_in_specs(
        lambda i: decode_seq, lambda i: i, n_heads=n_heads, n_new=n_new, n_pg=n_pg, n_pages=n_pages, cols=cols,
        n_lead=1)
    grid_spec = pltpu.PrefetchScalarGridSpec(
        num_scalar_prefetch=1,
        grid=(n_steps,),
        in_specs=in_specs,
        out_specs=[
            pl.BlockSpec((tm, dm), lambda i, pt: (i, 0)),
            pl.BlockSpec((None, rows, HEAD_DIM), lambda i, pt: (0, 0, 0)),
        ],
        scratch_shapes=_decode_scratch(n_heads, n_new),
    )
    args = [page_table, u, u, hg, hg, hg, hg, att, x2d, wpool, ps, wup, wua, wout, hs, hs, hs, lf_new]
    args += [cache_kt] * n_pg + [cache_vt] * n_pg + [cache_lft] * n_pg
    return pl.pallas_call(
        kern,
        grid_spec=grid_spec,
        out_shape=[
            jax.ShapeDtypeStruct((m, dm), F32),
            jax.ShapeDtypeStruct((1, rows, HEAD_DIM), F32),
        ],
        compiler_params=_cparams(1),
        name="out_prompt",
    )(*args)


def _out_sample(z, h, att, x2d, wpool, ps, wup, wua, wout, *, dec_b, n_new, pos0, cols):
    m, dm = x2d.shape
    pw = att.shape[1]
    gw = pw // len(POOL_WINDOWS)
    kern = functools.partial(_out_sample_kernel, dec_b=dec_b, n_new=n_new, pos0=pos0)
    return pl.pallas_call(
        kern,
        grid=(1,),
        in_specs=[
            pl.BlockSpec((dec_b, HALO + n_new, pw), lambda i: (0, 0, 0)),
            pl.BlockSpec((m, pw), lambda i: (0, cols["gp"] // pw)),
            pl.BlockSpec((m, pw), lambda i: (0, cols["ga"] // pw)),
            pl.BlockSpec((m, dm), lambda i: (0, cols["ma"] // dm)),
            pl.BlockSpec((m, dm), lambda i: (0, cols["mb"] // dm)),
            pl.BlockSpec((m, pw), lambda i: (0, 0)),
            pl.BlockSpec((m, dm), lambda i: (0, 0)),
        ] + _weight_specs(pw, gw, dm),
        out_specs=pl.BlockSpec((m, dm), lambda i: (0, 0)),
        out_shape=jax.ShapeDtypeStruct((m, dm), F32),
        compiler_params=_cparams(1),
        name="out_sample",
    )(z, h, h, h, h, att, x2d, wpool, ps, wup, wua, wout)


def kernel(x_prompt, x_sample, cache_k, cache_v, cache_logf, state_pool, page_table, norm_gain, w_in, b_f,
           q_norm_gain, k_norm_gain, w_pool_map, pool_scale, w_up_pool, w_up_attn, w_out):
    batch, seq, dm = x_prompt.shape
    dec_b, n_new, _ = x_sample.shape
    assert w_in.shape[0] == 1
    n_heads = b_f.shape[1]
    aw = n_heads * HEAD_DIM
    pw = w_up_pool.shape[1]
    n_pages = page_table.shape[1]
    past_len = n_pages * PAGE_SIZE
    assert pw == aw and dm == 2 * pw and n_pages % PAGES_PER_STEP == 0

    tn = 512
    plan = _inproj_plan(pw, aw, dm, n_heads, tn)
    cols = plan["cols"]
    w_t = w_in[0].T
    fl0, fl_n = plan["fl_rows"]
    wfl = jnp.pad(w_t[fl0:fl0 + fl_n], ((0, V7X_LANES - n_heads), (0, 0))).astype(BF16)
    bfp = jnp.pad(b_f, ((0, 0), (0, V7X_LANES - n_heads)))
    reps = tn // HEAD_DIM
    qg = jnp.tile(q_norm_gain[0], reps)[None, :]
    kg = jnp.tile(k_norm_gain[0], reps)[None, :]
    qgc = jnp.broadcast_to(jnp.tile(q_norm_gain[0], n_heads)[:, None], (aw, V7X_LANES))
    kgc = jnp.broadcast_to(jnp.tile(k_norm_gain[0], n_heads)[:, None], (aw, V7X_LANES))
    seg = np.arange(tn) // HEAD_DIM
    bd = jnp.asarray((seg[:, None] == seg[None, :]).astype(np.float32)).astype(BF16)
    wpool = w_pool_map[0].astype(BF16)
    wup = w_up_pool[0].astype(BF16)
    wua = w_up_attn[0].astype(BF16)
    wout = w_out[0].astype(BF16)

    ms = dec_b * n_new
    xs2 = x_sample.reshape(ms, dm)
    hs, lfs, w_tok, w_fm = _inproj_new(plan, xs2, norm_gain, w_t, wfl, bfp, qg, kg, bd)

    cache_kt = cache_k.transpose(0, 1, 3, 4, 2)
    cache_vt = cache_v.transpose(0, 1, 3, 4, 2)
    cache_lft = cache_logf.transpose(0, 1, 3, 2)
    decode_args = dict(n_heads=n_heads, n_new=n_new, n_pg=DECODE_PAGES_FUSED)

    xp2 = x_prompt.reshape(batch * seq, dm)
    xn_p, lfp = _prenorm(xp2, norm_gain, wfl, bfp, tm=1024)
    hg_p, u_p, att_rows_a, n_fused = _h_decode(plan, page_table, xn_p, w_tok, hs, lfs, cache_kt, cache_vt,
                                               cache_lft, tm=1024, **decode_args)
    qt_p, kt_p, vt_p = _inproj_fm(plan, xn_p, w_fm, qgc, kgc, tm=1024, seq=seq, width=aw)
    fcol, ft, lft = _fcum(lfp, batch=batch, seq=seq, n_heads=n_heads)
    ft_pairs = ft.reshape(batch, n_heads // 2, 2, seq)
    att_p = _prompt_attention(qt_p, kt_p, vt_p, fcol, ft_pairs, batch=batch, seq=seq, n_heads=n_heads, tq=256,
                              n_pair=ATTN_PAIRS_PER_STEP)
    assert n_fused == dec_b - 1
    yp, att_rows_b = _out_prompt(hg_p, u_p, att_p, xp2, wpool, pool_scale, wup, wua, wout, page_table, hs, lfs,
                                 cache_kt, cache_vt, cache_lft, seq=seq, tm=256, cols=cols, n_heads=n_heads,
                                 n_new=n_new, decode_seq=n_fused)

    k_s = hs[:, cols["k"]:cols["k"] + aw].reshape(dec_b, n_new, n_heads, HEAD_DIM)
    v_s = hs[:, cols["v"]:cols["v"] + aw].reshape(dec_b, n_new, n_heads, HEAD_DIM)
    u_s = hs[:, cols["u"]:cols["u"] + pw].reshape(dec_b, n_new, pw)
    logf_s = lfs[:, :n_heads].reshape(dec_b, n_new, n_heads)
    att_rows = jnp.concatenate([att_rows_a, att_rows_b], axis=0)
    att_s = att_rows.reshape(dec_b, n_heads, n_new, HEAD_DIM).transpose(0, 2, 1, 3).reshape(ms, aw)
    z_s = jnp.concatenate([jnp.zeros((dec_b, HALO - POOL_BUF, pw), F32), state_pool[0], u_s], axis=1)
    ys = _out_sample(z_s, hs, att_s, xs2, wpool, pool_scale, wup, wua, wout,
                     dec_b=dec_b, n_new=n_new, pos0=past_len, cols=cols)

    k_p = kt_p.reshape(1, batch, n_heads, HEAD_DIM, seq).transpose(0, 1, 4, 2, 3)
    v_p = vt_p.reshape(1, batch, n_heads, HEAD_DIM, seq).transpose(0, 1, 4, 2, 3)
    logf_p = lft.transpose(0, 2, 1)[None]
    pool_p = u_p.reshape(batch, seq, pw)[:, seq - POOL_BUF:, :][None]
    pool_s = z_s[:, HALO + n_new - POOL_BUF:, :][None]
    return (yp.reshape(batch, seq, dm), ys.reshape(dec_b, n_new, dm), k_p, v_p, logf_p, pool_p,
            k_s[None], v_s[None], logf_s[None], pool_s)
```

```python
import functools

import jax
import jax.numpy as jnp
import numpy as np
from jax import lax
from jax.experimental import pallas as pl
from jax.experimental.pallas import tpu as pltpu

F32 = jnp.float32
BF16 = jnp.bfloat16

HEAD_DIM = 64
POOL_WINDOWS = (2, 4, 8, 16)
POOL_BUF = 15
PAGE_SIZE = 128
EPS = 1e-6
NEG_INF = -1e30
ATTN_SCALE = HEAD_DIM ** -0.5
LOG2E = 1.4426950408889634

V7X_LANES = 128
V7X_SUBLANES = 8
V7X_VMEM_LIMIT_BYTES = 56 * 1024 * 1024

ROW_ALIGN = 16
HALO = 16
DECODE_PAGES_FUSED = 8
N_AUG = 3
ATTN_PAIRS_PER_STEP = 8


def _cparams(n_grid_axes):
    return pltpu.CompilerParams(
        dimension_semantics=("arbitrary",) * n_grid_axes,
        vmem_limit_bytes=V7X_VMEM_LIMIT_BYTES,
    )


def _split3(x):
    hi = x.astype(BF16)
    r1 = x - hi.astype(F32)
    mid = r1.astype(BF16)
    lo = (r1 - mid.astype(F32)).astype(BF16)
    return hi, mid, lo


def _dot(a, b):
    return jnp.dot(a, b, preferred_element_type=F32)


def _dot_nt(a, b):
    return lax.dot_general(a, b, (((1,), (1,)), ((), ())), preferred_element_type=F32)


def _sigmoid(x):
    return 1.0 / (1.0 + jnp.exp(-x))


def _silu(x):
    return x * _sigmoid(x)


def _scan_lanes(x, *, reverse):
    n = x.shape[-1]
    ax = x.ndim - 1
    lane = lax.broadcasted_iota(jnp.int32, x.shape, ax)
    k = 1
    while k < n:
        if reverse:
            shifted = pltpu.roll(x, n - k, ax)
            x = x + jnp.where(lane < n - k, shifted, 0.0)
        else:
            shifted = pltpu.roll(x, k, ax)
            x = x + jnp.where(lane >= k, shifted, 0.0)
        k *= 2
    return x


def _divmod_nonneg(x, n):
    assert n > 0 and n & (n - 1) == 0
    return lax.shift_right_logical(x, n.bit_length() - 1), x & (n - 1)


def _scan_rows(x):
    n = x.shape[0]
    row = lax.broadcasted_iota(jnp.int32, x.shape, 0)
    k = 1
    while k < n:
        x = x + jnp.where(row >= k, pltpu.roll(x, k, 0), 0.0)
        k *= 2
    return x


def _norm_and_logf(x_ref, ng_ref, wfl_ref, bf_ref, xn_sc, lf_ref):
    x = x_ref[...]
    ms = jnp.mean(x * x, axis=-1, keepdims=True)
    xn = x * lax.rsqrt(ms + EPS) * ng_ref[...]
    xn_sc[...] = xn.astype(BF16)
    z = _dot_nt(xn_sc[...], wfl_ref[...]) + bf_ref[...]
    lf_ref[...] = jnp.minimum(z, 0.0) - jnp.log1p(jnp.exp(-jnp.abs(z)))


def _step_in(j, rng):
    return jnp.logical_and(j >= rng[0], j < rng[1])


def _inproj_new_kernel(off_ref, x_ref, ng_ref, w_ref, wfl_ref, bf_ref, qg_ref, kg_ref, bd_ref,
                       h_ref, lf_ref, wtok_ref, wfm_ref, xn_sc, *, steps):
    del off_ref
    j = pl.program_id(0)
    n_tok = steps["q"][0]

    @pl.when(j == 0)
    def _():
        _norm_and_logf(x_ref, ng_ref, wfl_ref, bf_ref, xn_sc, lf_ref)

    w = w_ref[...].astype(BF16)
    acc = _dot_nt(xn_sc[...], w)

    @pl.when(j < n_tok)
    def _():
        wtok_ref[...] = w_ref[...].T.astype(BF16)

    @pl.when(j >= n_tok)
    def _():
        wfm_ref[...] = w

    def head_norm(gain):
        sq = acc * acc
        hi = sq.astype(BF16)
        lo = (sq - hi.astype(F32)).astype(BF16)
        ss = _dot(hi, bd_ref[...]) + _dot(lo, bd_ref[...])
        return acc * lax.rsqrt(ss * (1.0 / HEAD_DIM) + EPS) * gain

    is_q = _step_in(j, steps["q"])
    is_k = _step_in(j, steps["k"])

    @pl.when(is_q)
    def _():
        h_ref[...] = head_norm(qg_ref[...])

    @pl.when(is_k)
    def _():
        h_ref[...] = head_norm(kg_ref[...])

    @pl.when(jnp.logical_not(is_q | is_k))
    def _():
        h_ref[...] = acc


def _inproj_new(plan, x2d, ng, w_t, wfl, bfp, qg, kg, bd):
    m, d = x2d.shape
    tn = plan["tn"]
    nj = plan["n_steps"]
    kern = functools.partial(_inproj_new_kernel, steps=plan["steps"])
    n_tok = plan["steps"]["q"][0]
    grid_spec = pltpu.PrefetchScalarGridSpec(
        num_scalar_prefetch=1,
        grid=(nj,),
        in_specs=[
            pl.BlockSpec((m, d), lambda j, off: (0, 0)),
            pl.BlockSpec((1, d), lambda j, off: (0, 0)),
            pl.BlockSpec((pl.Element(tn), pl.Element(d)), lambda j, off: (off[j] * ROW_ALIGN, 0)),
            pl.BlockSpec((V7X_LANES, d), lambda j, off: (0, 0)),
            pl.BlockSpec((1, V7X_LANES), lambda j, off: (0, 0)),
            pl.BlockSpec((1, tn), lambda j, off: (0, 0)),
            pl.BlockSpec((1, tn), lambda j, off: (0, 0)),
            pl.BlockSpec((tn, tn), lambda j, off: (0, 0)),
        ],
        out_specs=[
            pl.BlockSpec((m, tn), lambda j, off: (0, j)),
            pl.BlockSpec((m, V7X_LANES), lambda j, off: (0, 0)),
            pl.BlockSpec((None, d, tn), lambda j, off: (jnp.minimum(j, n_tok - 1), 0, 0)),
            pl.BlockSpec((tn, d), lambda j, off: (jnp.maximum(j - n_tok, 0), 0)),
        ],
        scratch_shapes=[pltpu.VMEM((m, d), BF16)],
    )
    return pl.pallas_call(
        kern,
        grid_spec=grid_spec,
        out_shape=[
            jax.ShapeDtypeStruct((m, nj * tn), F32),
            jax.ShapeDtypeStruct((m, V7X_LANES), F32),
            jax.ShapeDtypeStruct((n_tok, d, tn), BF16),
            jax.ShapeDtypeStruct(((nj - n_tok) * tn, d), BF16),
        ],
        compiler_params=_cparams(1),
        name="inproj_new",
    )(plan["row_offsets"], x2d, ng, w_t, wfl, bfp, qg, kg, bd)


def _prenorm_kernel(x_ref, ng_ref, wfl_ref, bf_ref, xn_ref, lf_ref):
    _norm_and_logf(x_ref, ng_ref, wfl_ref, bf_ref, xn_ref, lf_ref)


def _prenorm(x2d, ng, wfl, bfp, *, tm):
    m, d = x2d.shape
    return pl.pallas_call(
        _prenorm_kernel,
        grid=(m // tm,),
        in_specs=[
            pl.BlockSpec((tm, d), lambda i: (i, 0)),
            pl.BlockSpec((1, d), lambda i: (0, 0)),
            pl.BlockSpec((V7X_LANES, d), lambda i: (0, 0)),
            pl.BlockSpec((1, V7X_LANES), lambda i: (0, 0)),
        ],
        out_specs=[
            pl.BlockSpec((tm, d), lambda i: (i, 0)),
            pl.BlockSpec((tm, V7X_LANES), lambda i: (i, 0)),
        ],
        out_shape=[
            jax.ShapeDtypeStruct((m, d), BF16),
            jax.ShapeDtypeStruct((m, V7X_LANES), F32),
        ],
        compiler_params=_cparams(1),
        name="prenorm",
    )(x2d, ng, wfl, bfp)


def _inproj_fm_kernel(xn_ref, w_ref, qgc_ref, kgc_ref, qt_ref, kt_ref, vt_ref):
    j = pl.program_id(1)

    def head_norm(acc, gain_col_ref):
        tn, tm = acc.shape
        a3 = acc.reshape(tn // HEAD_DIM, HEAD_DIM, tm)
        ms = jnp.mean(a3 * a3, axis=1, keepdims=True)
        g3 = gain_col_ref[...].reshape(tn // HEAD_DIM, HEAD_DIM, V7X_LANES)[:, :, 0:1]
        return (a3 * lax.rsqrt(ms + EPS) * g3).reshape(tn, tm)

    @pl.when(j == 0)
    def _():
        qt_ref[...] = head_norm(_dot_nt(w_ref[...], xn_ref[...]), qgc_ref)

    @pl.when(j == 1)
    def _():
        kt_ref[...] = head_norm(_dot_nt(w_ref[...], xn_ref[...]), kgc_ref)

    @pl.when(j == 2)
    def _():
        vt_ref[...] = _dot_nt(w_ref[...], xn_ref[...])


def _inproj_fm(plan, xn, wbf, qgc, kgc, *, tm, seq, width):
    m, d = xn.shape
    assert wbf.shape[0] == 3 * width
    nb = seq // tm
    out_spec = pl.BlockSpec((None, width, tm), lambda i, j: (_divmod_nonneg(i, nb)[0], 0, _divmod_nonneg(i, nb)[1]))
    return pl.pallas_call(
        _inproj_fm_kernel,
        grid=(m // tm, 3),
        in_specs=[
            pl.BlockSpec((tm, d), lambda i, j: (i, 0)),
            pl.BlockSpec((width, d), lambda i, j: (j, 0)),
            pl.BlockSpec((width, V7X_LANES), lambda i, j: (0, 0)),
            pl.BlockSpec((width, V7X_LANES), lambda i, j: (0, 0)),
        ],
        out_specs=[out_spec] * 3,
        out_shape=[jax.ShapeDtypeStruct((m // seq, width, seq), F32)] * 3,
        compiler_params=_cparams(2),
        name="inproj_fm",
    )(xn, wbf, qgc, kgc)


def _inproj_plan(pw, aw, dm, n_heads, tn):
    src, off = {}, 0
    for name, size in (("u", pw), ("gp", pw), ("q", aw), ("k", aw), ("v", aw), ("ga", aw), ("fl", n_heads),
                       ("ma", dm), ("mb", dm)):
        src[name] = (off, size)
        off += size
    order = ("ma", "mb", "gp", "ga", "u", "q", "k", "v")
    row_offsets, steps, cols = [], {}, {}
    for name in order:
        base, size = src[name]
        steps[name] = (len(row_offsets), len(row_offsets) + size // tn)
        cols[name] = len(row_offsets) * tn
        row_offsets += [base + o for o in range(0, size, tn)]
    assert all(r % ROW_ALIGN == 0 for r in row_offsets)
    return {"tn": tn, "steps": steps, "n_steps": len(row_offsets), "cols": cols, "fl_rows": src["fl"],
            "row_offsets": jnp.asarray([r // ROW_ALIGN for r in row_offsets], jnp.int32)}


def _fcum_kernel(lf_ref, fcol_ref, ft_ref, lft_ref, *, seq, n_heads):
    c = V7X_LANES
    row = lax.broadcasted_iota(jnp.int32, (c, c), 0)
    col = lax.broadcasted_iota(jnp.int32, (c, c), 1)
    tri = jnp.where(col <= row, 1.0, 0.0).astype(BF16)
    carry = jnp.zeros((1, c), F32)
    for ci in range(seq // c):
        x = lf_ref[ci * c:(ci + 1) * c, :]
        hi, mid, lo = _split3(x)
        fc = (_dot(tri, hi) + _dot(tri, mid)) + _dot(tri, lo) + carry
        fcol_ref[ci * c:(ci + 1) * c, :] = fc
        ft_ref[:, ci * c:(ci + 1) * c] = fc.T[:n_heads, :]
        lft_ref[:, ci * c:(ci + 1) * c] = x.T[:n_heads, :]
        carry = fc[c - 1:c, :]


def _fcum(lf2d, *, batch, seq, n_heads):
    kern = functools.partial(_fcum_kernel, seq=seq, n_heads=n_heads)
    return pl.pallas_call(
        kern,
        grid=(batch,),
        in_specs=[pl.BlockSpec((seq, V7X_LANES), lambda b: (b, 0))],
        out_specs=[
            pl.BlockSpec((seq, V7X_LANES), lambda b: (b, 0)),
            pl.BlockSpec((None, n_heads, seq), lambda b: (b, 0, 0)),
            pl.BlockSpec((None, n_heads, seq), lambda b: (b, 0, 0)),
        ],
        out_shape=[
            jax.ShapeDtypeStruct((batch * seq, V7X_LANES), F32),
            jax.ShapeDtypeStruct((batch, n_heads, seq), F32),
            jax.ShapeDtypeStruct((batch, n_heads, seq), F32),
        ],
        compiler_params=_cparams(1),
        name="forget_cumsum",
    )(lf2d)


def _attn_kernel(qt_ref, kt_ref, vt_ref, fcol_ref, ft_ref, o_ref, ka_sc, *, tq, seq, n_pair):
    g = pl.program_id(1)
    i = pl.program_id(2)
    d = HEAD_DIM
    lanes = V7X_LANES
    lane1 = lax.broadcasted_iota(jnp.int32, (1, lanes), 1)

    @pl.when(i == 0)
    def _():
        prow = lax.broadcasted_iota(jnp.int32, (N_AUG * lanes, lanes), 0)
        lane = lax.broadcasted_iota(jnp.int32, (N_AUG * lanes, lanes), 1)
        piece, hrow = prow // lanes, prow % lanes
        ones = jnp.where(((lane1 >= d) & (lane1 < d + N_AUG)) | (lane1 < N_AUG), 1.0, 0.0)
        sels = []
        for pr in range(n_pair):
            head0 = 2 * (g * n_pair + pr)
            hit = ((hrow == head0) & (lane == d + N_AUG + piece)) | ((hrow == head0 + 1) & (lane == N_AUG + piece))
            sels.append(jnp.where(hit, -1.0, 0.0).astype(BF16))
        ck = 256
        for c in range(seq // ck):
            pieces = jnp.concatenate(_split3(fcol_ref[c * ck:(c + 1) * ck, :] * LOG2E), axis=1)
            for pr in range(n_pair):
                k_rows = kt_ref[pr * lanes:(pr + 1) * lanes, c * ck:(c + 1) * ck].T
                aug = ones + _dot(pieces, sels[pr])
                ka_sc[2 * pr, c * ck:(c + 1) * ck, :] = jnp.where(lane1 < d, k_rows, aug).astype(BF16)
                ka_sc[2 * pr + 1, c * ck:(c + 1) * ck, :] = jnp.where(lane1 >= d, k_rows, aug).astype(BF16)

    qs = pl.multiple_of(i * tq, tq)
    rowi = lax.broadcasted_iota(jnp.int32, (d, tq), 0)

    def aug_rows(f_row):
        hi, mid, lo = (x.astype(F32) for x in _split3(f_row))
        return jnp.where(rowi == 0, hi, jnp.where(rowi == 1, mid, jnp.where(rowi == 2, lo,
                         jnp.where(rowi < 2 * N_AUG, 1.0, 0.0))))

    qa = []
    for pr in range(n_pair):
        qt = qt_ref[pr * lanes:(pr + 1) * lanes, :] * (ATTN_SCALE * LOG2E)
        fq = ft_ref[pr, :, pl.ds(qs, tq)] * LOG2E
        qa.append(jnp.concatenate([qt[:d, :], aug_rows(fq[0:1, :])], axis=0).astype(BF16))
        qa.append(jnp.concatenate([aug_rows(fq[1:2, :]), qt[d:, :]], axis=0).astype(BF16))
    krow = lax.broadcasted_iota(jnp.int32, (tq, tq), 0)
    qcol = lax.broadcasted_iota(jnp.int32, (tq, tq), 1)
    causal = krow <= qcol

    def step(j, carry, masked):
        ks = pl.multiple_of(j * tq, tq)
        heads = range(2 * n_pair)
        scores = [_dot(ka_sc[hh, pl.ds(ks, tq), :], qa[hh]) for hh in heads]
        stats, probs = [], []
        for hh in heads:
            m, l, _ = carry[hh]
            s = jnp.where(causal, scores[hh], NEG_INF) if masked else scores[hh]
            m_new = jnp.maximum(m, jnp.max(s, axis=0, keepdims=True))
            alpha = jnp.exp2(m - m_new)
            pt = jnp.exp2(s - m_new)
            stats.append((m_new, alpha * l + jnp.sum(pt, axis=0, keepdims=True), alpha))
            probs.append(pt.astype(BF16))
        pvs = [_dot(vt_ref[hh * d:(hh + 1) * d, pl.ds(ks, tq)].astype(BF16), probs[hh]) for hh in heads]
        return tuple((stats[hh][0], stats[hh][1], stats[hh][2] * carry[hh][2] + pvs[hh]) for hh in heads)

    init = tuple((jnp.full((1, tq), NEG_INF, F32), jnp.zeros((1, tq), F32), jnp.zeros((d, tq), F32))
                 for _ in range(2 * n_pair))
    carry = lax.fori_loop(0, i, lambda j, c: step(j, c, False), init)
    final = step(i, carry, True)
    o_ref[...] = jnp.concatenate([acc / l for (_, l, acc) in final], axis=0).T.astype(o_ref.dtype)


def _prompt_attention(qt, kt, vt, fcol, ft_pairs, *, batch, seq, n_heads, tq, n_pair):
    n_groups = n_heads // (2 * n_pair)
    qn = seq // tq
    rows = n_pair * V7X_LANES
    kern = functools.partial(_attn_kernel, tq=tq, seq=seq, n_pair=n_pair)
    return pl.pallas_call(
        kern,
        grid=(batch, n_groups, qn),
        in_specs=[
            pl.BlockSpec((None, rows, tq), lambda b, g, i: (b, g, i)),
            pl.BlockSpec((None, rows, seq), lambda b, g, i: (b, g, 0)),
            pl.BlockSpec((None, rows, seq), lambda b, g, i: (b, g, 0)),
            pl.BlockSpec((seq, V7X_LANES), lambda b, g, i: (b, 0)),
            pl.BlockSpec((None, n_pair, 2, seq), lambda b, g, i: (b, g, 0, 0)),
        ],
        out_specs=pl.BlockSpec((tq, rows), lambda b, g, i: (b * qn + i, g)),
        out_shape=jax.ShapeDtypeStruct((batch * seq, n_heads * HEAD_DIM), BF16),
        scratch_shapes=[pltpu.VMEM((2 * n_pair, seq, V7X_LANES), BF16)],
        compiler_params=_cparams(3),
        name="prompt_attention",
    )(qt, kt, vt, fcol, ft_pairs)


def _decode_step(cj, n_chunks, q_ref, kn_ref, vn_ref, lfn_ref, kc_refs, vc_refs, lf_refs, o_ref, scratch,
                 *, n_heads, n_new, between=None):
    m_sc, l_sc, acc_sc, carry_sc, qbd_sc, qb_sc = scratch
    rows = n_heads * n_new
    feat = n_heads * HEAD_DIM

    def tile_rows(x):
        return jnp.concatenate([x] * n_heads, axis=0)

    def rep_rows(x):
        return jnp.concatenate([jnp.broadcast_to(x[h:h + 1, :], (n_new, x.shape[1])) for h in range(n_heads)], axis=0)

    def to_pages(x):
        xp = jnp.concatenate([x, jnp.zeros((PAGE_SIZE - n_new, feat), F32)], axis=0)
        return jnp.concatenate([xp[:, c * PAGE_SIZE:(c + 1) * PAGE_SIZE].T for c in range(feat // PAGE_SIZE)], axis=0)

    def score_phase(k_pages, bias):
        s = _dot(qbd_sc[...], k_pages) + bias
        m_prev = m_sc[...]
        m_new = jnp.maximum(m_prev, jnp.max(s, axis=1, keepdims=True))
        alpha = jnp.exp(m_prev - m_new)
        p = jnp.exp(s - m_new)
        l_sc[...] = alpha * l_sc[...] + jnp.sum(p, axis=1, keepdims=True)
        m_sc[...] = m_new
        return p, alpha

    def value_phase(v_pages, p, alpha):
        acc_sc[...] = alpha * acc_sc[...] + _dot_nt(p.astype(BF16), v_pages)

    @pl.when(cj == 0)
    def _():
        m_sc[...] = jnp.full(m_sc.shape, NEG_INF, F32)
        l_sc[...] = jnp.zeros(l_sc.shape, F32)
        acc_sc[...] = jnp.zeros(acc_sc.shape, F32)
        carry_sc[...] = jnp.zeros(carry_sc.shape, F32)
        row_head = lax.broadcasted_iota(jnp.int32, (rows, feat), 0) // n_new
        col_head = lax.broadcasted_iota(jnp.int32, (rows, feat), 1) // HEAD_DIM
        qbd_sc[...] = jnp.where(row_head == col_head, tile_rows(q_ref[...] * ATTN_SCALE), 0.0).astype(BF16)
        f_new = _scan_rows(lfn_ref[...])
        rh = lax.broadcasted_iota(jnp.int32, (rows, V7X_LANES), 0) // n_new
        ln = lax.broadcasted_iota(jnp.int32, (rows, V7X_LANES), 1)
        qb = jnp.sum(jnp.where(rh == ln, tile_rows(f_new), 0.0), axis=1, keepdims=True)
        qb_sc[...] = qb
        f_pad = jnp.concatenate([f_new, jnp.zeros((PAGE_SIZE - n_new, V7X_LANES), F32)], axis=0)
        f_new_t = f_pad.T[:n_heads, :]
        tok = lax.broadcasted_iota(jnp.int32, (rows, PAGE_SIZE), 0) % n_new
        pos = lax.broadcasted_iota(jnp.int32, (rows, PAGE_SIZE), 1)
        bias = (qb - rep_rows(f_new_t)) + jnp.where(pos <= tok, 0.0, NEG_INF)
        p, alpha = score_phase(to_pages(kn_ref[...]).astype(BF16), bias)
        value_phase(to_pages(vn_ref[...]).astype(BF16), p, alpha)

    carry = carry_sc[...]
    biases = []
    for lf_ref in lf_refs:
        lf = lf_ref[...]
        incl = _scan_lanes(lf, reverse=True)
        biases.append(rep_rows((incl - lf) + carry))
        carry = carry + incl[:, 0:1]
    carry_sc[...] = carry
    k_pages = jnp.concatenate([r[...].reshape(feat, PAGE_SIZE).astype(BF16) for r in kc_refs], axis=1)
    p, alpha = score_phase(k_pages, qb_sc[...] + jnp.concatenate(biases, axis=1))
    if between is not None:
        between()
    v_pages = jnp.concatenate([r[...].reshape(feat, PAGE_SIZE).astype(BF16) for r in vc_refs], axis=1)
    value_phase(v_pages, p, alpha)

    @pl.when(cj == n_chunks - 1)
    def _():
        outs = [acc_sc[h * n_new:(h + 1) * n_new, h * HEAD_DIM:(h + 1) * HEAD_DIM] for h in range(n_heads)]
        o_ref[...] = jnp.concatenate(outs, axis=0) / l_sc[...]


def _decode_scratch(n_heads, n_new):
    rows = n_heads * n_new
    feat = n_heads * HEAD_DIM
    return [
        pltpu.VMEM((rows, 1), F32),
        pltpu.VMEM((rows, 1), F32),
        pltpu.VMEM((rows, feat), F32),
        pltpu.VMEM((n_heads, PAGE_SIZE), F32),
        pltpu.VMEM((rows, feat), BF16),
        pltpu.VMEM((rows, 1), F32),
    ]


def _decode_in_specs(seq_of, chunk_of, *, n_heads, n_new, n_pg, n_pages, cols, n_lead):
    feat = n_heads * HEAD_DIM
    last = n_pages - 1

    def tok_spec(width, col):
        return pl.BlockSpec((n_new, width), lambda *a: (seq_of(*a[:n_lead]), col))

    def page_spec(g, tail):
        def idx(*a):
            pt = a[n_lead]
            page = pt[seq_of(*a[:n_lead]), last - (chunk_of(*a[:n_lead]) * n_pg + g)]
            return (0, page) + (0,) * (1 + len(tail))
        return pl.BlockSpec((None, None, n_heads) + tail, idx)

    kv_tail = (HEAD_DIM, PAGE_SIZE)
    specs = [tok_spec(feat, cols["q"] // feat), tok_spec(feat, cols["k"] // feat), tok_spec(feat, cols["v"] // feat),
             tok_spec(V7X_LANES, 0)]
    specs += [page_spec(g, kv_tail) for g in range(n_pg)]
    specs += [page_spec(g, kv_tail) for g in range(n_pg)]
    specs += [page_spec(g, (PAGE_SIZE,)) for g in range(n_pg)]
    return specs


def _h_decode_kernel(pt_ref, xn_ref, w_ref, q_ref, kn_ref, vn_ref, lfn_ref, *rest, n_heads, n_new, n_pg, n_j,
                     n_gate, n_chunks):
    del pt_ref
    kc_refs = rest[0:n_pg]
    vc_refs = rest[n_pg:2 * n_pg]
    lf_refs = rest[2 * n_pg:3 * n_pg]
    hg_ref, u_ref, o_ref = rest[3 * n_pg:3 * n_pg + 3]
    j = pl.program_id(1)
    cj = _divmod_nonneg(pl.program_id(0) * n_j + j, n_chunks)[1]

    def step(store):
        _decode_step(cj, n_chunks, q_ref, kn_ref, vn_ref, lfn_ref, kc_refs, vc_refs, lf_refs, o_ref,
                     rest[3 * n_pg + 3:], n_heads=n_heads, n_new=n_new,
                     between=lambda: store(_dot(xn_ref[...], w_ref[...])))

    def store_gate(acc):
        hg_ref[...] = acc.astype(BF16)

    def store_u(acc):
        u_ref[...] = acc

    pl.when(j < n_gate)(lambda: step(store_gate))
    pl.when(j >= n_gate)(lambda: step(store_u))


def _h_decode(plan, page_table, xn, wbf, hs, lf_new, cache_kt, cache_vt, cache_lft, *, tm, n_heads, n_new, n_pg):
    m, d = xn.shape
    tn = plan["tn"]
    cols = plan["cols"]
    n_j = plan["steps"]["q"][0]
    n_pages = page_table.shape[1]
    n_chunks = n_pages // n_pg
    n_steps = (m // tm) * n_j
    assert n_steps % n_chunks == 0 and n_steps // n_chunks <= page_table.shape[0]
    n_seq = n_steps // n_chunks
    rows = n_heads * n_new
    n_gate = plan["steps"]["u"][0]
    assert plan["steps"]["u"][1] == n_j
    kern = functools.partial(_h_decode_kernel, n_heads=n_heads, n_new=n_new, n_pg=n_pg, n_j=n_j, n_gate=n_gate,
                             n_chunks=n_chunks)
    in_specs = [
        pl.BlockSpec((tm, d), lambda i, j, pt: (i, 0)),
        pl.BlockSpec((None, d, tn), lambda i, j, pt: (j, 0, 0)),
    ] + _decode_in_specs(lambda i, j: _divmod_nonneg(i * n_j + j, n_chunks)[0],
                         lambda i, j: _divmod_nonneg(i * n_j + j, n_chunks)[1],
                         n_heads=n_heads, n_new=n_new, n_pg=n_pg, n_pages=n_pages, cols=cols, n_lead=2)
    grid_spec = pltpu.PrefetchScalarGridSpec(
        num_scalar_prefetch=1,
        grid=(m // tm, n_j),
        in_specs=in_specs,
        out_specs=[
            pl.BlockSpec((tm, tn), lambda i, j, pt: (i, jnp.minimum(j, n_gate - 1))),
            pl.BlockSpec((tm, tn), lambda i, j, pt: (i, jnp.maximum(j - n_gate, 0))),
            pl.BlockSpec((None, rows, HEAD_DIM), lambda i, j, pt: (_divmod_nonneg(i * n_j + j, n_chunks)[0], 0, 0)),
        ],
        scratch_shapes=_decode_scratch(n_heads, n_new),
    )
    args = [page_table, xn, wbf, hs, hs, hs, lf_new] + [cache_kt] * n_pg + [cache_vt] * n_pg + [cache_lft] * n_pg
    hg, u, att_rows = pl.pallas_call(
        kern,
        grid_spec=grid_spec,
        out_shape=[
            jax.ShapeDtypeStruct((m, n_gate * tn), BF16),
            jax.ShapeDtypeStruct((m, (n_j - n_gate) * tn), F32),
            jax.ShapeDtypeStruct((n_seq, rows, HEAD_DIM), F32),
        ],
        compiler_params=_cparams(2),
        name="h_decode",
    )(*args)
    return hg, u, att_rows, n_seq


def _pool_diff(z, u, pos, group_w):
    s = z
    k = 1
    while k < group_w:
        s = s + pltpu.roll(s, k, 0)
        k *= 2
    cnt = jnp.minimum(pos + 1, group_w).astype(F32)
    return s[HALO:, :] / cnt - u


def _mix_tail(d_groups, gp, ga, att, ma, mb, x, wpool_ref, ps_ref, wup_ref, wua_ref, wout_ref):
    mixed = [_dot(d.astype(BF16), wpool_ref[gi]) for gi, d in enumerate(d_groups)]
    mixed = jnp.concatenate(mixed, axis=1)
    branch_a = (mixed * ps_ref[...]) * _silu(gp.astype(F32))
    branch_b = att.astype(F32) * _silu(ga.astype(F32))
    up_a = _dot(branch_a.astype(BF16), wup_ref[...])
    up_b = _dot(branch_b.astype(BF16), wua_ref[...])
    merged = _sigmoid(ma.astype(F32)) * up_a + _sigmoid(mb.astype(F32)) * up_b
    return x + _dot(merged.astype(BF16), wout_ref[...])


def _out_prompt_kernel(pt_ref, u_ref, halo_ref, gp_ref, ga_ref, ma_ref, mb_ref, att_ref, x_ref,
                       wpool_ref, ps_ref, wup_ref, wua_ref, wout_ref, q_ref, kn_ref, vn_ref, lfn_ref, *rest,
                       tm, seq, n_heads, n_new, n_pg):
    del pt_ref
    kc_refs = rest[0:n_pg]
    vc_refs = rest[n_pg:2 * n_pg]
    lf_refs = rest[2 * n_pg:3 * n_pg]
    y_ref, o_ref = rest[3 * n_pg:3 * n_pg + 2]
    i = pl.program_id(0)

    def output_stage():
        pos0 = (i * tm) % seq
        u = u_ref[...]
        halo = jnp.where(pos0 == 0, 0.0, halo_ref[...])
        z = jnp.concatenate([halo, u], axis=0)
        pos = pos0 + lax.broadcasted_iota(jnp.int32, (tm, 1), 0)
        gw = u.shape[1] // len(POOL_WINDOWS)
        d_groups = [_pool_diff(z[:, gi * gw:(gi + 1) * gw], u[:, gi * gw:(gi + 1) * gw], pos, w)
                    for gi, w in enumerate(POOL_WINDOWS)]
        y_ref[...] = _mix_tail(d_groups, gp_ref[...], ga_ref[...], att_ref[...], ma_ref[...], mb_ref[...], x_ref[...],
                               wpool_ref, ps_ref, wup_ref, wua_ref, wout_ref)

    _decode_step(i, pl.num_programs(0), q_ref, kn_ref, vn_ref, lfn_ref, kc_refs, vc_refs, lf_refs, o_ref,
                 rest[3 * n_pg + 2:], n_heads=n_heads, n_new=n_new, between=output_stage)


def _out_sample_kernel(z_ref, gp_ref, ga_ref, ma_ref, mb_ref, att_ref, x_ref,
                       wpool_ref, ps_ref, wup_ref, wua_ref, wout_ref, y_ref, *, dec_b, n_new, pos0):
    gw = z_ref.shape[2] // len(POOL_WINDOWS)
    pos = pos0 + lax.broadcasted_iota(jnp.int32, (n_new, 1), 0)
    per_seq = []
    for b in range(dec_b):
        z = z_ref[b]
        u = z[HALO:, :]
        per_seq.append([_pool_diff(z[:, gi * gw:(gi + 1) * gw], u[:, gi * gw:(gi + 1) * gw], pos, w)
                        for gi, w in enumerate(POOL_WINDOWS)])
    d_groups = [jnp.concatenate([per_seq[b][gi] for b in range(dec_b)], axis=0) for gi in range(len(POOL_WINDOWS))]
    y_ref[...] = _mix_tail(d_groups, gp_ref[...], ga_ref[...], att_ref[...], ma_ref[...], mb_ref[...], x_ref[...],
                           wpool_ref, ps_ref, wup_ref, wua_ref, wout_ref)


def _weight_specs(pw, gw, dm):
    one = pl.Buffered(1)
    return [
        pl.BlockSpec((len(POOL_WINDOWS), gw, gw), lambda *_: (0, 0, 0), pipeline_mode=one),
        pl.BlockSpec((1, pw), lambda *_: (0, 0), pipeline_mode=one),
        pl.BlockSpec((pw, dm), lambda *_: (0, 0), pipeline_mode=one),
        pl.BlockSpec((pw, dm), lambda *_: (0, 0), pipeline_mode=one),
        pl.BlockSpec((dm, dm), lambda *_: (0, 0), pipeline_mode=one),
    ]


def _out_prompt(hg, u, att, x2d, wpool, ps, wup, wua, wout, page_table, hs, lf_new, cache_kt, cache_vt, cache_lft,
                *, seq, tm, cols, n_heads, n_new, decode_seq):
    m, dm = x2d.shape
    pw = att.shape[1]
    gw = pw // len(POOL_WINDOWS)
    hb = tm // HALO
    n_steps = m // tm
    n_pages = page_table.shape[1]
    assert n_pages % n_steps == 0
    n_pg = n_pages // n_steps
    rows = n_heads * n_new
    kern = functools.partial(_out_prompt_kernel, tm=tm, seq=seq, n_heads=n_heads, n_new=n_new, n_pg=n_pg)
    in_specs = [
        pl.BlockSpec((tm, pw), lambda i, pt: (i, 0)),
        pl.BlockSpec((HALO, pw), lambda i, pt: (jnp.maximum(i * hb - 1, 0), 0)),
        pl.BlockSpec((tm, pw), lambda i, pt: (i, cols["gp"] // pw)),
        pl.BlockSpec((tm, pw), lambda i, pt: (i, cols["ga"] // pw)),
        pl.BlockSpec((tm, dm), lambda i, pt: (i, cols["ma"] // dm)),
        pl.BlockSpec((tm, dm), lambda i, pt: (i, cols["mb"] // dm)),
        pl.BlockSpec((tm, pw), lambda i, pt: (i, 0)),
        pl.BlockSpec((tm, dm), lambda i, pt: (i, 0)),
    ] + _weight_specs(pw, gw, dm) + _decode_in_specs(
        lambda i: decode_seq, lambda i: i, n_heads=n_heads, n_new=n_new, n_pg=n_pg, n_pages=n_pages, cols=cols,
        n_lead=1)
    grid_spec = pltpu.PrefetchScalarGridSpec(
        num_scalar_prefetch=1,
        grid=(n_steps,),
        in_specs=in_specs,
        out_specs=[
            pl.BlockSpec((tm, dm), lambda i, pt: (i, 0)),
            pl.BlockSpec((None, rows, HEAD_DIM), lambda i, pt: (0, 0, 0)),
        ],
        scratch_shapes=_decode_scratch(n_heads, n_new),
    )
    args = [page_table, u, u, hg, hg, hg, hg, att, x2d, wpool, ps, wup, wua, wout, hs, hs, hs, lf_new]
    args += [cache_kt] * n_pg + [cache_vt] * n_pg + [cache_lft] * n_pg
    return pl.pallas_call(
        kern,
        grid_spec=grid_spec,
        out_shape=[
            jax.ShapeDtypeStruct((m, dm), F32),
            jax.ShapeDtypeStruct((1, rows, HEAD_DIM), F32),
        ],
        compiler_params=_cparams(1),
        name="out_prompt",
    )(*args)


def _out_sample(z, h, att, x2d, wpool, ps, wup, wua, wout, *, dec_b, n_new, pos0, cols):
    m, dm = x2d.shape
    pw = att.shape[1]
    gw = pw // len(POOL_WINDOWS)
    kern = functools.partial(_out_sample_kernel, dec_b=dec_b, n_new=n_new, pos0=pos0)
    return pl.pallas_call(
        kern,
        grid=(1,),
        in_specs=[
            pl.BlockSpec((dec_b, HALO + n_new, pw), lambda i: (0, 0, 0)),
            pl.BlockSpec((m, pw), lambda i: (0, cols["gp"] // pw)),
            pl.BlockSpec((m, pw), lambda i: (0, cols["ga"] // pw)),
            pl.BlockSpec((m, dm), lambda i: (0, cols["ma"] // dm)),
            pl.BlockSpec((m, dm), lambda i: (0, cols["mb"] // dm)),
            pl.BlockSpec((m, pw), lambda i: (0, 0)),
            pl.BlockSpec((m, dm), lambda i: (0, 0)),
        ] + _weight_specs(pw, gw, dm),
        out_specs=pl.BlockSpec((m, dm), lambda i: (0, 0)),
        out_shape=jax.ShapeDtypeStruct((m, dm), F32),
        compiler_params=_cparams(1),
        name="out_sample",
    )(z, h, h, h, h, att, x2d, wpool, ps, wup, wua, wout)


def kernel(x_prompt, x_sample, cache_k, cache_v, cache_logf, state_pool, page_table, norm_gain, w_in, b_f,
           q_norm_gain, k_norm_gain, w_pool_map, pool_scale, w_up_pool, w_up_attn, w_out):
    batch, seq, dm = x_prompt.shape
    dec_b, n_new, _ = x_sample.shape
    assert w_in.shape[0] == 1
    n_heads = b_f.shape[1]
    aw = n_heads * HEAD_DIM
    pw = w_up_pool.shape[1]
    n_pages = page_table.shape[1]
    past_len = n_pages * PAGE_SIZE
    assert pw == aw and dm == 2 * pw and n_pages % DECODE_PAGES_FUSED == 0

    tn = 512
    plan = _inproj_plan(pw, aw, dm, n_heads, tn)
    cols = plan["cols"]
    w_t = w_in[0].T
    fl0, fl_n = plan["fl_rows"]
    wfl = jnp.pad(w_t[fl0:fl0 + fl_n], ((0, V7X_LANES - n_heads), (0, 0))).astype(BF16)
    bfp = jnp.pad(b_f, ((0, 0), (0, V7X_LANES - n_heads)))
    reps = tn // HEAD_DIM
    qg = jnp.tile(q_norm_gain[0], reps)[None, :]
    kg = jnp.tile(k_norm_gain[0], reps)[None, :]
    qgc = jnp.broadcast_to(jnp.tile(q_norm_gain[0], n_heads)[:, None], (aw, V7X_LANES))
    kgc = jnp.broadcast_to(jnp.tile(k_norm_gain[0], n_heads)[:, None], (aw, V7X_LANES))
    seg = np.arange(tn) // HEAD_DIM
    bd = jnp.asarray((seg[:, None] == seg[None, :]).astype(np.float32)).astype(BF16)
    wpool = w_pool_map[0].astype(BF16)
    wup = w_up_pool[0].astype(BF16)
    wua = w_up_attn[0].astype(BF16)
    wout = w_out[0].astype(BF16)

    ms = dec_b * n_new
    xs2 = x_sample.reshape(ms, dm)
    hs, lfs, w_tok, w_fm = _inproj_new(plan, xs2, norm_gain, w_t, wfl, bfp, qg, kg, bd)

    cache_kt = cache_k.transpose(0, 1, 3, 4, 2)
    cache_vt = cache_v.transpose(0, 1, 3, 4, 2)
    cache_lft = cache_logf.transpose(0, 1, 3, 2)
    decode_args = dict(n_heads=n_heads, n_new=n_new, n_pg=DECODE_PAGES_FUSED)

    xp2 = x_prompt.reshape(batch * seq, dm)
    xn_p, lfp = _prenorm(xp2, norm_gain, wfl, bfp, tm=1024)
    hg_p, u_p, att_rows_a, n_fused = _h_decode(plan, page_table, xn_p, w_tok, hs, lfs, cache_kt, cache_vt,
                                               cache_lft, tm=1024, **decode_args)
    qt_p, kt_p, vt_p = _inproj_fm(plan, xn_p, w_fm, qgc, kgc, tm=1024, seq=seq, width=aw)
    fcol, ft, lft = _fcum(lfp, batch=batch, seq=seq, n_heads=n_heads)
    ft_pairs = ft.reshape(batch, n_heads // 2, 2, seq)
    att_p = _prompt_attention(qt_p, kt_p, vt_p, fcol, ft_pairs, batch=batch, seq=seq, n_heads=n_heads, tq=256,
                              n_pair=ATTN_PAIRS_PER_STEP)
    assert n_fused == dec_b - 1
    yp, att_rows_b = _out_prompt(hg_p, u_p, att_p, xp2, wpool, pool_scale, wup, wua, wout, page_table, hs, lfs,
                                 cache_kt, cache_vt, cache_lft, seq=seq, tm=256, cols=cols, n_heads=n_heads,
                                 n_new=n_new, decode_seq=n_fused)

    k_s = hs[:, cols["k"]:cols["k"] + aw].reshape(dec_b, n_new, n_heads, HEAD_DIM)
    v_s = hs[:, cols["v"]:cols["v"] + aw].reshape(dec_b, n_new, n_heads, HEAD_DIM)
    u_s = hs[:, cols["u"]:cols["u"] + pw].reshape(dec_b, n_new, pw)
    logf_s = lfs[:, :n_heads].reshape(dec_b, n_new, n_heads)
    att_rows = jnp.concatenate([att_rows_a, att_rows_b], axis=0)
    att_s = att_rows.reshape(dec_b, n_heads, n_new, HEAD_DIM).transpose(0, 2, 1, 3).reshape(ms, aw)
    z_s = jnp.concatenate([jnp.zeros((dec_b, HALO - POOL_BUF, pw), F32), state_pool[0], u_s], axis=1)
    ys = _out_sample(z_s, hs, att_s, xs2, wpool, pool_scale, wup, wua, wout,
                     dec_b=dec_b, n_new=n_new, pos0=past_len, cols=cols)

    k_p = kt_p.reshape(1, batch, n_heads, HEAD_DIM, seq).transpose(0, 1, 4, 2, 3)
    v_p = vt_p.reshape(1, batch, n_heads, HEAD_DIM, seq).transpose(0, 1, 4, 2, 3)
    logf_p = lft.transpose(0, 2, 1)[None]
    pool_p = u_p.reshape(batch, seq, pw)[:, seq - POOL_BUF:, :][None]
    pool_s = z_s[:, HALO + n_new - POOL_BUF:, :][None]
    return (yp.reshape(batch, seq, dm), ys.reshape(dec_b, n_new, dm), k_p, v_p, logf_p, pool_p,
            k_s[None], v_s[None], logf_s[None], pool_s)
```

```python
import functools

import jax
import jax.numpy as jnp
import numpy as np
from jax import lax
from jax.experimental import pallas as pl
from jax.experimental.pallas import tpu as pltpu

F32 = jnp.float32
BF16 = jnp.bfloat16

HEAD_DIM = 64
POOL_WINDOWS = (2, 4, 8, 16)
POOL_BUF = 15
PAGE_SIZE = 128
EPS = 1e-6
NEG_INF = -1e30
ATTN_SCALE = HEAD_DIM ** -0.5
LOG2E = 1.4426950408889634

V7X_LANES = 128
V7X_SUBLANES = 8
V7X_VMEM_LIMIT_BYTES = 56 * 1024 * 1024

ROW_ALIGN = 16
HALO = 16
DECODE_PAGES_FUSED = 8
N_AUG = 3
ATTN_PAIRS_PER_STEP = 8


def _cparams(n_grid_axes):
    return pltpu.CompilerParams(
        dimension_semantics=("arbitrary",) * n_grid_axes,
        vmem_limit_bytes=V7X_VMEM_LIMIT_BYTES,
    )


def _split3(x):
    hi = x.astype(BF16)
    r1 = x - hi.astype(F32)
    mid = r1.astype(BF16)
    lo = (r1 - mid.astype(F32)).astype(BF16)
    return hi, mid, lo


def _dot(a, b):
    return jnp.dot(a, b, preferred_element_type=F32)


def _dot_nt(a, b):
    return lax.dot_general(a, b, (((1,), (1,)), ((), ())), preferred_element_type=F32)


def _sigmoid(x):
    return 1.0 / (1.0 + jnp.exp(-x))


def _silu(x):
    return x * _sigmoid(x)


def _scan_lanes(x, *, reverse):
    n = x.shape[-1]
    ax = x.ndim - 1
    lane = lax.broadcasted_iota(jnp.int32, x.shape, ax)
    k = 1
    while k < n:
        if reverse:
            shifted = pltpu.roll(x, n - k, ax)
            x = x + jnp.where(lane < n - k, shifted, 0.0)
        else:
            shifted = pltpu.roll(x, k, ax)
            x = x + jnp.where(lane >= k, shifted, 0.0)
        k *= 2
    return x


def _divmod_nonneg(x, n):
    assert n > 0 and n & (n - 1) == 0
    return lax.shift_right_logical(x, n.bit_length() - 1), x & (n - 1)


def _scan_rows(x):
    n = x.shape[0]
    row = lax.broadcasted_iota(jnp.int32, x.shape, 0)
    k = 1
    while k < n:
        x = x + jnp.where(row >= k, pltpu.roll(x, k, 0), 0.0)
        k *= 2
    return x


def _norm_and_logf(x_ref, ng_ref, wfl_ref, bf_ref, xn_sc, lf_ref):
    x = x_ref[...]
    ms = jnp.mean(x * x, axis=-1, keepdims=True)
    xn = x * lax.rsqrt(ms + EPS) * ng_ref[...]
    xn_sc[...] = xn.astype(BF16)
    z = _dot_nt(xn_sc[...], wfl_ref[...]) + bf_ref[...]
    lf_ref[...] = jnp.minimum(z, 0.0) - jnp.log1p(jnp.exp(-jnp.abs(z)))


def _step_in(j, rng):
    return jnp.logical_and(j >= rng[0], j < rng[1])


def _inproj_new_kernel(off_ref, x_ref, ng_ref, w_ref, wfl_ref, bf_ref, qg_ref, kg_ref, bd_ref,
                       h_ref, lf_ref, wtok_ref, wfm_ref, xn_sc, *, steps):
    del off_ref
    j = pl.program_id(0)
    n_tok = steps["q"][0]

    @pl.when(j == 0)
    def _():
        _norm_and_logf(x_ref, ng_ref, wfl_ref, bf_ref, xn_sc, lf_ref)

    w = w_ref[...].astype(BF16)
    acc = _dot_nt(xn_sc[...], w)

    @pl.when(j < n_tok)
    def _():
        wtok_ref[...] = w_ref[...].T.astype(BF16)

    @pl.when(j >= n_tok)
    def _():
        wfm_ref[...] = w

    def head_norm(gain):
        sq = acc * acc
        hi = sq.astype(BF16)
        lo = (sq - hi.astype(F32)).astype(BF16)
        ss = _dot(hi, bd_ref[...]) + _dot(lo, bd_ref[...])
        return acc * lax.rsqrt(ss * (1.0 / HEAD_DIM) + EPS) * gain

    is_q = _step_in(j, steps["q"])
    is_k = _step_in(j, steps["k"])

    @pl.when(is_q)
    def _():
        h_ref[...] = head_norm(qg_ref[...])

    @pl.when(is_k)
    def _():
        h_ref[...] = head_norm(kg_ref[...])

    @pl.when(jnp.logical_not(is_q | is_k))
    def _():
        h_ref[...] = acc


def _inproj_new(plan, x2d, ng, w_t, wfl, bfp, qg, kg, bd):
    m, d = x2d.shape
    tn = plan["tn"]
    nj = plan["n_steps"]
    kern = functools.partial(_inproj_new_kernel, steps=plan["steps"])
    n_tok = plan["steps"]["q"][0]
    grid_spec = pltpu.PrefetchScalarGridSpec(
        num_scalar_prefetch=1,
        grid=(nj,),
        in_specs=[
            pl.BlockSpec((m, d), lambda j, off: (0, 0)),
            pl.BlockSpec((1, d), lambda j, off: (0, 0)),
            pl.BlockSpec((pl.Element(tn), pl.Element(d)), lambda j, off: (off[j] * ROW_ALIGN, 0)),
            pl.BlockSpec((V7X_LANES, d), lambda j, off: (0, 0)),
            pl.BlockSpec((1, V7X_LANES), lambda j, off: (0, 0)),
            pl.BlockSpec((1, tn), lambda j, off: (0, 0)),
            pl.BlockSpec((1, tn), lambda j, off: (0, 0)),
            pl.BlockSpec((tn, tn), lambda j, off: (0, 0)),
        ],
        out_specs=[
            pl.BlockSpec((m, tn), lambda j, off: (0, j)),
            pl.BlockSpec((m, V7X_LANES), lambda j, off: (0, 0)),
            pl.BlockSpec((None, d, tn), lambda j, off: (jnp.minimum(j, n_tok - 1), 0, 0)),
            pl.BlockSpec((tn, d), lambda j, off: (jnp.maximum(j - n_tok, 0), 0)),
        ],
        scratch_shapes=[pltpu.VMEM((m, d), BF16)],
    )
    return pl.pallas_call(
        kern,
        grid_spec=grid_spec,
        out_shape=[
            jax.ShapeDtypeStruct((m, nj * tn), F32),
            jax.ShapeDtypeStruct((m, V7X_LANES), F32),
            jax.ShapeDtypeStruct((n_tok, d, tn), BF16),
            jax.ShapeDtypeStruct(((nj - n_tok) * tn, d), BF16),
        ],
        compiler_params=_cparams(1),
        name="inproj_new",
    )(plan["row_offsets"], x2d, ng, w_t, wfl, bfp, qg, kg, bd)


def _prenorm_kernel(x_ref, ng_ref, wfl_ref, bf_ref, xn_ref, lf_ref):
    _norm_and_logf(x_ref, ng_ref, wfl_ref, bf_ref, xn_ref, lf_ref)


def _prenorm(x2d, ng, wfl, bfp, *, tm):
    m, d = x2d.shape
    return pl.pallas_call(
        _prenorm_kernel,
        grid=(m // tm,),
        in_specs=[
            pl.BlockSpec((tm, d), lambda i: (i, 0)),
            pl.BlockSpec((1, d), lambda i: (0, 0)),
            pl.BlockSpec((V7X_LANES, d), lambda i: (0, 0)),
            pl.BlockSpec((1, V7X_LANES), lambda i: (0, 0)),
        ],
        out_specs=[
            pl.BlockSpec((tm, d), lambda i: (i, 0)),
            pl.BlockSpec((tm, V7X_LANES), lambda i: (i, 0)),
        ],
        out_shape=[
            jax.ShapeDtypeStruct((m, d), BF16),
            jax.ShapeDtypeStruct((m, V7X_LANES), F32),
        ],
        compiler_params=_cparams(1),
        name="prenorm",
    )(x2d, ng, wfl, bfp)


def _inproj_fm_kernel(xn_ref, w_ref, qgc_ref, kgc_ref, qt_ref, kt_ref, vt_ref):
    j = pl.program_id(1)

    def head_norm(acc, gain_col_ref):
        tn, tm = acc.shape
        a3 = acc.reshape(tn // HEAD_DIM, HEAD_DIM, tm)
        ms = jnp.mean(a3 * a3, axis=1, keepdims=True)
        g3 = gain_col_ref[...].reshape(tn // HEAD_DIM, HEAD_DIM, V7X_LANES)[:, :, 0:1]
        return (a3 * lax.rsqrt(ms + EPS) * g3).reshape(tn, tm)

    @pl.when(j == 0)
    def _():
        qt_ref[...] = (head_norm(_dot_nt(w_ref[...], xn_ref[...]), qgc_ref) * (ATTN_SCALE * LOG2E)).astype(BF16)

    @pl.when(j == 1)
    def _():
        kt_ref[...] = head_norm(_dot_nt(w_ref[...], xn_ref[...]), kgc_ref)

    @pl.when(j == 2)
    def _():
        vt_ref[...] = _dot_nt(w_ref[...], xn_ref[...])


def _inproj_fm(plan, xn, wbf, qgc, kgc, *, tm, seq, width):
    m, d = xn.shape
    assert wbf.shape[0] == 3 * width
    nb = seq // tm
    out_spec = pl.BlockSpec((None, width, tm), lambda i, j: (_divmod_nonneg(i, nb)[0], 0, _divmod_nonneg(i, nb)[1]))
    return pl.pallas_call(
        _inproj_fm_kernel,
        grid=(m // tm, 3),
        in_specs=[
            pl.BlockSpec((tm, d), lambda i, j: (i, 0)),
            pl.BlockSpec((width, d), lambda i, j: (j, 0)),
            pl.BlockSpec((width, V7X_LANES), lambda i, j: (0, 0)),
            pl.BlockSpec((width, V7X_LANES), lambda i, j: (0, 0)),
        ],
        out_specs=[out_spec] * 3,
        out_shape=[jax.ShapeDtypeStruct((m // seq, width, seq), dt) for dt in (BF16, F32, F32)],
        compiler_params=_cparams(2),
        name="inproj_fm",
    )(xn, wbf, qgc, kgc)


def _inproj_plan(pw, aw, dm, n_heads, tn):
    src, off = {}, 0
    for name, size in (("u", pw), ("gp", pw), ("q", aw), ("k", aw), ("v", aw), ("ga", aw), ("fl", n_heads),
                       ("ma", dm), ("mb", dm)):
        src[name] = (off, size)
        off += size
    order = ("ma", "mb", "gp", "ga", "u", "q", "k", "v")
    row_offsets, steps, cols = [], {}, {}
    for name in order:
        base, size = src[name]
        steps[name] = (len(row_offsets), len(row_offsets) + size // tn)
        cols[name] = len(row_offsets) * tn
        row_offsets += [base + o for o in range(0, size, tn)]
    assert all(r % ROW_ALIGN == 0 for r in row_offsets)
    return {"tn": tn, "steps": steps, "n_steps": len(row_offsets), "cols": cols, "fl_rows": src["fl"],
            "row_offsets": jnp.asarray([r // ROW_ALIGN for r in row_offsets], jnp.int32)}


def _fcum_kernel(lf_ref, fcol_ref, ft_ref, lft_ref, *, seq, n_heads):
    c = V7X_LANES
    row = lax.broadcasted_iota(jnp.int32, (c, c), 0)
    col = lax.broadcasted_iota(jnp.int32, (c, c), 1)
    tri = jnp.where(col <= row, 1.0, 0.0).astype(BF16)
    carry = jnp.zeros((1, c), F32)
    for ci in range(seq // c):
        x = lf_ref[ci * c:(ci + 1) * c, :]
        hi, mid, lo = _split3(x)
        fc = (_dot(tri, hi) + _dot(tri, mid)) + _dot(tri, lo) + carry
        fcol_ref[ci * c:(ci + 1) * c, :] = fc
        ft_ref[:, ci * c:(ci + 1) * c] = fc.T[:n_heads, :]
        lft_ref[:, ci * c:(ci + 1) * c] = x.T[:n_heads, :]
        carry = fc[c - 1:c, :]


def _fcum(lf2d, *, batch, seq, n_heads):
    kern = functools.partial(_fcum_kernel, seq=seq, n_heads=n_heads)
    return pl.pallas_call(
        kern,
        grid=(batch,),
        in_specs=[pl.BlockSpec((seq, V7X_LANES), lambda b: (b, 0))],
        out_specs=[
            pl.BlockSpec((seq, V7X_LANES), lambda b: (b, 0)),
            pl.BlockSpec((None, n_heads, seq), lambda b: (b, 0, 0)),
            pl.BlockSpec((None, n_heads, seq), lambda b: (b, 0, 0)),
        ],
        out_shape=[
            jax.ShapeDtypeStruct((batch * seq, V7X_LANES), F32),
            jax.ShapeDtypeStruct((batch, n_heads, seq), F32),
            jax.ShapeDtypeStruct((batch, n_heads, seq), F32),
        ],
        compiler_params=_cparams(1),
        name="forget_cumsum",
    )(lf2d)


def _attn_kernel(qt_ref, kt_ref, vt_ref, fcol_ref, ft_ref, o_ref, ka_sc, *, tq, seq, n_pair):
    g = pl.program_id(1)
    i = pl.program_id(2)
    d = HEAD_DIM
    lanes = V7X_LANES
    lane1 = lax.broadcasted_iota(jnp.int32, (1, lanes), 1)

    @pl.when(i == 0)
    def _():
        prow = lax.broadcasted_iota(jnp.int32, (N_AUG * lanes, lanes), 0)
        lane = lax.broadcasted_iota(jnp.int32, (N_AUG * lanes, lanes), 1)
        piece, hrow = prow // lanes, prow % lanes
        ones = jnp.where(((lane1 >= d) & (lane1 < d + N_AUG)) | (lane1 < N_AUG), 1.0, 0.0)
        sels = []
        for pr in range(n_pair):
            head0 = 2 * (g * n_pair + pr)
            hit = ((hrow == head0) & (lane == d + N_AUG + piece)) | ((hrow == head0 + 1) & (lane == N_AUG + piece))
            sels.append(jnp.where(hit, -1.0, 0.0).astype(BF16))
        ck = 256
        for c in range(seq // ck):
            pieces = jnp.concatenate(_split3(fcol_ref[c * ck:(c + 1) * ck, :] * LOG2E), axis=1)
            for pr in range(n_pair):
                k_rows = kt_ref[pr * lanes:(pr + 1) * lanes, c * ck:(c + 1) * ck].T
                aug = ones + _dot(pieces, sels[pr])
                ka_sc[2 * pr, c * ck:(c + 1) * ck, :] = jnp.where(lane1 < d, k_rows, aug).astype(BF16)
                ka_sc[2 * pr + 1, c * ck:(c + 1) * ck, :] = jnp.where(lane1 >= d, k_rows, aug).astype(BF16)

    qs = pl.multiple_of(i * tq, tq)
    rowi = lax.broadcasted_iota(jnp.int32, (d, tq), 0)

    def aug_rows(f_row):
        hi, mid, lo = (x.astype(F32) for x in _split3(f_row))
        return jnp.where(rowi == 0, hi, jnp.where(rowi == 1, mid, jnp.where(rowi == 2, lo,
                         jnp.where(rowi < 2 * N_AUG, 1.0, 0.0))))

    qa = []
    for pr in range(n_pair):
        qt = qt_ref[pr * lanes:(pr + 1) * lanes, :]
        fq = ft_ref[pr, :, pl.ds(qs, tq)] * LOG2E
        qa.append(jnp.concatenate([qt[:d, :], aug_rows(fq[0:1, :]).astype(BF16)], axis=0))
        qa.append(jnp.concatenate([aug_rows(fq[1:2, :]).astype(BF16), qt[d:, :]], axis=0))

    heads = range(2 * n_pair)

    def step(j, carry):
        ks = pl.multiple_of(j * tq, tq)
        scores = [_dot(ka_sc[hh, pl.ds(ks, tq), :], qa[hh]) for hh in heads]
        stats, probs = [], []
        for hh in heads:
            m, l, _ = carry[hh]
            m_new = jnp.maximum(m, jnp.max(scores[hh], axis=0, keepdims=True))
            alpha = jnp.exp2(m - m_new)
            pt = jnp.exp2(scores[hh] - m_new)
            stats.append((m_new, alpha * l + jnp.sum(pt, axis=0, keepdims=True), alpha))
            probs.append(pt.astype(BF16))
        pvs = [_dot(vt_ref[hh * d:(hh + 1) * d, pl.ds(ks, tq)].astype(BF16), probs[hh]) for hh in heads]
        return tuple((stats[hh][0], stats[hh][1], stats[hh][2] * carry[hh][2] + pvs[hh]) for hh in heads)

    def diagonal_step(carry):
        hq = tq // 2
        ka0 = pl.multiple_of(i * tq, tq)
        kb0 = pl.multiple_of(i * tq + hq, hq)
        tri = lax.broadcasted_iota(jnp.int32, (hq, hq), 0) <= lax.broadcasted_iota(jnp.int32, (hq, hq), 1)
        first = [_dot(ka_sc[hh, pl.ds(ka0, hq), :], qa[hh]) for hh in heads]
        second = [_dot(ka_sc[hh, pl.ds(kb0, hq), :], qa[hh][:, hq:]) for hh in heads]
        stats, probs_a, probs_b = [], [], []
        for hh in heads:
            m, l, _ = carry[hh]
            sa = jnp.concatenate([jnp.where(tri, first[hh][:, :hq], NEG_INF), first[hh][:, hq:]], axis=1)
            sb = jnp.where(tri, second[hh], NEG_INF)
            tile_max = jnp.max(sa, axis=0, keepdims=True)
            tile_max = jnp.concatenate([tile_max[:, :hq],
                                        jnp.maximum(tile_max[:, hq:], jnp.max(sb, axis=0, keepdims=True))], axis=1)
            m_new = jnp.maximum(m, tile_max)
            alpha = jnp.exp2(m - m_new)
            pa = jnp.exp2(sa - m_new)
            pb = jnp.exp2(sb - m_new[:, hq:])
            psum = jnp.sum(pa, axis=0, keepdims=True)
            psum = jnp.concatenate([psum[:, :hq], psum[:, hq:] + jnp.sum(pb, axis=0, keepdims=True)], axis=1)
            stats.append((m_new, alpha * l + psum, alpha))
            probs_a.append(pa.astype(BF16))
            probs_b.append(pb.astype(BF16))
        out = []
        for hh in heads:
            pv = _dot(vt_ref[hh * d:(hh + 1) * d, pl.ds(ka0, hq)].astype(BF16), probs_a[hh])
            pvb = _dot(vt_ref[hh * d:(hh + 1) * d, pl.ds(kb0, hq)].astype(BF16), probs_b[hh])
            pv = jnp.concatenate([pv[:, :hq], pv[:, hq:] + pvb], axis=1)
            out.append((stats[hh][0], stats[hh][1], stats[hh][2] * carry[hh][2] + pv))
        return tuple(out)

    init = tuple((jnp.full((1, tq), NEG_INF, F32), jnp.zeros((1, tq), F32), jnp.zeros((d, tq), F32))
                 for _ in range(2 * n_pair))
    carry = lax.fori_loop(0, i, step, init)
    final = diagonal_step(carry)
    o_ref[...] = jnp.concatenate([acc / l for (_, l, acc) in final], axis=0).T.astype(o_ref.dtype)


def _prompt_attention(qt, kt, vt, fcol, ft_pairs, *, batch, seq, n_heads, tq, n_pair):
    n_groups = n_heads // (2 * n_pair)
    qn = seq // tq
    rows = n_pair * V7X_LANES
    kern = functools.partial(_attn_kernel, tq=tq, seq=seq, n_pair=n_pair)
    return pl.pallas_call(
        kern,
        grid=(batch, n_groups, qn),
        in_specs=[
            pl.BlockSpec((None, rows, tq), lambda b, g, i: (b, g, i)),
            pl.BlockSpec((None, rows, seq), lambda b, g, i: (b, g, 0)),
            pl.BlockSpec((None, rows, seq), lambda b, g, i: (b, g, 0)),
            pl.BlockSpec((seq, V7X_LANES), lambda b, g, i: (b, 0)),
            pl.BlockSpec((None, n_pair, 2, seq), lambda b, g, i: (b, g, 0, 0)),
        ],
        out_specs=pl.BlockSpec((tq, rows), lambda b, g, i: (b * qn + i, g)),
        out_shape=jax.ShapeDtypeStruct((batch * seq, n_heads * HEAD_DIM), BF16),
        scratch_shapes=[pltpu.VMEM((2 * n_pair, seq, V7X_LANES), BF16)],
        compiler_params=_cparams(3),
        name="prompt_attention",
    )(qt, kt, vt, fcol, ft_pairs)


def _decode_step(cj, n_chunks, q_ref, kn_ref, vn_ref, lfn_ref, kc_refs, vc_refs, lf_refs, o_ref, scratch,
                 *, n_heads, n_new, between=None):
    m_sc, l_sc, acc_sc, carry_sc, qbd_sc, qb_sc = scratch
    rows = n_heads * n_new
    feat = n_heads * HEAD_DIM

    def tile_rows(x):
        return jnp.concatenate([x] * n_heads, axis=0)

    def rep_rows(x):
        return jnp.concatenate([jnp.broadcast_to(x[h:h + 1, :], (n_new, x.shape[1])) for h in range(n_heads)], axis=0)

    def to_pages(x):
        xp = jnp.concatenate([x, jnp.zeros((PAGE_SIZE - n_new, feat), F32)], axis=0)
        return jnp.concatenate([xp[:, c * PAGE_SIZE:(c + 1) * PAGE_SIZE].T for c in range(feat // PAGE_SIZE)], axis=0)

    def score_phase(k_pages, bias):
        s = _dot(qbd_sc[...], k_pages) + bias
        m_prev = m_sc[...]
        m_new = jnp.maximum(m_prev, jnp.max(s, axis=1, keepdims=True))
        alpha = jnp.exp(m_prev - m_new)
        p = jnp.exp(s - m_new)
        l_sc[...] = alpha * l_sc[...] + jnp.sum(p, axis=1, keepdims=True)
        m_sc[...] = m_new
        return p, alpha

    def value_phase(v_pages, p, alpha):
        acc_sc[...] = alpha * acc_sc[...] + _dot_nt(p.astype(BF16), v_pages)

    @pl.when(cj == 0)
    def _():
        m_sc[...] = jnp.full(m_sc.shape, NEG_INF, F32)
        l_sc[...] = jnp.zeros(l_sc.shape, F32)
        acc_sc[...] = jnp.zeros(acc_sc.shape, F32)
        carry_sc[...] = jnp.zeros(carry_sc.shape, F32)
        row_head = lax.broadcasted_iota(jnp.int32, (rows, feat), 0) // n_new
        col_head = lax.broadcasted_iota(jnp.int32, (rows, feat), 1) // HEAD_DIM
        qbd_sc[...] = jnp.where(row_head == col_head, tile_rows(q_ref[...] * ATTN_SCALE), 0.0).astype(BF16)
        f_new = _scan_rows(lfn_ref[...])
        rh = lax.broadcasted_iota(jnp.int32, (rows, V7X_LANES), 0) // n_new
        ln = lax.broadcasted_iota(jnp.int32, (rows, V7X_LANES), 1)
        qb = jnp.sum(jnp.where(rh == ln, tile_rows(f_new), 0.0), axis=1, keepdims=True)
        qb_sc[...] = qb
        f_pad = jnp.concatenate([f_new, jnp.zeros((PAGE_SIZE - n_new, V7X_LANES), F32)], axis=0)
        f_new_t = f_pad.T[:n_heads, :]
        tok = lax.broadcasted_iota(jnp.int32, (rows, PAGE_SIZE), 0) % n_new
        pos = lax.broadcasted_iota(jnp.int32, (rows, PAGE_SIZE), 1)
        bias = (qb - rep_rows(f_new_t)) + jnp.where(pos <= tok, 0.0, NEG_INF)
        p, alpha = score_phase(to_pages(kn_ref[...]).astype(BF16), bias)
        value_phase(to_pages(vn_ref[...]).astype(BF16), p, alpha)

    carry = carry_sc[...]
    biases = []
    for lf_ref in lf_refs:
        lf = lf_ref[...]
        incl = _scan_lanes(lf, reverse=True)
        biases.append(rep_rows((incl - lf) + carry))
        carry = carry + incl[:, 0:1]
    carry_sc[...] = carry
    k_pages = jnp.concatenate([r[...].reshape(feat, PAGE_SIZE).astype(BF16) for r in kc_refs], axis=1)
    p, alpha = score_phase(k_pages, qb_sc[...] + jnp.concatenate(biases, axis=1))
    if between is not None:
        between()
    v_pages = jnp.concatenate([r[...].reshape(feat, PAGE_SIZE).astype(BF16) for r in vc_refs], axis=1)
    value_phase(v_pages, p, alpha)

    @pl.when(cj == n_chunks - 1)
    def _():
        outs = [acc_sc[h * n_new:(h + 1) * n_new, h * HEAD_DIM:(h + 1) * HEAD_DIM] for h in range(n_heads)]
        o_ref[...] = jnp.concatenate(outs, axis=0) / l_sc[...]


def _decode_scratch(n_heads, n_new):
    rows = n_heads * n_new
    feat = n_heads * HEAD_DIM
    return [
        pltpu.VMEM((rows, 1), F32),
        pltpu.VMEM((rows, 1), F32),
        pltpu.VMEM((rows, feat), F32),
        pltpu.VMEM((n_heads, PAGE_SIZE), F32),
        pltpu.VMEM((rows, feat), BF16),
        pltpu.VMEM((rows, 1), F32),
    ]


def _decode_in_specs(seq_of, chunk_of, *, n_heads, n_new, n_pg, n_pages, cols, n_lead):
    feat = n_heads * HEAD_DIM
    last = n_pages - 1

    def tok_spec(width, col):
        return pl.BlockSpec((n_new, width), lambda *a: (seq_of(*a[:n_lead]), col))

    def page_spec(g, tail):
        def idx(*a):
            pt = a[n_lead]
            page = pt[seq_of(*a[:n_lead]), last - (chunk_of(*a[:n_lead]) * n_pg + g)]
            return (0, page) + (0,) * (1 + len(tail))
        return pl.BlockSpec((None, None, n_heads) + tail, idx)

    kv_tail = (HEAD_DIM, PAGE_SIZE)
    specs = [tok_spec(feat, cols["q"] // feat), tok_spec(feat, cols["k"] // feat), tok_spec(feat, cols["v"] // feat),
             tok_spec(V7X_LANES, 0)]
    specs += [page_spec(g, kv_tail) for g in range(n_pg)]
    specs += [page_spec(g, kv_tail) for g in range(n_pg)]
    specs += [page_spec(g, (PAGE_SIZE,)) for g in range(n_pg)]
    return specs


def _h_decode_kernel(pt_ref, xn_ref, w_ref, q_ref, kn_ref, vn_ref, lfn_ref, *rest, n_heads, n_new, n_pg, n_j,
                     n_gate, n_chunks):
    del pt_ref
    kc_refs = rest[0:n_pg]
    vc_refs = rest[n_pg:2 * n_pg]
    lf_refs = rest[2 * n_pg:3 * n_pg]
    hg_ref, u_ref, o_ref = rest[3 * n_pg:3 * n_pg + 3]
    j = pl.program_id(1)
    cj = _divmod_nonneg(pl.program_id(0) * n_j + j, n_chunks)[1]

    def step(store):
        _decode_step(cj, n_chunks, q_ref, kn_ref, vn_ref, lfn_ref, kc_refs, vc_refs, lf_refs, o_ref,
                     rest[3 * n_pg + 3:], n_heads=n_heads, n_new=n_new,
                     between=lambda: store(_dot(xn_ref[...], w_ref[...])))

    def store_gate(acc):
        hg_ref[...] = acc.astype(BF16)

    def store_u(acc):
        u_ref[...] = acc

    pl.when(j < n_gate)(lambda: step(store_gate))
    pl.when(j >= n_gate)(lambda: step(store_u))


def _h_decode(plan, page_table, xn, wbf, hs, lf_new, cache_kt, cache_vt, cache_lft, *, tm, n_heads, n_new, n_pg):
    m, d = xn.shape
    tn = plan["tn"]
    cols = plan["cols"]
    n_j = plan["steps"]["q"][0]
    n_pages = page_table.shape[1]
    n_chunks = n_pages // n_pg
    n_steps = (m // tm) * n_j
    assert n_steps % n_chunks == 0 and n_steps // n_chunks <= page_table.shape[0]
    n_seq = n_steps // n_chunks
    rows = n_heads * n_new
    n_gate = plan["steps"]["u"][0]
    assert plan["steps"]["u"][1] == n_j
    kern = functools.partial(_h_decode_kernel, n_heads=n_heads, n_new=n_new, n_pg=n_pg, n_j=n_j, n_gate=n_gate,
                             n_chunks=n_chunks)
    in_specs = [
        pl.BlockSpec((tm, d), lambda i, j, pt: (i, 0)),
        pl.BlockSpec((None, d, tn), lambda i, j, pt: (j, 0, 0)),
    ] + _decode_in_specs(lambda i, j: _divmod_nonneg(i * n_j + j, n_chunks)[0],
                         lambda i, j: _divmod_nonneg(i * n_j + j, n_chunks)[1],
                         n_heads=n_heads, n_new=n_new, n_pg=n_pg, n_pages=n_pages, cols=cols, n_lead=2)
    grid_spec = pltpu.PrefetchScalarGridSpec(
        num_scalar_prefetch=1,
        grid=(m // tm, n_j),
        in_specs=in_specs,
        out_specs=[
            pl.BlockSpec((tm, tn), lambda i, j, pt: (i, jnp.minimum(j, n_gate - 1))),
            pl.BlockSpec((tm, tn), lambda i, j, pt: (i, jnp.maximum(j - n_gate, 0))),
            pl.BlockSpec((None, rows, HEAD_DIM), lambda i, j, pt: (_divmod_nonneg(i * n_j + j, n_chunks)[0], 0, 0)),
        ],
        scratch_shapes=_decode_scratch(n_heads, n_new),
    )
    args = [page_table, xn, wbf, hs, hs, hs, lf_new] + [cache_kt] * n_pg + [cache_vt] * n_pg + [cache_lft] * n_pg
    hg, u, att_rows = pl.pallas_call(
        kern,
        grid_spec=grid_spec,
        out_shape=[
            jax.ShapeDtypeStruct((m, n_gate * tn), BF16),
            jax.ShapeDtypeStruct((m, (n_j - n_gate) * tn), F32),
            jax.ShapeDtypeStruct((n_seq, rows, HEAD_DIM), F32),
        ],
        compiler_params=_cparams(2),
        name="h_decode",
    )(*args)
    return hg, u, att_rows, n_seq


def _pool_diff(z, u, pos, group_w):
    s = z
    k = 1
    while k < group_w:
        s = s + pltpu.roll(s, k, 0)
        k *= 2
    cnt = jnp.minimum(pos + 1, group_w).astype(F32)
    return s[HALO:, :] / cnt - u


def _mix_tail(d_groups, gp, ga, att, ma, mb, x, wpool_ref, ps_ref, wup_ref, wua_ref, wout_ref):
    mixed = [_dot(d.astype(BF16), wpool_ref[gi]) for gi, d in enumerate(d_groups)]
    mixed = jnp.concatenate(mixed, axis=1)
    branch_a = (mixed * ps_ref[...]) * _silu(gp.astype(F32))
    branch_b = att.astype(F32) * _silu(ga.astype(F32))
    up_a = _dot(branch_a.astype(BF16), wup_ref[...])
    up_b = _dot(branch_b.astype(BF16), wua_ref[...])
    merged = _sigmoid(ma.astype(F32)) * up_a + _sigmoid(mb.astype(F32)) * up_b
    return x + _dot(merged.astype(BF16), wout_ref[...])


def _out_prompt_kernel(pt_ref, u_ref, halo_ref, gp_ref, ga_ref, ma_ref, mb_ref, att_ref, x_ref,
                       wpool_ref, ps_ref, wup_ref, wua_ref, wout_ref, q_ref, kn_ref, vn_ref, lfn_ref, *rest,
                       tm, seq, n_heads, n_new, n_pg):
    del pt_ref
    kc_refs = rest[0:n_pg]
    vc_refs = rest[n_pg:2 * n_pg]
    lf_refs = rest[2 * n_pg:3 * n_pg]
    y_ref, o_ref = rest[3 * n_pg:3 * n_pg + 2]
    i = pl.program_id(0)

    def output_stage():
        pos0 = (i * tm) % seq
        u = u_ref[...]
        halo = jnp.where(pos0 == 0, 0.0, halo_ref[...])
        z = jnp.concatenate([halo, u], axis=0)
        pos = pos0 + lax.broadcasted_iota(jnp.int32, (tm, 1), 0)
        gw = u.shape[1] // len(POOL_WINDOWS)
        d_groups = [_pool_diff(z[:, gi * gw:(gi + 1) * gw], u[:, gi * gw:(gi + 1) * gw], pos, w)
                    for gi, w in enumerate(POOL_WINDOWS)]
        y_ref[...] = _mix_tail(d_groups, gp_ref[...], ga_ref[...], att_ref[...], ma_ref[...], mb_ref[...], x_ref[...],
                               wpool_ref, ps_ref, wup_ref, wua_ref, wout_ref)

    _decode_step(i, pl.num_programs(0), q_ref, kn_ref, vn_ref, lfn_ref, kc_refs, vc_refs, lf_refs, o_ref,
                 rest[3 * n_pg + 2:], n_heads=n_heads, n_new=n_new, between=output_stage)


def _out_sample_kernel(z_ref, gp_ref, ga_ref, ma_ref, mb_ref, att_ref, x_ref,
                       wpool_ref, ps_ref, wup_ref, wua_ref, wout_ref, y_ref, *, dec_b, n_new, pos0):
    gw = z_ref.shape[2] // len(POOL_WINDOWS)
    pos = pos0 + lax.broadcasted_iota(jnp.int32, (n_new, 1), 0)
    per_seq = []
    for b in range(dec_b):
        z = z_ref[b]
        u = z[HALO:, :]
        per_seq.append([_pool_diff(z[:, gi * gw:(gi + 1) * gw], u[:, gi * gw:(gi + 1) * gw], pos, w)
                        for gi, w in enumerate(POOL_WINDOWS)])
    d_groups = [jnp.concatenate([per_seq[b][gi] for b in range(dec_b)], axis=0) for gi in range(len(POOL_WINDOWS))]
    y_ref[...] = _mix_tail(d_groups, gp_ref[...], ga_ref[...], att_ref[...], ma_ref[...], mb_ref[...], x_ref[...],
                           wpool_ref, ps_ref, wup_ref, wua_ref, wout_ref)


def _weight_specs(pw, gw, dm):
    one = pl.Buffered(1)
    return [
        pl.BlockSpec((len(POOL_WINDOWS), gw, gw), lambda *_: (0, 0, 0), pipeline_mode=one),
        pl.BlockSpec((1, pw), lambda *_: (0, 0), pipeline_mode=one),
        pl.BlockSpec((pw, dm), lambda *_: (0, 0), pipeline_mode=one),
        pl.BlockSpec((pw, dm), lambda *_: (0, 0), pipeline_mode=one),
        pl.BlockSpec((dm, dm), lambda *_: (0, 0), pipeline_mode=one),
    ]


def _out_prompt(hg, u, att, x2d, wpool, ps, wup, wua, wout, page_table, hs, lf_new, cache_kt, cache_vt, cache_lft,
                *, seq, tm, cols, n_heads, n_new, decode_seq):
    m, dm = x2d.shape
    pw = att.shape[1]
    gw = pw // len(POOL_WINDOWS)
    hb = tm // HALO
    n_steps = m // tm
    n_pages = page_table.shape[1]
    assert n_pages % n_steps == 0
    n_pg = n_pages // n_steps
    rows = n_heads * n_new
    kern = functools.partial(_out_prompt_kernel, tm=tm, seq=seq, n_heads=n_heads, n_new=n_new, n_pg=n_pg)
    in_specs = [
        pl.BlockSpec((tm, pw), lambda i, pt: (i, 0)),
        pl.BlockSpec((HALO, pw), lambda i, pt: (jnp.maximum(i * hb - 1, 0), 0)),
        pl.BlockSpec((tm, pw), lambda i, pt: (i, cols["gp"] // pw)),
        pl.BlockSpec((tm, pw), lambda i, pt: (i, cols["ga"] // pw)),
        pl.BlockSpec((tm, dm), lambda i, pt: (i, cols["ma"] // dm)),
        pl.BlockSpec((tm, dm), lambda i, pt: (i, cols["mb"] // dm)),
        pl.BlockSpec((tm, pw), lambda i, pt: (i, 0)),
        pl.BlockSpec((tm, dm), lambda i, pt: (i, 0)),
    ] + _weight_specs(pw, gw, dm) + _decode_in_specs(
        lambda i: decode_seq, lambda i: i, n_heads=n_heads, n_new=n_new, n_pg=n_pg, n_pages=n_pages, cols=cols,
        n_lead=1)
    grid_spec = pltpu.PrefetchScalarGridSpec(
        num_scalar_prefetch=1,
        grid=(n_steps,),
        in_specs=in_specs,
        out_specs=[
            pl.BlockSpec((tm, dm), lambda i, pt: (i, 0)),
            pl.BlockSpec((None, rows, HEAD_DIM), lambda i, pt: (0, 0, 0)),
        ],
        scratch_shapes=_decode_scratch(n_heads, n_new),
    )
    args = [page_table, u, u, hg, hg, hg, hg, att, x2d, wpool, ps, wup, wua, wout, hs, hs, hs, lf_new]
    args += [cache_kt] * n_pg + [cache_vt] * n_pg + [cache_lft] * n_pg
    return pl.pallas_call(
        kern,
        grid_spec=grid_spec,
        out_shape=[
            jax.ShapeDtypeStruct((m, dm), F32),
            jax.ShapeDtypeStruct((1, rows, HEAD_DIM), F32),
        ],
        compiler_params=_cparams(1),
        name="out_prompt",
    )(*args)


def _out_sample(z, h, att, x2d, wpool, ps, wup, wua, wout, *, dec_b, n_new, pos0, cols):
    m, dm = x2d.shape
    pw = att.shape[1]
    gw = pw // len(POOL_WINDOWS)
    kern = functools.partial(_out_sample_kernel, dec_b=dec_b, n_new=n_new, pos0=pos0)
    return pl.pallas_call(
        kern,
        grid=(1,),
        in_specs=[
            pl.BlockSpec((dec_b, HALO + n_new, pw), lambda i: (0, 0, 0)),
            pl.BlockSpec((m, pw), lambda i: (0, cols["gp"] // pw)),
            pl.BlockSpec((m, pw), lambda i: (0, cols["ga"] // pw)),
            pl.BlockSpec((m, dm), lambda i: (0, cols["ma"] // dm)),
            pl.BlockSpec((m, dm), lambda i: (0, cols["mb"] // dm)),
            pl.BlockSpec((m, pw), lambda i: (0, 0)),
            pl.BlockSpec((m, dm), lambda i: (0, 0)),
        ] + _weight_specs(pw, gw, dm),
        out_specs=pl.BlockSpec((m, dm), lambda i: (0, 0)),
        out_shape=jax.ShapeDtypeStruct((m, dm), F32),
        compiler_params=_cparams(1),
        name="out_sample",
    )(z, h, h, h, h, att, x2d, wpool, ps, wup, wua, wout)


def kernel(x_prompt, x_sample, cache_k, cache_v, cache_logf, state_pool, page_table, norm_gain, w_in, b_f,
           q_norm_gain, k_norm_gain, w_pool_map, pool_scale, w_up_pool, w_up_attn, w_out):
    batch, seq, dm = x_prompt.shape
    dec_b, n_new, _ = x_sample.shape
    assert w_in.shape[0] == 1
    n_heads = b_f.shape[1]
    aw = n_heads * HEAD_DIM
    pw = w_up_pool.shape[1]
    n_pages = page_table.shape[1]
    past_len = n_pages * PAGE_SIZE
    assert pw == aw and dm == 2 * pw and n_pages % DECODE_PAGES_FUSED == 0

    tn = 512
    plan = _inproj_plan(pw, aw, dm, n_heads, tn)
    cols = plan["cols"]
    w_t = w_in[0].T
    fl0, fl_n = plan["fl_rows"]
    wfl = jnp.pad(w_t[fl0:fl0 + fl_n], ((0, V7X_LANES - n_heads), (0, 0))).astype(BF16)
    bfp = jnp.pad(b_f, ((0, 0), (0, V7X_LANES - n_heads)))
    reps = tn // HEAD_DIM
    qg = jnp.tile(q_norm_gain[0], reps)[None, :]
    kg = jnp.tile(k_norm_gain[0], reps)[None, :]
    qgc = jnp.broadcast_to(jnp.tile(q_norm_gain[0], n_heads)[:, None], (aw, V7X_LANES))
    kgc = jnp.broadcast_to(jnp.tile(k_norm_gain[0], n_heads)[:, None], (aw, V7X_LANES))
    seg = np.arange(tn) // HEAD_DIM
    bd = jnp.asarray((seg[:, None] == seg[None, :]).astype(np.float32)).astype(BF16)
    wpool = w_pool_map[0].astype(BF16)
    wup = w_up_pool[0].astype(BF16)
    wua = w_up_attn[0].astype(BF16)
    wout = w_out[0].astype(BF16)

    ms = dec_b * n_new
    xs2 = x_sample.reshape(ms, dm)
    hs, lfs, w_tok, w_fm = _inproj_new(plan, xs2, norm_gain, w_t, wfl, bfp, qg, kg, bd)

    cache_kt = cache_k.transpose(0, 1, 3, 4, 2)
    cache_vt = cache_v.transpose(0, 1, 3, 4, 2)
    cache_lft = cache_logf.transpose(0, 1, 3, 2)
    decode_args = dict(n_heads=n_heads, n_new=n_new, n_pg=DECODE_PAGES_FUSED)

    xp2 = x_prompt.reshape(batch * seq, dm)
    xn_p, lfp = _prenorm(xp2, norm_gain, wfl, bfp, tm=1024)
    hg_p, u_p, att_rows_a, n_fused = _h_decode(plan, page_table, xn_p, w_tok, hs, lfs, cache_kt, cache_vt,
                                               cache_lft, tm=1024, **decode_args)
    qt_p, kt_p, vt_p = _inproj_fm(plan, xn_p, w_fm, qgc, kgc, tm=1024, seq=seq, width=aw)
    fcol, ft, lft = _fcum(lfp, batch=batch, seq=seq, n_heads=n_heads)
    ft_pairs = ft.reshape(batch, n_heads // 2, 2, seq)
    att_p = _prompt_attention(qt_p, kt_p, vt_p, fcol, ft_pairs, batch=batch, seq=seq, n_heads=n_heads, tq=256,
                              n_pair=ATTN_PAIRS_PER_STEP)
    assert n_fused == dec_b - 1
    yp, att_rows_b = _out_prompt(hg_p, u_p, att_p, xp2, wpool, pool_scale, wup, wua, wout, page_table, hs, lfs,
                                 cache_kt, cache_vt, cache_lft, seq=seq, tm=256, cols=cols, n_heads=n_heads,
                                 n_new=n_new, decode_seq=n_fused)

    k_s = hs[:, cols["k"]:cols["k"] + aw].reshape(dec_b, n_new, n_heads, HEAD_DIM)
    v_s = hs[:, cols["v"]:cols["v"] + aw].reshape(dec_b, n_new, n_heads, HEAD_DIM)
    u_s = hs[:, cols["u"]:cols["u"] + pw].reshape(dec_b, n_new, pw)
    logf_s = lfs[:, :n_heads].reshape(dec_b, n_new, n_heads)
    att_rows = jnp.concatenate([att_rows_a, att_rows_b], axis=0)
    att_s = att_rows.reshape(dec_b, n_heads, n_new, HEAD_DIM).transpose(0, 2, 1, 3).reshape(ms, aw)
    z_s = jnp.concatenate([jnp.zeros((dec_b, HALO - POOL_BUF, pw), F32), state_pool[0], u_s], axis=1)
    ys = _out_sample(z_s, hs, att_s, xs2, wpool, pool_scale, wup, wua, wout,
                     dec_b=dec_b, n_new=n_new, pos0=past_len, cols=cols)

    k_p = kt_p.reshape(1, batch, n_heads, HEAD_DIM, seq).transpose(0, 1, 4, 2, 3)
    v_p = vt_p.reshape(1, batch, n_heads, HEAD_DIM, seq).transpose(0, 1, 4, 2, 3)
    logf_p = lft.transpose(0, 2, 1)[None]
    pool_p = u_p.reshape(batch, seq, pw)[:, seq - POOL_BUF:, :][None]
    pool_s = z_s[:, HALO + n_new - POOL_BUF:, :][None]
    return (yp.reshape(batch, seq, dm), ys.reshape(dec_b, n_new, dm), k_p, v_p, logf_p, pool_p,
            k_s[None], v_s[None], logf_s[None], pool_s)
```

```python
import functools

import jax
import jax.numpy as jnp
import numpy as np
from jax import lax
from jax.experimental import pallas as pl
from jax.experimental.pallas import tpu as pltpu

F32 = jnp.float32
BF16 = jnp.bfloat16

HEAD_DIM = 64
POOL_WINDOWS = (2, 4, 8, 16)
POOL_BUF = 15
PAGE_SIZE = 128
EPS = 1e-6
NEG_INF = -1e30
ATTN_SCALE = HEAD_DIM ** -0.5
LOG2E = 1.4426950408889634

V7X_LANES = 128
V7X_VMEM_LIMIT_BYTES = 56 * 1024 * 1024

ROW_ALIGN = 16
HALO = 16
DECODE_PAGES_FUSED = 8
N_AUG = 3
ATTN_PAIRS_PER_STEP = 8


def _cparams(n_grid_axes):
    return pltpu.CompilerParams(
        dimension_semantics=("arbitrary",) * n_grid_axes,
        vmem_limit_bytes=V7X_VMEM_LIMIT_BYTES,
    )


def _split3(x):
    hi = x.astype(BF16)
    r1 = x - hi.astype(F32)
    mid = r1.astype(BF16)
    lo = (r1 - mid.astype(F32)).astype(BF16)
    return hi, mid, lo


def _dot(a, b):
    return jnp.dot(a, b, preferred_element_type=F32)


def _dot_nt(a, b):
    return lax.dot_general(a, b, (((1,), (1,)), ((), ())), preferred_element_type=F32)


def _sigmoid(x):
    return 1.0 / (1.0 + jnp.exp(-x))


def _silu(x):
    return x * _sigmoid(x)


def _suffix_sum_lanes(x):
    n = x.shape[-1]
    ax = x.ndim - 1
    lane = lax.broadcasted_iota(jnp.int32, x.shape, ax)
    k = 1
    while k < n:
        shifted = pltpu.roll(x, n - k, ax)
        x = x + jnp.where(lane < n - k, shifted, 0.0)
        k *= 2
    return x


def _divmod_nonneg(x, n):
    assert n > 0 and n & (n - 1) == 0
    return lax.shift_right_logical(x, n.bit_length() - 1), x & (n - 1)


def _scan_rows(x):
    n = x.shape[0]
    row = lax.broadcasted_iota(jnp.int32, x.shape, 0)
    k = 1
    while k < n:
        x = x + jnp.where(row >= k, pltpu.roll(x, k, 0), 0.0)
        k *= 2
    return x


def _norm_and_logf(x_ref, ng_ref, wfl_ref, bf_ref, xn_sc, lf_ref):
    x = x_ref[...]
    ms = jnp.mean(x * x, axis=-1, keepdims=True)
    xn = x * lax.rsqrt(ms + EPS) * ng_ref[...]
    xn_sc[...] = xn.astype(BF16)
    z = _dot_nt(xn_sc[...], wfl_ref[...]) + bf_ref[...]
    lf_ref[...] = jnp.minimum(z, 0.0) - jnp.log1p(jnp.exp(-jnp.abs(z)))


def _step_in(j, rng):
    return jnp.logical_and(j >= rng[0], j < rng[1])


def _inproj_new_kernel(off_ref, x_ref, ng_ref, w_ref, wfl_ref, bf_ref, qg_ref, kg_ref, bd_ref,
                       h_ref, lf_ref, wtok_ref, wfm_ref, xn_sc, *, steps):
    del off_ref
    j = pl.program_id(0)
    n_tok = steps["q"][0]

    @pl.when(j == 0)
    def _():
        _norm_and_logf(x_ref, ng_ref, wfl_ref, bf_ref, xn_sc, lf_ref)

    w = w_ref[...].astype(BF16)
    acc = _dot_nt(xn_sc[...], w)

    @pl.when(j < n_tok)
    def _():
        wtok_ref[...] = w_ref[...].T.astype(BF16)

    @pl.when(j >= n_tok)
    def _():
        wfm_ref[...] = w

    def head_norm(gain):
        sq = acc * acc
        hi = sq.astype(BF16)
        lo = (sq - hi.astype(F32)).astype(BF16)
        ss = _dot(hi, bd_ref[...]) + _dot(lo, bd_ref[...])
        return acc * lax.rsqrt(ss * (1.0 / HEAD_DIM) + EPS) * gain

    is_q = _step_in(j, steps["q"])
    is_k = _step_in(j, steps["k"])

    @pl.when(is_q)
    def _():
        h_ref[...] = head_norm(qg_ref[...])

    @pl.when(is_k)
    def _():
        h_ref[...] = head_norm(kg_ref[...])

    @pl.when(jnp.logical_not(is_q | is_k))
    def _():
        h_ref[...] = acc


def _inproj_new(plan, x2d, ng, w_t, wfl, bfp, qg, kg, bd):
    m, d = x2d.shape
    tn = plan["tn"]
    nj = plan["n_steps"]
    kern = functools.partial(_inproj_new_kernel, steps=plan["steps"])
    n_tok = plan["steps"]["q"][0]
    grid_spec = pltpu.PrefetchScalarGridSpec(
        num_scalar_prefetch=1,
        grid=(nj,),
        in_specs=[
            pl.BlockSpec((m, d), lambda j, off: (0, 0)),
            pl.BlockSpec((1, d), lambda j, off: (0, 0)),
            pl.BlockSpec((pl.Element(tn), pl.Element(d)), lambda j, off: (off[j] * ROW_ALIGN, 0)),
            pl.BlockSpec((V7X_LANES, d), lambda j, off: (0, 0)),
            pl.BlockSpec((1, V7X_LANES), lambda j, off: (0, 0)),
            pl.BlockSpec((1, tn), lambda j, off: (0, 0)),
            pl.BlockSpec((1, tn), lambda j, off: (0, 0)),
            pl.BlockSpec((tn, tn), lambda j, off: (0, 0)),
        ],
        out_specs=[
            pl.BlockSpec((m, tn), lambda j, off: (0, j)),
            pl.BlockSpec((m, V7X_LANES), lambda j, off: (0, 0)),
            pl.BlockSpec((None, d, tn), lambda j, off: (jnp.minimum(j, n_tok - 1), 0, 0)),
            pl.BlockSpec((tn, d), lambda j, off: (jnp.maximum(j - n_tok, 0), 0)),
        ],
        scratch_shapes=[pltpu.VMEM((m, d), BF16)],
    )
    return pl.pallas_call(
        kern,
        grid_spec=grid_spec,
        out_shape=[
            jax.ShapeDtypeStruct((m, nj * tn), F32),
            jax.ShapeDtypeStruct((m, V7X_LANES), F32),
            jax.ShapeDtypeStruct((n_tok, d, tn), BF16),
            jax.ShapeDtypeStruct(((nj - n_tok) * tn, d), BF16),
        ],
        compiler_params=_cparams(1),
        name="inproj_new",
    )(plan["row_offsets"], x2d, ng, w_t, wfl, bfp, qg, kg, bd)


def _prenorm_kernel(x_ref, ng_ref, wfl_ref, bf_ref, xn_ref, lf_ref):
    _norm_and_logf(x_ref, ng_ref, wfl_ref, bf_ref, xn_ref, lf_ref)


def _prenorm(x2d, ng, wfl, bfp, *, tm):
    m, d = x2d.shape
    return pl.pallas_call(
        _prenorm_kernel,
        grid=(m // tm,),
        in_specs=[
            pl.BlockSpec((tm, d), lambda i: (i, 0)),
            pl.BlockSpec((1, d), lambda i: (0, 0)),
            pl.BlockSpec((V7X_LANES, d), lambda i: (0, 0)),
            pl.BlockSpec((1, V7X_LANES), lambda i: (0, 0)),
        ],
        out_specs=[
            pl.BlockSpec((tm, d), lambda i: (i, 0)),
            pl.BlockSpec((tm, V7X_LANES), lambda i: (i, 0)),
        ],
        out_shape=[
            jax.ShapeDtypeStruct((m, d), BF16),
            jax.ShapeDtypeStruct((m, V7X_LANES), F32),
        ],
        compiler_params=_cparams(1),
        name="prenorm",
    )(x2d, ng, wfl, bfp)


def _inproj_fm_kernel(xn_ref, w_ref, qgc_ref, kgc_ref, qt_ref, kt_ref, vt_ref):
    j = pl.program_id(1)

    def head_norm(acc, gain_col_ref):
        tn, tm = acc.shape
        a3 = acc.reshape(tn // HEAD_DIM, HEAD_DIM, tm)
        ms = jnp.mean(a3 * a3, axis=1, keepdims=True)
        g3 = gain_col_ref[...].reshape(tn // HEAD_DIM, HEAD_DIM, V7X_LANES)[:, :, 0:1]
        return (a3 * lax.rsqrt(ms + EPS) * g3).reshape(tn, tm)

    @pl.when(j == 0)
    def _():
        qt_ref[...] = (head_norm(_dot_nt(w_ref[...], xn_ref[...]), qgc_ref) * (ATTN_SCALE * LOG2E)).astype(BF16)

    @pl.when(j == 1)
    def _():
        kt_ref[...] = head_norm(_dot_nt(w_ref[...], xn_ref[...]), kgc_ref)

    @pl.when(j == 2)
    def _():
        vt_ref[...] = _dot_nt(w_ref[...], xn_ref[...])


def _inproj_fm(plan, xn, wbf, qgc, kgc, *, tm, seq, width):
    m, d = xn.shape
    assert wbf.shape[0] == 3 * width
    nb = seq // tm
    out_spec = pl.BlockSpec((None, width, tm), lambda i, j: (_divmod_nonneg(i, nb)[0], 0, _divmod_nonneg(i, nb)[1]))
    return pl.pallas_call(
        _inproj_fm_kernel,
        grid=(m // tm, 3),
        in_specs=[
            pl.BlockSpec((tm, d), lambda i, j: (i, 0)),
            pl.BlockSpec((width, d), lambda i, j: (j, 0)),
            pl.BlockSpec((width, V7X_LANES), lambda i, j: (0, 0)),
            pl.BlockSpec((width, V7X_LANES), lambda i, j: (0, 0)),
        ],
        out_specs=[out_spec] * 3,
        out_shape=[jax.ShapeDtypeStruct((m // seq, width, seq), dt) for dt in (BF16, F32, F32)],
        compiler_params=_cparams(2),
        name="inproj_fm",
    )(xn, wbf, qgc, kgc)


def _inproj_plan(pw, aw, dm, n_heads, tn):
    src, off = {}, 0
    for name, size in (("u", pw), ("gp", pw), ("q", aw), ("k", aw), ("v", aw), ("ga", aw), ("fl", n_heads),
                       ("ma", dm), ("mb", dm)):
        src[name] = (off, size)
        off += size
    order = ("ma", "mb", "gp", "ga", "u", "q", "k", "v")
    row_offsets, steps, cols = [], {}, {}
    for name in order:
        base, size = src[name]
        steps[name] = (len(row_offsets), len(row_offsets) + size // tn)
        cols[name] = len(row_offsets) * tn
        row_offsets += [base + o for o in range(0, size, tn)]
    assert all(r % ROW_ALIGN == 0 for r in row_offsets)
    return {"tn": tn, "steps": steps, "n_steps": len(row_offsets), "cols": cols, "fl_rows": src["fl"],
            "row_offsets": jnp.asarray([r // ROW_ALIGN for r in row_offsets], jnp.int32)}


def _fcum_kernel(lf_ref, fcol_ref, ft_ref, lft_ref, *, seq, n_heads):
    c = V7X_LANES
    row = lax.broadcasted_iota(jnp.int32, (c, c), 0)
    col = lax.broadcasted_iota(jnp.int32, (c, c), 1)
    tri = jnp.where(col <= row, 1.0, 0.0).astype(BF16)
    carry = jnp.zeros((1, c), F32)
    for ci in range(seq // c):
        x = lf_ref[ci * c:(ci + 1) * c, :]
        hi, mid, lo = _split3(x)
        fc = (_dot(tri, hi) + _dot(tri, mid)) + _dot(tri, lo) + carry
        fcol_ref[ci * c:(ci + 1) * c, :] = fc
        ft_ref[:, ci * c:(ci + 1) * c] = fc.T[:n_heads, :]
        lft_ref[:, ci * c:(ci + 1) * c] = x.T[:n_heads, :]
        carry = fc[c - 1:c, :]


def _fcum(lf2d, *, batch, seq, n_heads):
    kern = functools.partial(_fcum_kernel, seq=seq, n_heads=n_heads)
    return pl.pallas_call(
        kern,
        grid=(batch,),
        in_specs=[pl.BlockSpec((seq, V7X_LANES), lambda b: (b, 0))],
        out_specs=[
            pl.BlockSpec((seq, V7X_LANES), lambda b: (b, 0)),
            pl.BlockSpec((None, n_heads, seq), lambda b: (b, 0, 0)),
            pl.BlockSpec((None, n_heads, seq), lambda b: (b, 0, 0)),
        ],
        out_shape=[
            jax.ShapeDtypeStruct((batch * seq, V7X_LANES), F32),
            jax.ShapeDtypeStruct((batch, n_heads, seq), F32),
            jax.ShapeDtypeStruct((batch, n_heads, seq), F32),
        ],
        compiler_params=_cparams(1),
        name="forget_cumsum",
    )(lf2d)


def _attn_kernel(qt_ref, kt_ref, vt_ref, fcol_ref, ft_ref, o_ref, ka_sc, *, tq, seq, n_pair):
    g = pl.program_id(1)
    i = pl.program_id(2)
    d = HEAD_DIM
    lanes = V7X_LANES
    lane1 = lax.broadcasted_iota(jnp.int32, (1, lanes), 1)

    @pl.when(i == 0)
    def _():
        prow = lax.broadcasted_iota(jnp.int32, (N_AUG * lanes, lanes), 0)
        lane = lax.broadcasted_iota(jnp.int32, (N_AUG * lanes, lanes), 1)
        piece, hrow = prow // lanes, prow % lanes
        ones = jnp.where(((lane1 >= d) & (lane1 < d + N_AUG)) | (lane1 < N_AUG), 1.0, 0.0)
        sels = []
        for pr in range(n_pair):
            head0 = 2 * (g * n_pair + pr)
            hit = ((hrow == head0) & (lane == d + N_AUG + piece)) | ((hrow == head0 + 1) & (lane == N_AUG + piece))
            sels.append(jnp.where(hit, -1.0, 0.0).astype(BF16))
        ck = 256
        for c in range(seq // ck):
            pieces = jnp.concatenate(_split3(fcol_ref[c * ck:(c + 1) * ck, :] * LOG2E), axis=1)
            for pr in range(n_pair):
                k_rows = kt_ref[pr * lanes:(pr + 1) * lanes, c * ck:(c + 1) * ck].T
                aug = ones + _dot(pieces, sels[pr])
                ka_sc[2 * pr, c * ck:(c + 1) * ck, :] = jnp.where(lane1 < d, k_rows, aug).astype(BF16)
                ka_sc[2 * pr + 1, c * ck:(c + 1) * ck, :] = jnp.where(lane1 >= d, k_rows, aug).astype(BF16)

    qs = pl.multiple_of(i * tq, tq)
    rowi = lax.broadcasted_iota(jnp.int32, (d, tq), 0)

    def aug_rows(f_row):
        hi, mid, lo = (x.astype(F32) for x in _split3(f_row))
        return jnp.where(rowi == 0, hi, jnp.where(rowi == 1, mid, jnp.where(rowi == 2, lo,
                         jnp.where(rowi < 2 * N_AUG, 1.0, 0.0))))

    qa = []
    for pr in range(n_pair):
        qt = qt_ref[pr * lanes:(pr + 1) * lanes, :]
        fq = ft_ref[pr, :, pl.ds(qs, tq)] * LOG2E
        qa.append(jnp.concatenate([qt[:d, :], aug_rows(fq[0:1, :]).astype(BF16)], axis=0))
        qa.append(jnp.concatenate([aug_rows(fq[1:2, :]).astype(BF16), qt[d:, :]], axis=0))

    heads = range(2 * n_pair)

    def step(j, carry, tk=tq):
        ks = pl.multiple_of(j * tk, tk)
        scores = [_dot(ka_sc[hh, pl.ds(ks, tk), :], qa[hh]) for hh in heads]
        stats, probs = [], []
        for hh in heads:
            m, l, _ = carry[hh]
            m_new = jnp.maximum(m, jnp.max(scores[hh], axis=0, keepdims=True))
            alpha = jnp.exp2(m - m_new)
            pt = jnp.exp2(scores[hh] - m_new)
            stats.append((m_new, alpha * l + jnp.sum(pt, axis=0, keepdims=True), alpha))
            probs.append(pt.astype(BF16))
        pvs = [_dot(vt_ref[hh * d:(hh + 1) * d, pl.ds(ks, tk)].astype(BF16), probs[hh]) for hh in heads]
        return tuple((stats[hh][0], stats[hh][1], stats[hh][2] * carry[hh][2] + pvs[hh]) for hh in heads)

    def diagonal_step(carry):
        hq = tq // 2
        ka0 = pl.multiple_of(i * tq, tq)
        kb0 = pl.multiple_of(i * tq + hq, hq)
        tri = lax.broadcasted_iota(jnp.int32, (hq, hq), 0) <= lax.broadcasted_iota(jnp.int32, (hq, hq), 1)
        first = [_dot(ka_sc[hh, pl.ds(ka0, hq), :], qa[hh]) for hh in heads]
        second = [_dot(ka_sc[hh, pl.ds(kb0, hq), :], qa[hh][:, hq:]) for hh in heads]
        stats, probs_a, probs_b = [], [], []
        for hh in heads:
            m, l, _ = carry[hh]
            sa = jnp.concatenate([jnp.where(tri, first[hh][:, :hq], NEG_INF), first[hh][:, hq:]], axis=1)
            sb = jnp.where(tri, second[hh], NEG_INF)
            tile_max = jnp.max(sa, axis=0, keepdims=True)
            tile_max = jnp.concatenate([tile_max[:, :hq],
                                        jnp.maximum(tile_max[:, hq:], jnp.max(sb, axis=0, keepdims=True))], axis=1)
            m_new = jnp.maximum(m, tile_max)
            alpha = jnp.exp2(m - m_new)
            pa = jnp.exp2(sa - m_new)
            pb = jnp.exp2(sb - m_new[:, hq:])
            psum = jnp.sum(pa, axis=0, keepdims=True)
            psum = jnp.concatenate([psum[:, :hq], psum[:, hq:] + jnp.sum(pb, axis=0, keepdims=True)], axis=1)
            stats.append((m_new, alpha * l + psum, alpha))
            probs_a.append(pa.astype(BF16))
            probs_b.append(pb.astype(BF16))
        out = []
        for hh in heads:
            pv = _dot(vt_ref[hh * d:(hh + 1) * d, pl.ds(ka0, hq)].astype(BF16), probs_a[hh])
            pvb = _dot(vt_ref[hh * d:(hh + 1) * d, pl.ds(kb0, hq)].astype(BF16), probs_b[hh])
            pv = jnp.concatenate([pv[:, :hq], pv[:, hq:] + pvb], axis=1)
            out.append((stats[hh][0], stats[hh][1], stats[hh][2] * carry[hh][2] + pv))
        return tuple(out)

    init = tuple((jnp.full((1, tq), NEG_INF, F32), jnp.zeros((1, tq), F32), jnp.zeros((d, tq), F32))
                 for _ in range(2 * n_pair))
    carry = lax.fori_loop(0, lax.shift_right_logical(i, 1), lambda j, c: step(j, c, 2 * tq), init)
    carry = lax.cond((i & 1) == 1, lambda c: step(i - 1, c), lambda c: c, carry)
    final = diagonal_step(carry)
    o_ref[...] = jnp.concatenate([acc / l for (_, l, acc) in final], axis=0).T.astype(o_ref.dtype)


def _prompt_attention(qt, kt, vt, fcol, ft_pairs, *, batch, seq, n_heads, tq, n_pair):
    n_groups = n_heads // (2 * n_pair)
    qn = seq // tq
    rows = n_pair * V7X_LANES
    kern = functools.partial(_attn_kernel, tq=tq, seq=seq, n_pair=n_pair)
    return pl.pallas_call(
        kern,
        grid=(batch, n_groups, qn),
        in_specs=[
            pl.BlockSpec((None, rows, tq), lambda b, g, i: (b, g, i)),
            pl.BlockSpec((None, rows, seq), lambda b, g, i: (b, g, 0)),
            pl.BlockSpec((None, rows, seq), lambda b, g, i: (b, g, 0)),
            pl.BlockSpec((seq, V7X_LANES), lambda b, g, i: (b, 0)),
            pl.BlockSpec((None, n_pair, 2, seq), lambda b, g, i: (b, g, 0, 0)),
        ],
        out_specs=pl.BlockSpec((tq, rows), lambda b, g, i: (b * qn + i, g)),
        out_shape=jax.ShapeDtypeStruct((batch * seq, n_heads * HEAD_DIM), BF16),
        scratch_shapes=[pltpu.VMEM((2 * n_pair, seq, V7X_LANES), BF16)],
        compiler_params=_cparams(3),
        name="prompt_attention",
    )(qt, kt, vt, fcol, ft_pairs)


def _decode_step(cj, n_chunks, q_ref, kn_ref, vn_ref, lfn_ref, kc_refs, vc_refs, lf_refs, o_ref, scratch,
                 *, n_heads, n_new, between=None):
    m_sc, l_sc, acc_sc, carry_sc, qbd_sc, qb_sc = scratch
    rows = n_heads * n_new
    feat = n_heads * HEAD_DIM

    def tile_rows(x):
        return jnp.concatenate([x] * n_heads, axis=0)

    def rep_rows(x):
        return jnp.concatenate([jnp.broadcast_to(x[h:h + 1, :], (n_new, x.shape[1])) for h in range(n_heads)], axis=0)

    def to_pages(x):
        xp = jnp.concatenate([x, jnp.zeros((PAGE_SIZE - n_new, feat), F32)], axis=0)
        return jnp.concatenate([xp[:, c * PAGE_SIZE:(c + 1) * PAGE_SIZE].T for c in range(feat // PAGE_SIZE)], axis=0)

    def score_phase(k_pages, bias):
        s = _dot(qbd_sc[...], k_pages) + bias
        m_prev = m_sc[...]
        m_new = jnp.maximum(m_prev, jnp.max(s, axis=1, keepdims=True))
        alpha = jnp.exp(m_prev - m_new)
        p = jnp.exp(s - m_new)
        l_sc[...] = alpha * l_sc[...] + jnp.sum(p, axis=1, keepdims=True)
        m_sc[...] = m_new
        return p, alpha

    def value_phase(v_pages, p, alpha):
        acc_sc[...] = alpha * acc_sc[...] + _dot_nt(p.astype(BF16), v_pages)

    @pl.when(cj == 0)
    def _():
        m_sc[...] = jnp.full(m_sc.shape, NEG_INF, F32)
        l_sc[...] = jnp.zeros(l_sc.shape, F32)
        acc_sc[...] = jnp.zeros(acc_sc.shape, F32)
        carry_sc[...] = jnp.zeros(carry_sc.shape, F32)
        row_head = lax.broadcasted_iota(jnp.int32, (rows, feat), 0) // n_new
        col_head = lax.broadcasted_iota(jnp.int32, (rows, feat), 1) // HEAD_DIM
        qbd_sc[...] = jnp.where(row_head == col_head, tile_rows(q_ref[...] * ATTN_SCALE), 0.0).astype(BF16)
        f_new = _scan_rows(lfn_ref[...])
        rh = lax.broadcasted_iota(jnp.int32, (rows, V7X_LANES), 0) // n_new
        ln = lax.broadcasted_iota(jnp.int32, (rows, V7X_LANES), 1)
        qb = jnp.sum(jnp.where(rh == ln, tile_rows(f_new), 0.0), axis=1, keepdims=True)
        qb_sc[...] = qb
        f_pad = jnp.concatenate([f_new, jnp.zeros((PAGE_SIZE - n_new, V7X_LANES), F32)], axis=0)
        f_new_t = f_pad.T[:n_heads, :]
        tok = lax.broadcasted_iota(jnp.int32, (rows, PAGE_SIZE), 0) % n_new
        pos = lax.broadcasted_iota(jnp.int32, (rows, PAGE_SIZE), 1)
        bias = (qb - rep_rows(f_new_t)) + jnp.where(pos <= tok, 0.0, NEG_INF)
        p, alpha = score_phase(to_pages(kn_ref[...]).astype(BF16), bias)
        value_phase(to_pages(vn_ref[...]).astype(BF16), p, alpha)

    carry = carry_sc[...]
    biases = []
    for lf_ref in lf_refs:
        lf = lf_ref[...]
        incl = _suffix_sum_lanes(lf)
        biases.append(rep_rows((incl - lf) + carry))
        carry = carry + incl[:, 0:1]
    carry_sc[...] = carry
    k_pages = jnp.concatenate([r[...].reshape(feat, PAGE_SIZE).astype(BF16) for r in kc_refs], axis=1)
    p, alpha = score_phase(k_pages, qb_sc[...] + jnp.concatenate(biases, axis=1))
    if between is not None:
        between()
    v_pages = jnp.concatenate([r[...].reshape(feat, PAGE_SIZE).astype(BF16) for r in vc_refs], axis=1)
    value_phase(v_pages, p, alpha)

    @pl.when(cj == n_chunks - 1)
    def _():
        outs = [acc_sc[h * n_new:(h + 1) * n_new, h * HEAD_DIM:(h + 1) * HEAD_DIM] for h in range(n_heads)]
        o_ref[...] = jnp.concatenate(outs, axis=0) / l_sc[...]


def _decode_scratch(n_heads, n_new):
    rows = n_heads * n_new
    feat = n_heads * HEAD_DIM
    return [
        pltpu.VMEM((rows, 1), F32),
        pltpu.VMEM((rows, 1), F32),
        pltpu.VMEM((rows, feat), F32),
        pltpu.VMEM((n_heads, PAGE_SIZE), F32),
        pltpu.VMEM((rows, feat), BF16),
        pltpu.VMEM((rows, 1), F32),
    ]


def _decode_in_specs(seq_of, chunk_of, *, n_heads, n_new, n_pg, n_pages, cols, n_lead):
    feat = n_heads * HEAD_DIM
    last = n_pages - 1

    def tok_spec(width, col):
        return pl.BlockSpec((n_new, width), lambda *a: (seq_of(*a[:n_lead]), col))

    def page_spec(g, tail):
        def idx(*a):
            pt = a[n_lead]
            page = pt[seq_of(*a[:n_lead]), last - (chunk_of(*a[:n_lead]) * n_pg + g)]
            return (0, page) + (0,) * (1 + len(tail))
        return pl.BlockSpec((None, None, n_heads) + tail, idx)

    kv_tail = (HEAD_DIM, PAGE_SIZE)
    specs = [tok_spec(feat, cols["q"] // feat), tok_spec(feat, cols["k"] // feat), tok_spec(feat, cols["v"] // feat),
             tok_spec(V7X_LANES, 0)]
    specs += [page_spec(g, kv_tail) for g in range(n_pg)]
    specs += [page_spec(g, kv_tail) for g in range(n_pg)]
    specs += [page_spec(g, (PAGE_SIZE,)) for g in range(n_pg)]
    return specs


def _h_decode_kernel(pt_ref, xn_ref, w_ref, q_ref, kn_ref, vn_ref, lfn_ref, *rest, n_heads, n_new, n_pg, n_j,
                     n_gate, n_chunks):
    del pt_ref
    kc_refs = rest[0:n_pg]
    vc_refs = rest[n_pg:2 * n_pg]
    lf_refs = rest[2 * n_pg:3 * n_pg]
    hg_ref, u_ref, o_ref = rest[3 * n_pg:3 * n_pg + 3]
    j = pl.program_id(1)
    cj = _divmod_nonneg(pl.program_id(0) * n_j + j, n_chunks)[1]

    def step(store):
        _decode_step(cj, n_chunks, q_ref, kn_ref, vn_ref, lfn_ref, kc_refs, vc_refs, lf_refs, o_ref,
                     rest[3 * n_pg + 3:], n_heads=n_heads, n_new=n_new,
                     between=lambda: store(_dot(xn_ref[...], w_ref[...])))

    def store_gate(acc):
        hg_ref[...] = acc.astype(BF16)

    def store_u(acc):
        u_ref[...] = acc

    pl.when(j < n_gate)(lambda: step(store_gate))
    pl.when(j >= n_gate)(lambda: step(store_u))


def _h_decode(plan, page_table, xn, wbf, hs, lf_new, cache_kt, cache_vt, cache_lft, *, tm, n_heads, n_new, n_pg):
    m, d = xn.shape
    tn = plan["tn"]
    cols = plan["cols"]
    n_j = plan["steps"]["q"][0]
    n_pages = page_table.shape[1]
    n_chunks = n_pages // n_pg
    n_steps = (m // tm) * n_j
    assert n_steps % n_chunks == 0 and n_steps // n_chunks <= page_table.shape[0]
    n_seq = n_steps // n_chunks
    rows = n_heads * n_new
    n_gate = plan["steps"]["u"][0]
    assert plan["steps"]["u"][1] == n_j
    kern = functools.partial(_h_decode_kernel, n_heads=n_heads, n_new=n_new, n_pg=n_pg, n_j=n_j, n_gate=n_gate,
                             n_chunks=n_chunks)
    in_specs = [
        pl.BlockSpec((tm, d), lambda i, j, pt: (i, 0)),
        pl.BlockSpec((None, d, tn), lambda i, j, pt: (j, 0, 0)),
    ] + _decode_in_specs(lambda i, j: _divmod_nonneg(i * n_j + j, n_chunks)[0],
                         lambda i, j: _divmod_nonneg(i * n_j + j, n_chunks)[1],
                         n_heads=n_heads, n_new=n_new, n_pg=n_pg, n_pages=n_pages, cols=cols, n_lead=2)
    grid_spec = pltpu.PrefetchScalarGridSpec(
        num_scalar_prefetch=1,
        grid=(m // tm, n_j),
        in_specs=in_specs,
        out_specs=[
            pl.BlockSpec((tm, tn), lambda i, j, pt: (i, jnp.minimum(j, n_gate - 1))),
            pl.BlockSpec((tm, tn), lambda i, j, pt: (i, jnp.maximum(j - n_gate, 0))),
            pl.BlockSpec((None, rows, HEAD_DIM), lambda i, j, pt: (_divmod_nonneg(i * n_j + j, n_chunks)[0], 0, 0)),
        ],
        scratch_shapes=_decode_scratch(n_heads, n_new),
    )
    args = [page_table, xn, wbf, hs, hs, hs, lf_new] + [cache_kt] * n_pg + [cache_vt] * n_pg + [cache_lft] * n_pg
    hg, u, att_rows = pl.pallas_call(
        kern,
        grid_spec=grid_spec,
        out_shape=[
            jax.ShapeDtypeStruct((m, n_gate * tn), BF16),
            jax.ShapeDtypeStruct((m, (n_j - n_gate) * tn), F32),
            jax.ShapeDtypeStruct((n_seq, rows, HEAD_DIM), F32),
        ],
        compiler_params=_cparams(2),
        name="h_decode",
    )(*args)
    return hg, u, att_rows, n_seq


def _pool_diff(z, u, pos, group_w):
    s = z
    k = 1
    while k < group_w:
        s = s + pltpu.roll(s, k, 0)
        k *= 2
    cnt = jnp.minimum(pos + 1, group_w).astype(F32)
    return s[HALO:, :] / cnt - u


def _mix_tail(d_groups, gp, ga, att, ma, mb, x, wpool_ref, ps_ref, wup_ref, wua_ref, wout_ref):
    mixed = [_dot(d.astype(BF16), wpool_ref[gi]) for gi, d in enumerate(d_groups)]
    mixed = jnp.concatenate(mixed, axis=1)
    branch_a = (mixed * ps_ref[...]) * _silu(gp.astype(F32))
    branch_b = att.astype(F32) * _silu(ga.astype(F32))
    up_a = _dot(branch_a.astype(BF16), wup_ref[...])
    up_b = _dot(branch_b.astype(BF16), wua_ref[...])
    merged = _sigmoid(ma.astype(F32)) * up_a + _sigmoid(mb.astype(F32)) * up_b
    return x + _dot(merged.astype(BF16), wout_ref[...])


def _out_prompt_kernel(pt_ref, u_ref, halo_ref, gp_ref, ga_ref, ma_ref, mb_ref, att_ref, x_ref,
                       wpool_ref, ps_ref, wup_ref, wua_ref, wout_ref, q_ref, kn_ref, vn_ref, lfn_ref, *rest,
                       tm, seq, n_heads, n_new, n_pg):
    del pt_ref
    kc_refs = rest[0:n_pg]
    vc_refs = rest[n_pg:2 * n_pg]
    lf_refs = rest[2 * n_pg:3 * n_pg]
    y_ref, o_ref = rest[3 * n_pg:3 * n_pg + 2]
    i = pl.program_id(0)

    def output_stage():
        pos0 = (i * tm) % seq
        u = u_ref[...]
        halo = jnp.where(pos0 == 0, 0.0, halo_ref[...])
        z = jnp.concatenate([halo, u], axis=0)
        pos = pos0 + lax.broadcasted_iota(jnp.int32, (tm, 1), 0)
        gw = u.shape[1] // len(POOL_WINDOWS)
        d_groups = [_pool_diff(z[:, gi * gw:(gi + 1) * gw], u[:, gi * gw:(gi + 1) * gw], pos, w)
                    for gi, w in enumerate(POOL_WINDOWS)]
        y_ref[...] = _mix_tail(d_groups, gp_ref[...], ga_ref[...], att_ref[...], ma_ref[...], mb_ref[...], x_ref[...],
                               wpool_ref, ps_ref, wup_ref, wua_ref, wout_ref)

    _decode_step(i, pl.num_programs(0), q_ref, kn_ref, vn_ref, lfn_ref, kc_refs, vc_refs, lf_refs, o_ref,
                 rest[3 * n_pg + 2:], n_heads=n_heads, n_new=n_new, between=output_stage)


def _out_sample_kernel(z_ref, gp_ref, ga_ref, ma_ref, mb_ref, att_ref, x_ref,
                       wpool_ref, ps_ref, wup_ref, wua_ref, wout_ref, y_ref, *, dec_b, n_new, pos0):
    gw = z_ref.shape[2] // len(POOL_WINDOWS)
    pos = pos0 + lax.broadcasted_iota(jnp.int32, (n_new, 1), 0)
    per_seq = []
    for b in range(dec_b):
        z = z_ref[b]
        u = z[HALO:, :]
        per_seq.append([_pool_diff(z[:, gi * gw:(gi + 1) * gw], u[:, gi * gw:(gi + 1) * gw], pos, w)
                        for gi, w in enumerate(POOL_WINDOWS)])
    d_groups = [jnp.concatenate([per_seq[b][gi] for b in range(dec_b)], axis=0) for gi in range(len(POOL_WINDOWS))]
    y_ref[...] = _mix_tail(d_groups, gp_ref[...], ga_ref[...], att_ref[...], ma_ref[...], mb_ref[...], x_ref[...],
                           wpool_ref, ps_ref, wup_ref, wua_ref, wout_ref)


def _weight_specs(pw, gw, dm):
    one = pl.Buffered(1)
    return [
        pl.BlockSpec((len(POOL_WINDOWS), gw, gw), lambda *_: (0, 0, 0), pipeline_mode=one),
        pl.BlockSpec((1, pw), lambda *_: (0, 0), pipeline_mode=one),
        pl.BlockSpec((pw, dm), lambda *_: (0, 0), pipeline_mode=one),
        pl.BlockSpec((pw, dm), lambda *_: (0, 0), pipeline_mode=one),
        pl.BlockSpec((dm, dm), lambda *_: (0, 0), pipeline_mode=one),
    ]


def _out_prompt(hg, u, att, x2d, wpool, ps, wup, wua, wout, page_table, hs, lf_new, cache_kt, cache_vt, cache_lft,
                *, seq, tm, cols, n_heads, n_new, decode_seq):
    m, dm = x2d.shape
    pw = att.shape[1]
    gw = pw // len(POOL_WINDOWS)
    hb = tm // HALO
    n_steps = m // tm
    n_pages = page_table.shape[1]
    assert n_pages % n_steps == 0
    n_pg = n_pages // n_steps
    rows = n_heads * n_new
    kern = functools.partial(_out_prompt_kernel, tm=tm, seq=seq, n_heads=n_heads, n_new=n_new, n_pg=n_pg)
    in_specs = [
        pl.BlockSpec((tm, pw), lambda i, pt: (i, 0)),
        pl.BlockSpec((HALO, pw), lambda i, pt: (jnp.maximum(i * hb - 1, 0), 0)),
        pl.BlockSpec((tm, pw), lambda i, pt: (i, cols["gp"] // pw)),
        pl.BlockSpec((tm, pw), lambda i, pt: (i, cols["ga"] // pw)),
        pl.BlockSpec((tm, dm), lambda i, pt: (i, cols["ma"] // dm)),
        pl.BlockSpec((tm, dm), lambda i, pt: (i, cols["mb"] // dm)),
        pl.BlockSpec((tm, pw), lambda i, pt: (i, 0)),
        pl.BlockSpec((tm, dm), lambda i, pt: (i, 0)),
    ] + _weight_specs(pw, gw, dm) + _decode_in_specs(
        lambda i: decode_seq, lambda i: i, n_heads=n_heads, n_new=n_new, n_pg=n_pg, n_pages=n_pages, cols=cols,
        n_lead=1)
    grid_spec = pltpu.PrefetchScalarGridSpec(
        num_scalar_prefetch=1,
        grid=(n_steps,),
        in_specs=in_specs,
        out_specs=[
            pl.BlockSpec((tm, dm), lambda i, pt: (i, 0)),
            pl.BlockSpec((None, rows, HEAD_DIM), lambda i, pt: (0, 0, 0)),
        ],
        scratch_shapes=_decode_scratch(n_heads, n_new),
    )
    args = [page_table, u, u, hg, hg, hg, hg, att, x2d, wpool, ps, wup, wua, wout, hs, hs, hs, lf_new]
    args += [cache_kt] * n_pg + [cache_vt] * n_pg + [cache_lft] * n_pg
    return pl.pallas_call(
        kern,
        grid_spec=grid_spec,
        out_shape=[
            jax.ShapeDtypeStruct((m, dm), F32),
            jax.ShapeDtypeStruct((1, rows, HEAD_DIM), F32),
        ],
        compiler_params=_cparams(1),
        name="out_prompt",
    )(*args)


def _out_sample(z, h, att, x2d, wpool, ps, wup, wua, wout, *, dec_b, n_new, pos0, cols):
    m, dm = x2d.shape
    pw = att.shape[1]
    gw = pw // len(POOL_WINDOWS)
    kern = functools.partial(_out_sample_kernel, dec_b=dec_b, n_new=n_new, pos0=pos0)
    return pl.pallas_call(
        kern,
        grid=(1,),
        in_specs=[
            pl.BlockSpec((dec_b, HALO + n_new, pw), lambda i: (0, 0, 0)),
            pl.BlockSpec((m, pw), lambda i: (0, cols["gp"] // pw)),
            pl.BlockSpec((m, pw), lambda i: (0, cols["ga"] // pw)),
            pl.BlockSpec((m, dm), lambda i: (0, cols["ma"] // dm)),
            pl.BlockSpec((m, dm), lambda i: (0, cols["mb"] // dm)),
            pl.BlockSpec((m, pw), lambda i: (0, 0)),
            pl.BlockSpec((m, dm), lambda i: (0, 0)),
        ] + _weight_specs(pw, gw, dm),
        out_specs=pl.BlockSpec((m, dm), lambda i: (0, 0)),
        out_shape=jax.ShapeDtypeStruct((m, dm), F32),
        compiler_params=_cparams(1),
        name="out_sample",
    )(z, h, h, h, h, att, x2d, wpool, ps, wup, wua, wout)


def kernel(x_prompt, x_sample, cache_k, cache_v, cache_logf, state_pool, page_table, norm_gain, w_in, b_f,
           q_norm_gain, k_norm_gain, w_pool_map, pool_scale, w_up_pool, w_up_attn, w_out):
    batch, seq, dm = x_prompt.shape
    dec_b, n_new, _ = x_sample.shape
    assert w_in.shape[0] == 1
    n_heads = b_f.shape[1]
    aw = n_heads * HEAD_DIM
    pw = w_up_pool.shape[1]
    n_pages = page_table.shape[1]
    past_len = n_pages * PAGE_SIZE
    assert pw == aw and dm == 2 * pw and n_pages % DECODE_PAGES_FUSED == 0

    tn = 512
    plan = _inproj_plan(pw, aw, dm, n_heads, tn)
    cols = plan["cols"]
    w_t = w_in[0].T
    fl0, fl_n = plan["fl_rows"]
    wfl = jnp.pad(w_t[fl0:fl0 + fl_n], ((0, V7X_LANES - n_heads), (0, 0))).astype(BF16)
    bfp = jnp.pad(b_f, ((0, 0), (0, V7X_LANES - n_heads)))
    reps = tn // HEAD_DIM
    qg = jnp.tile(q_norm_gain[0], reps)[None, :]
    kg = jnp.tile(k_norm_gain[0], reps)[None, :]
    qgc = jnp.broadcast_to(jnp.tile(q_norm_gain[0], n_heads)[:, None], (aw, V7X_LANES))
    kgc = jnp.broadcast_to(jnp.tile(k_norm_gain[0], n_heads)[:, None], (aw, V7X_LANES))
    seg = np.arange(tn) // HEAD_DIM
    bd = jnp.asarray((seg[:, None] == seg[None, :]).astype(np.float32)).astype(BF16)
    wpool = w_pool_map[0].astype(BF16)
    wup = w_up_pool[0].astype(BF16)
    wua = w_up_attn[0].astype(BF16)
    wout = w_out[0].astype(BF16)

    ms = dec_b * n_new
    xs2 = x_sample.reshape(ms, dm)
    hs, lfs, w_tok, w_fm = _inproj_new(plan, xs2, norm_gain, w_t, wfl, bfp, qg, kg, bd)

    cache_kt = cache_k.transpose(0, 1, 3, 4, 2)
    cache_vt = cache_v.transpose(0, 1, 3, 4, 2)
    cache_lft = cache_logf.transpose(0, 1, 3, 2)
    decode_args = dict(n_heads=n_heads, n_new=n_new, n_pg=DECODE_PAGES_FUSED)

    xp2 = x_prompt.reshape(batch * seq, dm)
    xn_p, lfp = _prenorm(xp2, norm_gain, wfl, bfp, tm=1024)
    hg_p, u_p, att_rows_a, n_fused = _h_decode(plan, page_table, xn_p, w_tok, hs, lfs, cache_kt, cache_vt,
                                               cache_lft, tm=1024, **decode_args)
    qt_p, kt_p, vt_p = _inproj_fm(plan, xn_p, w_fm, qgc, kgc, tm=1024, seq=seq, width=aw)
    fcol, ft, lft = _fcum(lfp, batch=batch, seq=seq, n_heads=n_heads)
    ft_pairs = ft.reshape(batch, n_heads // 2, 2, seq)
    att_p = _prompt_attention(qt_p, kt_p, vt_p, fcol, ft_pairs, batch=batch, seq=seq, n_heads=n_heads, tq=256,
                              n_pair=ATTN_PAIRS_PER_STEP)
    assert n_fused == dec_b - 1
    yp, att_rows_b = _out_prompt(hg_p, u_p, att_p, xp2, wpool, pool_scale, wup, wua, wout, page_table, hs, lfs,
                                 cache_kt, cache_vt, cache_lft, seq=seq, tm=256, cols=cols, n_heads=n_heads,
                                 n_new=n_new, decode_seq=n_fused)

    k_s = hs[:, cols["k"]:cols["k"] + aw].reshape(dec_b, n_new, n_heads, HEAD_DIM)
    v_s = hs[:, cols["v"]:cols["v"] + aw].reshape(dec_b, n_new, n_heads, HEAD_DIM)
    u_s = hs[:, cols["u"]:cols["u"] + pw].reshape(dec_b, n_new, pw)
    logf_s = lfs[:, :n_heads].reshape(dec_b, n_new, n_heads)
    att_rows = jnp.concatenate([att_rows_a, att_rows_b], axis=0)
    att_s = att_rows.reshape(dec_b, n_heads, n_new, HEAD_DIM).transpose(0, 2, 1, 3).reshape(ms, aw)
    z_s = jnp.concatenate([jnp.zeros((dec_b, HALO - POOL_BUF, pw), F32), state_pool[0], u_s], axis=1)
    ys = _out_sample(z_s, hs, att_s, xs2, wpool, pool_scale, wup, wua, wout,
                     dec_b=dec_b, n_new=n_new, pos0=past_len, cols=cols)

    k_p = kt_p.reshape(1, batch, n_heads, HEAD_DIM, seq).transpose(0, 1, 4, 2, 3)
    v_p = vt_p.reshape(1, batch, n_heads, HEAD_DIM, seq).transpose(0, 1, 4, 2, 3)
    logf_p = lft.transpose(0, 2, 1)[None]
    pool_p = u_p.reshape(batch, seq, pw)[:, seq - POOL_BUF:, :][None]
    pool_s = z_s[:, HALO + n_new - POOL_BUF:, :][None]
    return (yp.reshape(batch, seq, dm), ys.reshape(dec_b, n_new, dm), k_p, v_p, logf_p, pool_p,
            k_s[None], v_s[None], logf_s[None], pool_s)
```

```python
import functools

import jax
import jax.numpy as jnp
import numpy as np
from jax import lax
from jax.experimental import pallas as pl
from jax.experimental.pallas import tpu as pltpu

F32 = jnp.float32
BF16 = jnp.bfloat16

HEAD_DIM = 64
POOL_WINDOWS = (2, 4, 8, 16)
POOL_BUF = 15
PAGE_SIZE = 128
EPS = 1e-6
NEG_INF = -1e30
ATTN_SCALE = HEAD_DIM ** -0.5
LOG2E = 1.4426950408889634

V7X_LANES = 128
V7X_VMEM_LIMIT_BYTES = 56 * 1024 * 1024

ROW_ALIGN = 16
HALO = 16
DECODE_PAGES_FUSED = 8
N_AUG = 3
ATTN_PAIRS_PER_STEP = 8


def _cparams(n_grid_axes):
    return pltpu.CompilerParams(
        dimension_semantics=("arbitrary",) * n_grid_axes,
        vmem_limit_bytes=V7X_VMEM_LIMIT_BYTES,
    )


def _split3(x):
    hi = x.astype(BF16)
    r1 = x - hi.astype(F32)
    mid = r1.astype(BF16)
    lo = (r1 - mid.astype(F32)).astype(BF16)
    return hi, mid, lo


def _dot(a, b):
    return jnp.dot(a, b, preferred_element_type=F32)


def _dot_nt(a, b):
    return lax.dot_general(a, b, (((1,), (1,)), ((), ())), preferred_element_type=F32)


def _sigmoid(x):
    return 1.0 / (1.0 + jnp.exp(-x))


def _silu(x):
    return x * _sigmoid(x)


def _suffix_sum_lanes(x):
    n = x.shape[-1]
    ax = x.ndim - 1
    lane = lax.broadcasted_iota(jnp.int32, x.shape, ax)
    k = 1
    while k < n:
        shifted = pltpu.roll(x, n - k, ax)
        x = x + jnp.where(lane < n - k, shifted, 0.0)
        k *= 2
    return x


def _divmod_nonneg(x, n):
    assert n > 0 and n & (n - 1) == 0
    return lax.shift_right_logical(x, n.bit_length() - 1), x & (n - 1)


def _scan_rows(x):
    n = x.shape[0]
    row = lax.broadcasted_iota(jnp.int32, x.shape, 0)
    k = 1
    while k < n:
        x = x + jnp.where(row >= k, pltpu.roll(x, k, 0), 0.0)
        k *= 2
    return x


def _norm_and_logf(x_ref, ng_ref, wfl_ref, bf_ref, xn_sc, lf_ref):
    x = x_ref[...]
    ms = jnp.mean(x * x, axis=-1, keepdims=True)
    xn = x * lax.rsqrt(ms + EPS) * ng_ref[...]
    xn_sc[...] = xn.astype(BF16)
    z = _dot_nt(xn_sc[...], wfl_ref[...]) + bf_ref[...]
    lf_ref[...] = jnp.minimum(z, 0.0) - jnp.log1p(jnp.exp(-jnp.abs(z)))


def _step_in(j, rng):
    return jnp.logical_and(j >= rng[0], j < rng[1])


def _inproj_new_kernel(off_ref, x_ref, ng_ref, w_ref, wfl_ref, bf_ref, qg_ref, kg_ref, bd_ref,
                       h_ref, lf_ref, wtok_ref, wfm_ref, xn_sc, *, steps):
    del off_ref
    j = pl.program_id(0)
    n_tok = steps["q"][0]

    @pl.when(j == 0)
    def _():
        _norm_and_logf(x_ref, ng_ref, wfl_ref, bf_ref, xn_sc, lf_ref)

    w = w_ref[...].astype(BF16)
    acc = _dot_nt(xn_sc[...], w)

    @pl.when(j < n_tok)
    def _():
        wtok_ref[...] = w_ref[...].T.astype(BF16)

    @pl.when(j >= n_tok)
    def _():
        wfm_ref[...] = w

    def head_norm(gain):
        sq = acc * acc
        hi = sq.astype(BF16)
        lo = (sq - hi.astype(F32)).astype(BF16)
        ss = _dot(hi, bd_ref[...]) + _dot(lo, bd_ref[...])
        return acc * lax.rsqrt(ss * (1.0 / HEAD_DIM) + EPS) * gain

    is_q = _step_in(j, steps["q"])
    is_k = _step_in(j, steps["k"])

    @pl.when(is_q)
    def _():
        h_ref[...] = head_norm(qg_ref[...])

    @pl.when(is_k)
    def _():
        h_ref[...] = head_norm(kg_ref[...])

    @pl.when(jnp.logical_not(is_q | is_k))
    def _():
        h_ref[...] = acc


def _inproj_new(plan, x2d, ng, w_t, wfl, bfp, qg, kg, bd):
    m, d = x2d.shape
    tn = plan["tn"]
    nj = plan["n_steps"]
    kern = functools.partial(_inproj_new_kernel, steps=plan["steps"])
    n_tok = plan["steps"]["q"][0]
    grid_spec = pltpu.PrefetchScalarGridSpec(
        num_scalar_prefetch=1,
        grid=(nj,),
        in_specs=[
            pl.BlockSpec((m, d), lambda j, off: (0, 0)),
            pl.BlockSpec((1, d), lambda j, off: (0, 0)),
            pl.BlockSpec((pl.Element(tn), pl.Element(d)), lambda j, off: (off[j] * ROW_ALIGN, 0)),
            pl.BlockSpec((V7X_LANES, d), lambda j, off: (0, 0)),
            pl.BlockSpec((1, V7X_LANES), lambda j, off: (0, 0)),
            pl.BlockSpec((1, tn), lambda j, off: (0, 0)),
            pl.BlockSpec((1, tn), lambda j, off: (0, 0)),
            pl.BlockSpec((tn, tn), lambda j, off: (0, 0)),
        ],
        out_specs=[
            pl.BlockSpec((m, tn), lambda j, off: (0, j)),
            pl.BlockSpec((m, V7X_LANES), lambda j, off: (0, 0)),
            pl.BlockSpec((None, d, tn), lambda j, off: (jnp.minimum(j, n_tok - 1), 0, 0)),
            pl.BlockSpec((tn, d), lambda j, off: (jnp.maximum(j - n_tok, 0), 0)),
        ],
        scratch_shapes=[pltpu.VMEM((m, d), BF16)],
    )
    return pl.pallas_call(
        kern,
        grid_spec=grid_spec,
        out_shape=[
            jax.ShapeDtypeStruct((m, nj * tn), F32),
            jax.ShapeDtypeStruct((m, V7X_LANES), F32),
            jax.ShapeDtypeStruct((n_tok, d, tn), BF16),
            jax.ShapeDtypeStruct(((nj - n_tok) * tn, d), BF16),
        ],
        compiler_params=_cparams(1),
        name="inproj_new",
    )(plan["row_offsets"], x2d, ng, w_t, wfl, bfp, qg, kg, bd)


def _prenorm_kernel(x_ref, ng_ref, wfl_ref, bf_ref, xn_ref, lf_ref):
    _norm_and_logf(x_ref, ng_ref, wfl_ref, bf_ref, xn_ref, lf_ref)


def _prenorm(x2d, ng, wfl, bfp, *, tm):
    m, d = x2d.shape
    return pl.pallas_call(
        _prenorm_kernel,
        grid=(m // tm,),
        in_specs=[
            pl.BlockSpec((tm, d), lambda i: (i, 0)),
            pl.BlockSpec((1, d), lambda i: (0, 0)),
            pl.BlockSpec((V7X_LANES, d), lambda i: (0, 0)),
            pl.BlockSpec((1, V7X_LANES), lambda i: (0, 0)),
        ],
        out_specs=[
            pl.BlockSpec((tm, d), lambda i: (i, 0)),
            pl.BlockSpec((tm, V7X_LANES), lambda i: (i, 0)),
        ],
        out_shape=[
            jax.ShapeDtypeStruct((m, d), BF16),
            jax.ShapeDtypeStruct((m, V7X_LANES), F32),
        ],
        compiler_params=_cparams(1),
        name="prenorm",
    )(x2d, ng, wfl, bfp)


def _inproj_fm_kernel(xn_ref, w_ref, qgc_ref, kgc_ref, qt_ref, kt_ref, vt_ref):
    j = pl.program_id(1)

    def head_norm(acc, gain_col_ref):
        tn, tm = acc.shape
        a3 = acc.reshape(tn // HEAD_DIM, HEAD_DIM, tm)
        ms = jnp.mean(a3 * a3, axis=1, keepdims=True)
        g3 = gain_col_ref[...].reshape(tn // HEAD_DIM, HEAD_DIM, V7X_LANES)[:, :, 0:1]
        return (a3 * lax.rsqrt(ms + EPS) * g3).reshape(tn, tm)

    @pl.when(j == 0)
    def _():
        qt_ref[...] = (head_norm(_dot_nt(w_ref[...], xn_ref[...]), qgc_ref) * (ATTN_SCALE * LOG2E)).astype(BF16)

    @pl.when(j == 1)
    def _():
        kt_ref[...] = head_norm(_dot_nt(w_ref[...], xn_ref[...]), kgc_ref)

    @pl.when(j == 2)
    def _():
        vt_ref[...] = _dot_nt(w_ref[...], xn_ref[...])


def _inproj_fm(plan, xn, wbf, qgc, kgc, *, tm, seq, width):
    m, d = xn.shape
    assert wbf.shape[0] == 3 * width
    nb = seq // tm
    out_spec = pl.BlockSpec((None, width, tm), lambda i, j: (_divmod_nonneg(i, nb)[0], 0, _divmod_nonneg(i, nb)[1]))
    return pl.pallas_call(
        _inproj_fm_kernel,
        grid=(m // tm, 3),
        in_specs=[
            pl.BlockSpec((tm, d), lambda i, j: (i, 0)),
            pl.BlockSpec((width, d), lambda i, j: (j, 0)),
            pl.BlockSpec((width, V7X_LANES), lambda i, j: (0, 0)),
            pl.BlockSpec((width, V7X_LANES), lambda i, j: (0, 0)),
        ],
        out_specs=[out_spec] * 3,
        out_shape=[jax.ShapeDtypeStruct((m // seq, width, seq), dt) for dt in (BF16, F32, F32)],
        compiler_params=_cparams(2),
        name="inproj_fm",
    )(xn, wbf, qgc, kgc)


def _inproj_plan(pw, aw, dm, n_heads, tn):
    src, off = {}, 0
    for name, size in (("u", pw), ("gp", pw), ("q", aw), ("k", aw), ("v", aw), ("ga", aw), ("fl", n_heads),
                       ("ma", dm), ("mb", dm)):
        src[name] = (off, size)
        off += size
    order = ("ma", "mb", "gp", "ga", "u", "q", "k", "v")
    row_offsets, steps, cols = [], {}, {}
    for name in order:
        base, size = src[name]
        steps[name] = (len(row_offsets), len(row_offsets) + size // tn)
        cols[name] = len(row_offsets) * tn
        row_offsets += [base + o for o in range(0, size, tn)]
    assert all(r % ROW_ALIGN == 0 for r in row_offsets)
    return {"tn": tn, "steps": steps, "n_steps": len(row_offsets), "cols": cols, "fl_rows": src["fl"],
            "row_offsets": jnp.asarray([r // ROW_ALIGN for r in row_offsets], jnp.int32)}


def _fcum_kernel(lf_ref, fcol_ref, ft_ref, lft_ref, *, seq, n_heads):
    c = V7X_LANES
    row = lax.broadcasted_iota(jnp.int32, (c, c), 0)
    col = lax.broadcasted_iota(jnp.int32, (c, c), 1)
    tri = jnp.where(col <= row, 1.0, 0.0).astype(BF16)
    carry = jnp.zeros((1, c), F32)
    for ci in range(seq // c):
        x = lf_ref[ci * c:(ci + 1) * c, :]
        hi, mid, lo = _split3(x)
        fc = (_dot(tri, hi) + _dot(tri, mid)) + _dot(tri, lo) + carry
        fcol_ref[ci * c:(ci + 1) * c, :] = fc
        ft_ref[:, ci * c:(ci + 1) * c] = fc.T[:n_heads, :]
        lft_ref[:, ci * c:(ci + 1) * c] = x.T[:n_heads, :]
        carry = fc[c - 1:c, :]


def _fcum(lf2d, *, batch, seq, n_heads):
    kern = functools.partial(_fcum_kernel, seq=seq, n_heads=n_heads)
    return pl.pallas_call(
        kern,
        grid=(batch,),
        in_specs=[pl.BlockSpec((seq, V7X_LANES), lambda b: (b, 0))],
        out_specs=[
            pl.BlockSpec((seq, V7X_LANES), lambda b: (b, 0)),
            pl.BlockSpec((None, n_heads, seq), lambda b: (b, 0, 0)),
            pl.BlockSpec((None, n_heads, seq), lambda b: (b, 0, 0)),
        ],
        out_shape=[
            jax.ShapeDtypeStruct((batch * seq, V7X_LANES), F32),
            jax.ShapeDtypeStruct((batch, n_heads, seq), F32),
            jax.ShapeDtypeStruct((batch, n_heads, seq), F32),
        ],
        compiler_params=_cparams(1),
        name="forget_cumsum",
    )(lf2d)


def _attn_kernel(qt_ref, kt_ref, vt_ref, fcol_ref, ft_ref, o_ref, ka_sc, *, tq, seq, n_pair):
    g = pl.program_id(1)
    i = pl.program_id(2)
    d = HEAD_DIM
    lanes = V7X_LANES
    lane1 = lax.broadcasted_iota(jnp.int32, (1, lanes), 1)

    @pl.when(i == 0)
    def _():
        prow = lax.broadcasted_iota(jnp.int32, (N_AUG * lanes, lanes), 0)
        lane = lax.broadcasted_iota(jnp.int32, (N_AUG * lanes, lanes), 1)
        piece, hrow = prow // lanes, prow % lanes
        ones = jnp.where(((lane1 >= d) & (lane1 < d + N_AUG)) | (lane1 < N_AUG), 1.0, 0.0)
        sels = []
        for pr in range(n_pair):
            head0 = 2 * (g * n_pair + pr)
            hit = ((hrow == head0) & (lane == d + N_AUG + piece)) | ((hrow == head0 + 1) & (lane == N_AUG + piece))
            sels.append(jnp.where(hit, -1.0, 0.0).astype(BF16))
        ck = 256
        for c in range(seq // ck):
            pieces = jnp.concatenate(_split3(fcol_ref[c * ck:(c + 1) * ck, :] * LOG2E), axis=1)
            for pr in range(n_pair):
                k_rows = kt_ref[pr * lanes:(pr + 1) * lanes, c * ck:(c + 1) * ck].T
                aug = ones + _dot(pieces, sels[pr])
                ka_sc[2 * pr, c * ck:(c + 1) * ck, :] = jnp.where(lane1 < d, k_rows, aug).astype(BF16)
                ka_sc[2 * pr + 1, c * ck:(c + 1) * ck, :] = jnp.where(lane1 >= d, k_rows, aug).astype(BF16)

    qs = pl.multiple_of(i * tq, tq)
    rowi = lax.broadcasted_iota(jnp.int32, (d, tq), 0)

    def aug_rows(f_row):
        hi, mid, lo = (x.astype(F32) for x in _split3(f_row))
        return jnp.where(rowi == 0, hi, jnp.where(rowi == 1, mid, jnp.where(rowi == 2, lo,
                         jnp.where(rowi < 2 * N_AUG, 1.0, 0.0))))

    qa = []
    for pr in range(n_pair):
        qt = qt_ref[pr * lanes:(pr + 1) * lanes, :]
        fq = ft_ref[pr, :, pl.ds(qs, tq)] * LOG2E
        qa.append(jnp.concatenate([qt[:d, :], aug_rows(fq[0:1, :]).astype(BF16)], axis=0))
        qa.append(jnp.concatenate([aug_rows(fq[1:2, :]).astype(BF16), qt[d:, :]], axis=0))

    heads = range(2 * n_pair)

    def step(j, carry):
        ks = pl.multiple_of(j * tq, tq)
        scores = [_dot(ka_sc[hh, pl.ds(ks, tq), :], qa[hh]) for hh in heads]
        stats, probs = [], []
        for hh in heads:
            m, l, _ = carry[hh]
            halves = []
            for c0 in range(0, tq, lanes):
                cs = slice(c0, c0 + lanes)
                s_h = scores[hh][:, cs]
                m_h = jnp.maximum(m[:, cs], jnp.max(s_h, axis=0, keepdims=True))
                a_h = jnp.exp2(m[:, cs] - m_h)
                p_h = jnp.exp2(s_h - m_h)
                halves.append((m_h, a_h * l[:, cs] + jnp.sum(p_h, axis=0, keepdims=True), a_h, p_h.astype(BF16)))
            m_new, l_new, alpha, pt = (jnp.concatenate([h[k] for h in halves], axis=1) for k in range(4))
            stats.append((m_new, l_new, alpha))
            probs.append(pt)
        pvs = [_dot(vt_ref[hh * d:(hh + 1) * d, pl.ds(ks, tq)].astype(BF16), probs[hh]) for hh in heads]
        return tuple((stats[hh][0], stats[hh][1], stats[hh][2] * carry[hh][2] + pvs[hh]) for hh in heads)

    def diagonal_step(carry):
        hq = tq // 2
        ka0 = pl.multiple_of(i * tq, tq)
        kb0 = pl.multiple_of(i * tq + hq, hq)
        tri = lax.broadcasted_iota(jnp.int32, (hq, hq), 0) <= lax.broadcasted_iota(jnp.int32, (hq, hq), 1)
        first = [_dot(ka_sc[hh, pl.ds(ka0, hq), :], qa[hh]) for hh in heads]
        second = [_dot(ka_sc[hh, pl.ds(kb0, hq), :], qa[hh][:, hq:]) for hh in heads]
        stats, probs_a, probs_b = [], [], []
        for hh in heads:
            m, l, _ = carry[hh]
            sa = jnp.concatenate([jnp.where(tri, first[hh][:, :hq], NEG_INF), first[hh][:, hq:]], axis=1)
            sb = jnp.where(tri, second[hh], NEG_INF)
            tile_max = jnp.max(sa, axis=0, keepdims=True)
            tile_max = jnp.concatenate([tile_max[:, :hq],
                                        jnp.maximum(tile_max[:, hq:], jnp.max(sb, axis=0, keepdims=True))], axis=1)
            m_new = jnp.maximum(m, tile_max)
            alpha = jnp.exp2(m - m_new)
            pa = jnp.exp2(sa - m_new)
            pb = jnp.exp2(sb - m_new[:, hq:])
            psum = jnp.sum(pa, axis=0, keepdims=True)
            psum = jnp.concatenate([psum[:, :hq], psum[:, hq:] + jnp.sum(pb, axis=0, keepdims=True)], axis=1)
            stats.append((m_new, alpha * l + psum, alpha))
            probs_a.append(pa.astype(BF16))
            probs_b.append(pb.astype(BF16))
        out = []
        for hh in heads:
            pv = _dot(vt_ref[hh * d:(hh + 1) * d, pl.ds(ka0, hq)].astype(BF16), probs_a[hh])
            pvb = _dot(vt_ref[hh * d:(hh + 1) * d, pl.ds(kb0, hq)].astype(BF16), probs_b[hh])
            pv = jnp.concatenate([pv[:, :hq], pv[:, hq:] + pvb], axis=1)
            out.append((stats[hh][0], stats[hh][1], stats[hh][2] * carry[hh][2] + pv))
        return tuple(out)

    init = tuple((jnp.full((1, tq), NEG_INF, F32), jnp.zeros((1, tq), F32), jnp.zeros((d, tq), F32))
                 for _ in range(2 * n_pair))
    carry = lax.fori_loop(0, i, step, init)
    final = diagonal_step(carry)
    o_ref[...] = jnp.concatenate([acc / l for (_, l, acc) in final], axis=0).T.astype(o_ref.dtype)


def _prompt_attention(qt, kt, vt, fcol, ft_pairs, *, batch, seq, n_heads, tq, n_pair):
    n_groups = n_heads // (2 * n_pair)
    qn = seq // tq
    rows = n_pair * V7X_LANES
    kern = functools.partial(_attn_kernel, tq=tq, seq=seq, n_pair=n_pair)
    return pl.pallas_call(
        kern,
        grid=(batch, n_groups, qn),
        in_specs=[
            pl.BlockSpec((None, rows, tq), lambda b, g, i: (b, g, i)),
            pl.BlockSpec((None, rows, seq), lambda b, g, i: (b, g, 0)),
            pl.BlockSpec((None, rows, seq), lambda b, g, i: (b, g, 0)),
            pl.BlockSpec((seq, V7X_LANES), lambda b, g, i: (b, 0)),
            pl.BlockSpec((None, n_pair, 2, seq), lambda b, g, i: (b, g, 0, 0)),
        ],
        out_specs=pl.BlockSpec((tq, rows), lambda b, g, i: (b * qn + i, g)),
        out_shape=jax.ShapeDtypeStruct((batch * seq, n_heads * HEAD_DIM), BF16),
        scratch_shapes=[pltpu.VMEM((2 * n_pair, seq, V7X_LANES), BF16)],
        compiler_params=_cparams(3),
        name="prompt_attention",
    )(qt, kt, vt, fcol, ft_pairs)


def _decode_step(cj, n_chunks, q_ref, kn_ref, vn_ref, lfn_ref, kc_refs, vc_refs, lf_refs, o_ref, scratch,
                 *, n_heads, n_new, between=None):
    m_sc, l_sc, acc_sc, carry_sc, qbd_sc, qb_sc = scratch
    rows = n_heads * n_new
    feat = n_heads * HEAD_DIM

    def tile_rows(x):
        return jnp.concatenate([x] * n_heads, axis=0)

    def rep_rows(x):
        return jnp.concatenate([jnp.broadcast_to(x[h:h + 1, :], (n_new, x.shape[1])) for h in range(n_heads)], axis=0)

    def to_pages(x):
        xp = jnp.concatenate([x, jnp.zeros((PAGE_SIZE - n_new, feat), F32)], axis=0)
        return jnp.concatenate([xp[:, c * PAGE_SIZE:(c + 1) * PAGE_SIZE].T for c in range(feat // PAGE_SIZE)], axis=0)

    def score_phase(k_pages, bias):
        s = _dot(qbd_sc[...], k_pages) + bias
        m_prev = m_sc[...]
        m_new = jnp.maximum(m_prev, jnp.max(s, axis=1, keepdims=True))
        alpha = jnp.exp(m_prev - m_new)
        p = jnp.exp(s - m_new)
        l_sc[...] = alpha * l_sc[...] + jnp.sum(p, axis=1, keepdims=True)
        m_sc[...] = m_new
        return p, alpha

    def value_phase(v_pages, p, alpha):
        acc_sc[...] = alpha * acc_sc[...] + _dot_nt(p.astype(BF16), v_pages)

    @pl.when(cj == 0)
    def _():
        m_sc[...] = jnp.full(m_sc.shape, NEG_INF, F32)
        l_sc[...] = jnp.zeros(l_sc.shape, F32)
        acc_sc[...] = jnp.zeros(acc_sc.shape, F32)
        carry_sc[...] = jnp.zeros(carry_sc.shape, F32)
        row_head = lax.broadcasted_iota(jnp.int32, (rows, feat), 0) // n_new
        col_head = lax.broadcasted_iota(jnp.int32, (rows, feat), 1) // HEAD_DIM
        qbd_sc[...] = jnp.where(row_head == col_head, tile_rows(q_ref[...] * ATTN_SCALE), 0.0).astype(BF16)
        f_new = _scan_rows(lfn_ref[...])
        rh = lax.broadcasted_iota(jnp.int32, (rows, V7X_LANES), 0) // n_new
        ln = lax.broadcasted_iota(jnp.int32, (rows, V7X_LANES), 1)
        qb = jnp.sum(jnp.where(rh == ln, tile_rows(f_new), 0.0), axis=1, keepdims=True)
        qb_sc[...] = qb
        f_pad = jnp.concatenate([f_new, jnp.zeros((PAGE_SIZE - n_new, V7X_LANES), F32)], axis=0)
        f_new_t = f_pad.T[:n_heads, :]
        tok = lax.broadcasted_iota(jnp.int32, (rows, PAGE_SIZE), 0) % n_new
        pos = lax.broadcasted_iota(jnp.int32, (rows, PAGE_SIZE), 1)
        bias = (qb - rep_rows(f_new_t)) + jnp.where(pos <= tok, 0.0, NEG_INF)
        p, alpha = score_phase(to_pages(kn_ref[...]).astype(BF16), bias)
        value_phase(to_pages(vn_ref[...]).astype(BF16), p, alpha)

    carry = carry_sc[...]
    biases = []
    for lf_ref in lf_refs:
        lf = lf_ref[...]
        incl = _suffix_sum_lanes(lf)
        biases.append(rep_rows((incl - lf) + carry))
        carry = carry + incl[:, 0:1]
    carry_sc[...] = carry
    k_pages = jnp.concatenate([r[...].reshape(feat, PAGE_SIZE).astype(BF16) for r in kc_refs], axis=1)
    p, alpha = score_phase(k_pages, qb_sc[...] + jnp.concatenate(biases, axis=1))
    if between is not None:
        between()
    v_pages = jnp.concatenate([r[...].reshape(feat, PAGE_SIZE).astype(BF16) for r in vc_refs], axis=1)
    value_phase(v_pages, p, alpha)

    @pl.when(cj == n_chunks - 1)
    def _():
        outs = [acc_sc[h * n_new:(h + 1) * n_new, h * HEAD_DIM:(h + 1) * HEAD_DIM] for h in range(n_heads)]
        o_ref[...] = jnp.concatenate(outs, axis=0) / l_sc[...]


def _decode_scratch(n_heads, n_new):
    rows = n_heads * n_new
    feat = n_heads * HEAD_DIM
    return [
        pltpu.VMEM((rows, 1), F32),
        pltpu.VMEM((rows, 1), F32),
        pltpu.VMEM((rows, feat), F32),
        pltpu.VMEM((n_heads, PAGE_SIZE), F32),
        pltpu.VMEM((rows, feat), BF16),
        pltpu.VMEM((rows, 1), F32),
    ]


def _decode_in_specs(seq_of, chunk_of, *, n_heads, n_new, n_pg, n_pages, cols, n_lead):
    feat = n_heads * HEAD_DIM
    last = n_pages - 1

    def tok_spec(width, col):
        return pl.BlockSpec((n_new, width), lambda *a: (seq_of(*a[:n_lead]), col))

    def page_spec(g, tail):
        def idx(*a):
            pt = a[n_lead]
            page = pt[seq_of(*a[:n_lead]), last - (chunk_of(*a[:n_lead]) * n_pg + g)]
            return (0, page) + (0,) * (1 + len(tail))
        return pl.BlockSpec((None, None, n_heads) + tail, idx)

    kv_tail = (HEAD_DIM, PAGE_SIZE)
    specs = [tok_spec(feat, cols["q"] // feat), tok_spec(feat, cols["k"] // feat), tok_spec(feat, cols["v"] // feat),
             tok_spec(V7X_LANES, 0)]
    specs += [page_spec(g, kv_tail) for g in range(n_pg)]
    specs += [page_spec(g, kv_tail) for g in range(n_pg)]
    specs += [page_spec(g, (PAGE_SIZE,)) for g in range(n_pg)]
    return specs


def _h_decode_kernel(pt_ref, xn_ref, w_ref, q_ref, kn_ref, vn_ref, lfn_ref, *rest, n_heads, n_new, n_pg, n_j,
                     n_gate, n_chunks):
    del pt_ref
    kc_refs = rest[0:n_pg]
    vc_refs = rest[n_pg:2 * n_pg]
    lf_refs = rest[2 * n_pg:3 * n_pg]
    hg_ref, u_ref, o_ref = rest[3 * n_pg:3 * n_pg + 3]
    j = pl.program_id(1)
    cj = _divmod_nonneg(pl.program_id(0) * n_j + j, n_chunks)[1]

    def step(store):
        _decode_step(cj, n_chunks, q_ref, kn_ref, vn_ref, lfn_ref, kc_refs, vc_refs, lf_refs, o_ref,
                     rest[3 * n_pg + 3:], n_heads=n_heads, n_new=n_new,
                     between=lambda: store(_dot(xn_ref[...], w_ref[...])))

    def store_gate(acc):
        hg_ref[...] = acc.astype(BF16)

    def store_u(acc):
        u_ref[...] = acc

    pl.when(j < n_gate)(lambda: step(store_gate))
    pl.when(j >= n_gate)(lambda: step(store_u))


def _h_decode(plan, page_table, xn, wbf, hs, lf_new, cache_kt, cache_vt, cache_lft, *, tm, n_heads, n_new, n_pg):
    m, d = xn.shape
    tn = plan["tn"]
    cols = plan["cols"]
    n_j = plan["steps"]["q"][0]
    n_pages = page_table.shape[1]
    n_chunks = n_pages // n_pg
    n_steps = (m // tm) * n_j
    assert n_steps % n_chunks == 0 and n_steps // n_chunks <= page_table.shape[0]
    n_seq = n_steps // n_chunks
    rows = n_heads * n_new
    n_gate = plan["steps"]["u"][0]
    assert plan["steps"]["u"][1] == n_j
    kern = functools.partial(_h_decode_kernel, n_heads=n_heads, n_new=n_new, n_pg=n_pg, n_j=n_j, n_gate=n_gate,
                             n_chunks=n_chunks)
    in_specs = [
        pl.BlockSpec((tm, d), lambda i, j, pt: (i, 0)),
        pl.BlockSpec((None, d, tn), lambda i, j, pt: (j, 0, 0)),
    ] + _decode_in_specs(lambda i, j: _divmod_nonneg(i * n_j + j, n_chunks)[0],
                         lambda i, j: _divmod_nonneg(i * n_j + j, n_chunks)[1],
                         n_heads=n_heads, n_new=n_new, n_pg=n_pg, n_pages=n_pages, cols=cols, n_lead=2)
    grid_spec = pltpu.PrefetchScalarGridSpec(
        num_scalar_prefetch=1,
        grid=(m // tm, n_j),
        in_specs=in_specs,
        out_specs=[
            pl.BlockSpec((tm, tn), lambda i, j, pt: (i, jnp.minimum(j, n_gate - 1))),
            pl.BlockSpec((tm, tn), lambda i, j, pt: (i, jnp.maximum(j - n_gate, 0))),
            pl.BlockSpec((None, rows, HEAD_DIM), lambda i, j, pt: (_divmod_nonneg(i * n_j + j, n_chunks)[0], 0, 0)),
        ],
        scratch_shapes=_decode_scratch(n_heads, n_new),
    )
    args = [page_table, xn, wbf, hs, hs, hs, lf_new] + [cache_kt] * n_pg + [cache_vt] * n_pg + [cache_lft] * n_pg
    hg, u, att_rows = pl.pallas_call(
        kern,
        grid_spec=grid_spec,
        out_shape=[
            jax.ShapeDtypeStruct((m, n_gate * tn), BF16),
            jax.ShapeDtypeStruct((m, (n_j - n_gate) * tn), F32),
            jax.ShapeDtypeStruct((n_seq, rows, HEAD_DIM), F32),
        ],
        compiler_params=_cparams(2),
        name="h_decode",
    )(*args)
    return hg, u, att_rows, n_seq


def _pool_diff(z, u, pos, group_w):
    s = z
    k = 1
    while k < group_w:
        s = s + pltpu.roll(s, k, 0)
        k *= 2
    cnt = jnp.minimum(pos + 1, group_w).astype(F32)
    return s[HALO:, :] / cnt - u


def _mix_tail(d_groups, gp, ga, att, ma, mb, x, wpool_ref, ps_ref, wup_ref, wua_ref, wout_ref):
    mixed = [_dot(d.astype(BF16), wpool_ref[gi]) for gi, d in enumerate(d_groups)]
    mixed = jnp.concatenate(mixed, axis=1)
    branch_a = (mixed * ps_ref[...]) * _silu(gp.astype(F32))
    branch_b = att.astype(F32) * _silu(ga.astype(F32))
    up_a = _dot(branch_a.astype(BF16), wup_ref[...])
    up_b = _dot(branch_b.astype(BF16), wua_ref[...])
    merged = _sigmoid(ma.astype(F32)) * up_a + _sigmoid(mb.astype(F32)) * up_b
    return x + _dot(merged.astype(BF16), wout_ref[...])


def _out_prompt_kernel(pt_ref, u_ref, halo_ref, gp_ref, ga_ref, ma_ref, mb_ref, att_ref, x_ref,
                       wpool_ref, ps_ref, wup_ref, wua_ref, wout_ref, q_ref, kn_ref, vn_ref, lfn_ref, *rest,
                       tm, seq, n_heads, n_new, n_pg):
    del pt_ref
    kc_refs = rest[0:n_pg]
    vc_refs = rest[n_pg:2 * n_pg]
    lf_refs = rest[2 * n_pg:3 * n_pg]
    y_ref, o_ref = rest[3 * n_pg:3 * n_pg + 2]
    i = pl.program_id(0)

    def output_stage():
        pos0 = (i * tm) % seq
        u = u_ref[...]
        halo = jnp.where(pos0 == 0, 0.0, halo_ref[...])
        z = jnp.concatenate([halo, u], axis=0)
        pos = pos0 + lax.broadcasted_iota(jnp.int32, (tm, 1), 0)
        gw = u.shape[1] // len(POOL_WINDOWS)
        d_groups = [_pool_diff(z[:, gi * gw:(gi + 1) * gw], u[:, gi * gw:(gi + 1) * gw], pos, w)
                    for gi, w in enumerate(POOL_WINDOWS)]
        y_ref[...] = _mix_tail(d_groups, gp_ref[...], ga_ref[...], att_ref[...], ma_ref[...], mb_ref[...], x_ref[...],
                               wpool_ref, ps_ref, wup_ref, wua_ref, wout_ref)

    _decode_step(i, pl.num_programs(0), q_ref, kn_ref, vn_ref, lfn_ref, kc_refs, vc_refs, lf_refs, o_ref,
                 rest[3 * n_pg + 2:], n_heads=n_heads, n_new=n_new, between=output_stage)


def _out_sample_kernel(z_ref, gp_ref, ga_ref, ma_ref, mb_ref, att_ref, x_ref,
                       wpool_ref, ps_ref, wup_ref, wua_ref, wout_ref, y_ref, *, dec_b, n_new, pos0):
    gw = z_ref.shape[2] // len(POOL_WINDOWS)
    pos = pos0 + lax.broadcasted_iota(jnp.int32, (n_new, 1), 0)
    per_seq = []
    for b in range(dec_b):
        z = z_ref[b]
        u = z[HALO:, :]
        per_seq.append([_pool_diff(z[:, gi * gw:(gi + 1) * gw], u[:, gi * gw:(gi + 1) * gw], pos, w)
                        for gi, w in enumerate(POOL_WINDOWS)])
    d_groups = [jnp.concatenate([per_seq[b][gi] for b in range(dec_b)], axis=0) for gi in range(len(POOL_WINDOWS))]
    y_ref[...] = _mix_tail(d_groups, gp_ref[...], ga_ref[...], att_ref[...], ma_ref[...], mb_ref[...], x_ref[...],
                           wpool_ref, ps_ref, wup_ref, wua_ref, wout_ref)


def _weight_specs(pw, gw, dm):
    one = pl.Buffered(1)
    return [
        pl.BlockSpec((len(POOL_WINDOWS), gw, gw), lambda *_: (0, 0, 0), pipeline_mode=one),
        pl.BlockSpec((1, pw), lambda *_: (0, 0), pipeline_mode=one),
        pl.BlockSpec((pw, dm), lambda *_: (0, 0), pipeline_mode=one),
        pl.BlockSpec((pw, dm), lambda *_: (0, 0), pipeline_mode=one),
        pl.BlockSpec((dm, dm), lambda *_: (0, 0), pipeline_mode=one),
    ]


def _out_prompt(hg, u, att, x2d, wpool, ps, wup, wua, wout, page_table, hs, lf_new, cache_kt, cache_vt, cache_lft,
                *, seq, tm, cols, n_heads, n_new, decode_seq):
    m, dm = x2d.shape
    pw = att.shape[1]
    gw = pw // len(POOL_WINDOWS)
    hb = tm // HALO
    n_steps = m // tm
    n_pages = page_table.shape[1]
    assert n_pages % n_steps == 0
    n_pg = n_pages // n_steps
    rows = n_heads * n_new
    kern = functools.partial(_out_prompt_kernel, tm=tm, seq=seq, n_heads=n_heads, n_new=n_new, n_pg=n_pg)
    in_specs = [
        pl.BlockSpec((tm, pw), lambda i, pt: (i, 0)),
        pl.BlockSpec((HALO, pw), lambda i, pt: (jnp.maximum(i * hb - 1, 0), 0)),
        pl.BlockSpec((tm, pw), lambda i, pt: (i, cols["gp"] // pw)),
        pl.BlockSpec((tm, pw), lambda i, pt: (i, cols["ga"] // pw)),
        pl.BlockSpec((tm, dm), lambda i, pt: (i, cols["ma"] // dm)),
        pl.BlockSpec((tm, dm), lambda i, pt: (i, cols["mb"] // dm)),
        pl.BlockSpec((tm, pw), lambda i, pt: (i, 0)),
        pl.BlockSpec((tm, dm), lambda i, pt: (i, 0)),
    ] + _weight_specs(pw, gw, dm) + _decode_in_specs(
        lambda i: decode_seq, lambda i: i, n_heads=n_heads, n_new=n_new, n_pg=n_pg, n_pages=n_pages, cols=cols,
        n_lead=1)
    grid_spec = pltpu.PrefetchScalarGridSpec(
        num_scalar_prefetch=1,
        grid=(n_steps,),
        in_specs=in_specs,
        out_specs=[
            pl.BlockSpec((tm, dm), lambda i, pt: (i, 0)),
            pl.BlockSpec((None, rows, HEAD_DIM), lambda i, pt: (0, 0, 0)),
        ],
        scratch_shapes=_decode_scratch(n_heads, n_new),
    )
    args = [page_table, u, u, hg, hg, hg, hg, att, x2d, wpool, ps, wup, wua, wout, hs, hs, hs, lf_new]
    args += [cache_kt] * n_pg + [cache_vt] * n_pg + [cache_lft] * n_pg
    return pl.pallas_call(
        kern,
        grid_spec=grid_spec,
        out_shape=[
            jax.ShapeDtypeStruct((m, dm), F32),
            jax.ShapeDtypeStruct((1, rows, HEAD_DIM), F32),
        ],
        compiler_params=_cparams(1),
        name="out_prompt",
    )(*args)


def _out_sample(z, h, att, x2d, wpool, ps, wup, wua, wout, *, dec_b, n_new, pos0, cols):
    m, dm = x2d.shape
    pw = att.shape[1]
    gw = pw // len(POOL_WINDOWS)
    kern = functools.partial(_out_sample_kernel, dec_b=dec_b, n_new=n_new, pos0=pos0)
    return pl.pallas_call(
        kern,
        grid=(1,),
        in_specs=[
            pl.BlockSpec((dec_b, HALO + n_new, pw), lambda i: (0, 0, 0)),
            pl.BlockSpec((m, pw), lambda i: (0, cols["gp"] // pw)),
            pl.BlockSpec((m, pw), lambda i: (0, cols["ga"] // pw)),
            pl.BlockSpec((m, dm), lambda i: (0, cols["ma"] // dm)),
            pl.BlockSpec((m, dm), lambda i: (0, cols["mb"] // dm)),
            pl.BlockSpec((m, pw), lambda i: (0, 0)),
            pl.BlockSpec((m, dm), lambda i: (0, 0)),
        ] + _weight_specs(pw, gw, dm),
        out_specs=pl.BlockSpec((m, dm), lambda i: (0, 0)),
        out_shape=jax.ShapeDtypeStruct((m, dm), F32),
        compiler_params=_cparams(1),
        name="out_sample",
    )(z, h, h, h, h, att, x2d, wpool, ps, wup, wua, wout)


def kernel(x_prompt, x_sample, cache_k, cache_v, cache_logf, state_pool, page_table, norm_gain, w_in, b_f,
           q_norm_gain, k_norm_gain, w_pool_map, pool_scale, w_up_pool, w_up_attn, w_out):
    batch, seq, dm = x_prompt.shape
    dec_b, n_new, _ = x_sample.shape
    assert w_in.shape[0] == 1
    n_heads = b_f.shape[1]
    aw = n_heads * HEAD_DIM
    pw = w_up_pool.shape[1]
    n_pages = page_table.shape[1]
    past_len = n_pages * PAGE_SIZE
    assert pw == aw and dm == 2 * pw and n_pages % DECODE_PAGES_FUSED == 0

    tn = 512
    plan = _inproj_plan(pw, aw, dm, n_heads, tn)
    cols = plan["cols"]
    w_t = w_in[0].T
    fl0, fl_n = plan["fl_rows"]
    wfl = jnp.pad(w_t[fl0:fl0 + fl_n], ((0, V7X_LANES - n_heads), (0, 0))).astype(BF16)
    bfp = jnp.pad(b_f, ((0, 0), (0, V7X_LANES - n_heads)))
    reps = tn // HEAD_DIM
    qg = jnp.tile(q_norm_gain[0], reps)[None, :]
    kg = jnp.tile(k_norm_gain[0], reps)[None, :]
    qgc = jnp.broadcast_to(jnp.tile(q_norm_gain[0], n_heads)[:, None], (aw, V7X_LANES))
    kgc = jnp.broadcast_to(jnp.tile(k_norm_gain[0], n_heads)[:, None], (aw, V7X_LANES))
    seg = np.arange(tn) // HEAD_DIM
    bd = jnp.asarray((seg[:, None] == seg[None, :]).astype(np.float32)).astype(BF16)
    wpool = w_pool_map[0].astype(BF16)
    wup = w_up_pool[0].astype(BF16)
    wua = w_up_attn[0].astype(BF16)
    wout = w_out[0].astype(BF16)

    ms = dec_b * n_new
    xs2 = x_sample.reshape(ms, dm)
    hs, lfs, w_tok, w_fm = _inproj_new(plan, xs2, norm_gain, w_t, wfl, bfp, qg, kg, bd)

    cache_kt = cache_k.transpose(0, 1, 3, 4, 2)
    cache_vt = cache_v.transpose(0, 1, 3, 4, 2)
    cache_lft = cache_logf.transpose(0, 1, 3, 2)
    decode_args = dict(n_heads=n_heads, n_new=n_new, n_pg=DECODE_PAGES_FUSED)

    xp2 = x_prompt.reshape(batch * seq, dm)
    xn_p, lfp = _prenorm(xp2, norm_gain, wfl, bfp, tm=1024)
    hg_p, u_p, att_rows_a, n_fused = _h_decode(plan, page_table, xn_p, w_tok, hs, lfs, cache_kt, cache_vt,
                                               cache_lft, tm=1024, **decode_args)
    qt_p, kt_p, vt_p = _inproj_fm(plan, xn_p, w_fm, qgc, kgc, tm=1024, seq=seq, width=aw)
    fcol, ft, lft = _fcum(lfp, batch=batch, seq=seq, n_heads=n_heads)
    ft_pairs = ft.reshape(batch, n_heads // 2, 2, seq)
    att_p = _prompt_attention(qt_p, kt_p, vt_p, fcol, ft_pairs, batch=batch, seq=seq, n_heads=n_heads, tq=256,
                              n_pair=ATTN_PAIRS_PER_STEP)
    assert n_fused == dec_b - 1
    yp, att_rows_b = _out_prompt(hg_p, u_p, att_p, xp2, wpool, pool_scale, wup, wua, wout, page_table, hs, lfs,
                                 cache_kt, cache_vt, cache_lft, seq=seq, tm=256, cols=cols, n_heads=n_heads,
                                 n_new=n_new, decode_seq=n_fused)

    k_s = hs[:, cols["k"]:cols["k"] + aw].reshape(dec_b, n_new, n_heads, HEAD_DIM)
    v_s = hs[:, cols["v"]:cols["v"] + aw].reshape(dec_b, n_new, n_heads, HEAD_DIM)
    u_s = hs[:, cols["u"]:cols["u"] + pw].reshape(dec_b, n_new, pw)
    logf_s = lfs[:, :n_heads].reshape(dec_b, n_new, n_heads)
    att_rows = jnp.concatenate([att_rows_a, att_rows_b], axis=0)
    att_s = att_rows.reshape(dec_b, n_heads, n_new, HEAD_DIM).transpose(0, 2, 1, 3).reshape(ms, aw)
    z_s = jnp.concatenate([jnp.zeros((dec_b, HALO - POOL_BUF, pw), F32), state_pool[0], u_s], axis=1)
    ys = _out_sample(z_s, hs, att_s, xs2, wpool, pool_scale, wup, wua, wout,
                     dec_b=dec_b, n_new=n_new, pos0=past_len, cols=cols)

    k_p = kt_p.reshape(1, batch, n_heads, HEAD_DIM, seq).transpose(0, 1, 4, 2, 3)
    v_p = vt_p.reshape(1, batch, n_heads, HEAD_DIM, seq).transpose(0, 1, 4, 2, 3)
    logf_p = lft.transpose(0, 2, 1)[None]
    pool_p = u_p.reshape(batch, seq, pw)[:, seq - POOL_BUF:, :][None]
    pool_s = z_s[:, HALO + n_new - POOL_BUF:, :][None]
    return (yp.reshape(batch, seq, dm), ys.reshape(dec_b, n_new, dm), k_p, v_p, logf_p, pool_p,
            k_s[None], v_s[None], logf_s[None], pool_s)
```
